```python
import jax, jax.numpy as jnp
from jax import lax
import numpy as np

D_MODEL = 2048
BATCH = 2
SEQ = 8192
DEPTH = 4
DEC_BATCH = 16
DEC_SEQ = 64
PAST_LEN = 2048

CHUNK = 64
N_MIXERS = 2
N_RWKV_LAYERS = (DEPTH + 1) // 2
N_GLA_LAYERS = DEPTH // 2
RWKV_HEAD = 64
RWKV_HEADS = D_MODEL // RWKV_HEAD
RWKV_DECAY_LORA = 96
RWKV_AAA_LORA = 96
RWKV_MV_LORA = 64
RWKV_GATE_LORA = 256
RWKV_GN_EPS = 64e-5
GLA_HEADS = 4
GLA_DK = D_MODEL // 2
GLA_DV = D_MODEL
GLA_HEAD_K = GLA_DK // GLA_HEADS
GLA_HEAD_V = GLA_DV // GLA_HEADS
GLA_GATE_RANK = 16
GLA_GATE_NORMALIZER = 16.0
GLA_NORM_EPS = 1e-5
FFN_HIDDEN = 5632
CONV_WIDTH = 3
NORM_EPS = 1e-6

kernel_name = 'rwkv7_gla_convffn_stream_step'


def rmsnorm(x, g):
    xf = x.astype(jnp.float32)
    y = xf * lax.rsqrt(jnp.mean(xf * xf, axis=-1, keepdims=True) + NORM_EPS)
    return y.astype(x.dtype) * g


def _wkv7_step(S, inp):
    r, d, k, v, kk, a = inp
    sa = jnp.einsum('bhvk,bhk->bhv', S, -kk)
    S = S * d[:, :, None, :] + sa[..., None] * (kk * a)[:, :, None, :] + v[..., None] * k[:, :, None, :]
    o = jnp.einsum('bhvk,bhk->bhv', S, r)
    return S, o


def rwkv7_time_mix(x, x_prev, wkv0, v_first, vres, mix, w0, w1, w2, a0, a1, a2,
                   g1, g2, k_k, k_a, r_k, wr, wk, wv, wo, lnx_w, lnx_b):
    B, T, D = x.shape
    H, N = RWKV_HEADS, RWKV_HEAD
    xx = jnp.concatenate([x_prev[:, None, :].astype(x.dtype), x[:, :-1, :]], axis=1) - x
    xr, xw, xk, xv, xa, xg = [x + xx * mix[m] for m in range(6)]
    r = xr @ wr
    w = -jax.nn.softplus(-(w0 + jnp.tanh(xw @ w1) @ w2)) - 0.5
    k = xk @ wk
    v = xv @ wv
    if vres is None:
        v_first = v
    else:
        v0, v1, v2 = vres
        v = v + (v_first - v) * jax.nn.sigmoid(v0 + (xv @ v1) @ v2)
    a = jax.nn.sigmoid(a0 + (xa @ a1) @ a2)
    g = jax.nn.sigmoid(xg @ g1) @ g2
    kk = (k * k_k).reshape(B, T, H, N).astype(jnp.float32)
    kk = kk / jnp.maximum(jnp.sqrt(jnp.sum(kk * kk, axis=-1, keepdims=True)), 1e-12)
    k = k * (1 + (a - 1) * k_a)
    heads = lambda t: t.reshape(B, T, H, N).astype(jnp.float32)
    rh, kh, vh, ah = heads(r), heads(k), heads(v), heads(a)
    dh = jnp.exp(-jnp.exp(heads(w)))
    tmaj = lambda t: jnp.moveaxis(t, 1, 0)
    S, o = lax.scan(_wkv7_step, wkv0.astype(jnp.float32),
                    (tmaj(rh), tmaj(dh), tmaj(kh), tmaj(vh), tmaj(kk), tmaj(ah)))
    o = jnp.moveaxis(o, 0, 1)
    mu = jnp.mean(o, axis=-1, keepdims=True)
    var = jnp.mean(jnp.square(o - mu), axis=-1, keepdims=True)
    o = ((o - mu) * lax.rsqrt(var + RWKV_GN_EPS)).reshape(B, T, D).astype(x.dtype) * lnx_w + lnx_b
    bonus = (jnp.sum(rh * kh * r_k, axis=-1, keepdims=True) * vh).reshape(B, T, D).astype(x.dtype)
    out = ((o + bonus) * g) @ wo
    return out, x[:, -1, :], S.astype(wkv0.dtype), v_first


def gla_chunked(q, k, v, g, S0):
    B, H, T, DK = q.shape
    C = min(CHUNK, T)
    n = T // C
    blk = lambda t: jnp.moveaxis(t.reshape(B, H, n, C, t.shape[-1]), 2, 0)
    causal = jnp.tril(jnp.ones((C, C), dtype=bool))

    def step(S, inp):
        qc, kc, vc, gc = inp
        cum = jnp.cumsum(gc, axis=-2)
        last = cum[:, :, -1:, :]
        qe = qc * jnp.exp(cum)
        ke = kc * jnp.exp(-cum)
        scores = jnp.where(causal, jnp.einsum('bhid,bhjd->bhij', qe, ke), 0.0)
        o = jnp.einsum('bhij,bhjv->bhiv', scores, vc) + jnp.einsum('bhid,bhdv->bhiv', qe, S)
        S = S * jnp.exp(last)[:, :, 0, :, None] + jnp.einsum('bhjd,bhjv->bhdv', kc * jnp.exp(last - cum), vc)
        return S, o

    S, o = lax.scan(step, S0, (blk(q), blk(k), blk(v), blk(g)))
    o = jnp.moveaxis(o, 0, 2).reshape(B, H, T, v.shape[-1])
    return o, S


def gla_time_mix(x, S0, w_in, gk_w2, gk_b, head_norm, wo):
    B, T, D = x.shape
    proj = x @ w_in
    q, k, v, gate, lr = jnp.split(
        proj, [GLA_DK, 2 * GLA_DK, 2 * GLA_DK + GLA_DV, 2 * GLA_DK + 2 * GLA_DV], axis=-1)
    gk = jax.nn.log_sigmoid((lr @ gk_w2 + gk_b).astype(jnp.float32)) / GLA_GATE_NORMALIZER
    hk = lambda t: t.reshape(B, T, GLA_HEADS, GLA_HEAD_K).transpose(0, 2, 1, 3).astype(jnp.float32)
    qh = hk(q) * GLA_HEAD_K ** -0.5
    kh, gh = hk(k), hk(gk)
    vh = v.reshape(B, T, GLA_HEADS, GLA_HEAD_V).transpose(0, 2, 1, 3).astype(jnp.float32)
    o, S = gla_chunked(qh, kh, vh, gh, S0.astype(jnp.float32))
    o = o * lax.rsqrt(jnp.mean(o * o, axis=-1, keepdims=True) + GLA_NORM_EPS)
    o = o.transpose(0, 2, 1, 3).astype(x.dtype) * head_norm
    o = o.reshape(B, T, GLA_DV) * jax.nn.silu(gate)
    return o @ wo, S.astype(S0.dtype)


def conv_ffn(x, conv_state, w_up, conv_w, conv_b, w_down):
    T = x.shape[1]
    u = x @ w_up
    up = jnp.concatenate([conv_state.astype(u.dtype), u], axis=1)
    c = conv_b + sum(conv_w[i] * up[:, i:i + T] for i in range(CONV_WIDTH))
    val, gate = jnp.split(c, 2, axis=-1)
    return (jax.nn.silu(gate) * val) @ w_down, up[:, T:]


def run_trunk(x, shift_st, wkv_st, gla_st, conv_st, prm):
    new_shift, new_wkv, new_gla, new_conv = [], [], [], []
    v_first = None
    for i in range(DEPTH):
        h = rmsnorm(x, prm['norm_mix'][i])
        j = i // N_MIXERS
        if i % N_MIXERS == 0:
            vres = None if j == 0 else (prm['rwkv_v0'][j - 1], prm['rwkv_v1'][j - 1], prm['rwkv_v2'][j - 1])
            out, s_shift, s_wkv, v_first = rwkv7_time_mix(
                h, shift_st[j], wkv_st[j], v_first, vres, prm['rwkv_mix'][j],
                prm['rwkv_w0'][j], prm['rwkv_w1'][j], prm['rwkv_w2'][j],
                prm['rwkv_a0'][j], prm['rwkv_a1'][j], prm['rwkv_a2'][j],
                prm['rwkv_g1'][j], prm['rwkv_g2'][j], prm['rwkv_k_k'][j], prm['rwkv_k_a'][j],
                prm['rwkv_r_k'][j], prm['rwkv_wr'][j], prm['rwkv_wk'][j], prm['rwkv_wv'][j],
                prm['rwkv_wo'][j], prm['rwkv_lnx_w'][j], prm['rwkv_lnx_b'][j])
            new_shift.append(s_shift)
            new_wkv.append(s_wkv)
        else:
            out, s_gla = gla_time_mix(h, gla_st[j], prm['gla_w_in'][j], prm['gla_gk_w2'][j],
                                      prm['gla_gk_b'][j], prm['gla_head_norm'][j], prm['gla_wo'][j])
            new_gla.append(s_gla)
        x = x + out
        h = rmsnorm(x, prm['norm_ffn'][i])
        out, s_conv = conv_ffn(h, conv_st[i], prm['ffn_w_up'][i], prm['ffn_conv_w'][i],
                               prm['ffn_conv_b'][i], prm['ffn_w_down'][i])
        new_conv.append(s_conv)
        x = x + out
    y = rmsnorm(x, prm['norm_final'])
    return y, jnp.stack(new_shift), jnp.stack(new_wkv), jnp.stack(new_gla), jnp.stack(new_conv)


def setup_inputs(seed: int = 0) -> dict:
    key = jax.random.key(seed)
    ks = jax.random.split(key, 40)
    f32 = jnp.float32
    nrm = lambda k, shape, s: jax.random.normal(k, shape, f32) * s
    LA, LB, D, H, N, F2 = N_RWKV_LAYERS, N_GLA_LAYERS, D_MODEL, RWKV_HEADS, RWKV_HEAD, 2 * FFN_HIDDEN
    w_in_cols = 2 * GLA_DK + 2 * GLA_DV + GLA_GATE_RANK
    conv_shift = jnp.zeros((CONV_WIDTH, F2), f32).at[CONV_WIDTH - 1].set(1.0)
    return {
        'x_prompt': nrm(ks[0], (BATCH, SEQ, D), 1.0),
        'x_sample': nrm(ks[1], (DEC_BATCH, DEC_SEQ, D), 1.0),
        'state_rwkv_shift': nrm(ks[2], (LA, DEC_BATCH, D), 1.0),
        'state_rwkv_wkv': nrm(ks[3], (LA, DEC_BATCH, H, N, N), 0.2),
        'state_gla': nrm(ks[4], (LB, DEC_BATCH, GLA_HEADS, GLA_HEAD_K, GLA_HEAD_V), 0.1),
        'state_ffn_conv': nrm(ks[5], (DEPTH, DEC_BATCH, CONV_WIDTH - 1, F2), 1.0),
        'norm_mix': 1.0 + nrm(ks[6], (DEPTH, D), 0.02),
        'norm_ffn': 1.0 + nrm(ks[7], (DEPTH, D), 0.02),
        'norm_final': 1.0 + nrm(ks[8], (D,), 0.02),
        'rwkv_mix': jax.random.uniform(ks[9], (LA, 6, D), f32),
        'rwkv_w0': jax.random.uniform(ks[10], (LA, D), f32, -6.0, -1.0),
        'rwkv_w1': nrm(ks[11], (LA, D, RWKV_DECAY_LORA), D ** -0.5),
        'rwkv_w2': nrm(ks[12], (LA, RWKV_DECAY_LORA, D), 0.1 * RWKV_DECAY_LORA ** -0.5),
        'rwkv_a0': nrm(ks[13], (LA, D), 0.1),
        'rwkv_a1': nrm(ks[14], (LA, D, RWKV_AAA_LORA), D ** -0.5),
        'rwkv_a2': nrm(ks[15], (LA, RWKV_AAA_LORA, D), 0.1 * RWKV_AAA_LORA ** -0.5),
        'rwkv_v0': nrm(ks[16], (LA - 1, D), 0.1),
        'rwkv_v1': nrm(ks[17], (LA - 1, D, RWKV_MV_LORA), D ** -0.5),
        'rwkv_v2': nrm(ks[18], (LA - 1, RWKV_MV_LORA, D), 0.1 * RWKV_MV_LORA ** -0.5),
        'rwkv_g1': nrm(ks[19], (LA, D, RWKV_GATE_LORA), D ** -0.5),
        'rwkv_g2': nrm(ks[20], (LA, RWKV_GATE_LORA, D), RWKV_GATE_LORA ** -0.5),
        'rwkv_k_k': 0.85 + nrm(ks[21], (LA, D), 0.05),
        'rwkv_k_a': 1.0 + nrm(ks[22], (LA, D), 0.05),
        'rwkv_r_k': nrm(ks[23], (LA, H, N), 0.1),
        'rwkv_wr': nrm(ks[24], (LA, D, D), D ** -0.5),
        'rwkv_wk': nrm(ks[25], (LA, D, D), D ** -0.5),
        'rwkv_wv': nrm(ks[26], (LA, D, D), D ** -0.5),
        'rwkv_wo': nrm(ks[27], (LA, D, D), 0.5 * D ** -0.5),
        'rwkv_lnx_w': 1.0 + nrm(ks[28], (LA, D), 0.02),
        'rwkv_lnx_b': nrm(ks[29], (LA, D), 0.01),
        'gla_w_in': nrm(ks[30], (LB, D, w_in_cols), D ** -0.5),
        'gla_gk_w2': nrm(ks[31], (LB, GLA_GATE_RANK, GLA_DK), GLA_GATE_RANK ** -0.5),
        'gla_gk_b': nrm(ks[32], (LB, GLA_DK), 0.1),
        'gla_head_norm': 1.0 + nrm(ks[33], (LB, GLA_HEAD_V), 0.02),
        'gla_wo': nrm(ks[34], (LB, GLA_DV, D), 0.5 * GLA_DV ** -0.5),
        'ffn_w_up': nrm(ks[35], (DEPTH, D, F2), D ** -0.5),
        'ffn_conv_w': conv_shift + nrm(ks[36], (DEPTH, CONV_WIDTH, F2), 0.3),
        'ffn_conv_b': nrm(ks[37], (DEPTH, F2), 0.01),
        'ffn_w_down': nrm(ks[38], (DEPTH, FFN_HIDDEN, D), 0.5 * FFN_HIDDEN ** -0.5),
    }


def reference(x_prompt, x_sample, state_rwkv_shift, state_rwkv_wkv, state_gla, state_ffn_conv,
              norm_mix, norm_ffn, norm_final, rwkv_mix, rwkv_w0, rwkv_w1, rwkv_w2,
              rwkv_a0, rwkv_a1, rwkv_a2, rwkv_v0, rwkv_v1, rwkv_v2, rwkv_g1, rwkv_g2,
              rwkv_k_k, rwkv_k_a, rwkv_r_k, rwkv_wr, rwkv_wk, rwkv_wv, rwkv_wo,
              rwkv_lnx_w, rwkv_lnx_b, gla_w_in, gla_gk_w2, gla_gk_b, gla_head_norm, gla_wo,
              ffn_w_up, ffn_conv_w, ffn_conv_b, ffn_w_down):
    prm = dict(norm_mix=norm_mix, norm_ffn=norm_ffn, norm_final=norm_final,
               rwkv_mix=rwkv_mix, rwkv_w0=rwkv_w0, rwkv_w1=rwkv_w1, rwkv_w2=rwkv_w2,
               rwkv_a0=rwkv_a0, rwkv_a1=rwkv_a1, rwkv_a2=rwkv_a2,
               rwkv_v0=rwkv_v0, rwkv_v1=rwkv_v1, rwkv_v2=rwkv_v2,
               rwkv_g1=rwkv_g1, rwkv_g2=rwkv_g2, rwkv_k_k=rwkv_k_k, rwkv_k_a=rwkv_k_a,
               rwkv_r_k=rwkv_r_k, rwkv_wr=rwkv_wr, rwkv_wk=rwkv_wk, rwkv_wv=rwkv_wv,
               rwkv_wo=rwkv_wo, rwkv_lnx_w=rwkv_lnx_w, rwkv_lnx_b=rwkv_lnx_b,
               gla_w_in=gla_w_in, gla_gk_w2=gla_gk_w2, gla_gk_b=gla_gk_b,
               gla_head_norm=gla_head_norm, gla_wo=gla_wo,
               ffn_w_up=ffn_w_up, ffn_conv_w=ffn_conv_w, ffn_conv_b=ffn_conv_b,
               ffn_w_down=ffn_w_down)
    bp, dt = x_prompt.shape[0], x_prompt.dtype
    zero_shift = jnp.zeros((N_RWKV_LAYERS, bp, D_MODEL), dt)
    zero_wkv = jnp.zeros((N_RWKV_LAYERS, bp, RWKV_HEADS, RWKV_HEAD, RWKV_HEAD), dt)
    zero_gla = jnp.zeros((N_GLA_LAYERS, bp, GLA_HEADS, GLA_HEAD_K, GLA_HEAD_V), dt)
    zero_conv = jnp.zeros((DEPTH, bp, CONV_WIDTH - 1, 2 * FFN_HIDDEN), dt)
    y_prompt, p_shift, p_wkv, p_gla, p_conv = run_trunk(
        x_prompt, zero_shift, zero_wkv, zero_gla, zero_conv, prm)
    y_sample, s_shift, s_wkv, s_gla, s_conv = run_trunk(
        x_sample, state_rwkv_shift, state_rwkv_wkv, state_gla, state_ffn_conv, prm)
    return (y_prompt, y_sample, p_shift, p_wkv, p_gla, p_conv, s_shift, s_wkv, s_gla, s_conv)
```

```python
import functools

import jax
import jax.numpy as jnp
from jax import lax
from jax.experimental import pallas as pl
from jax.experimental.pallas import tpu as pltpu

F32 = jnp.float32
BF16 = jnp.bfloat16

CHUNK = 64
RWKV_HEAD = 64
RWKV_GN_EPS = 64e-5
GLA_HEADS = 4
GLA_GATE_RANK = 16
GLA_GATE_NORMALIZER = 16.0
GLA_NORM_EPS = 1e-5
NORM_EPS = 1e-6

LANES = 128
MXU_DIM = 256
HEADS_PER_GROUP = MXU_DIM // RWKV_HEAD
VMEM_LIMIT = 56 * 1024 * 1024


def _params(sem):
    return pltpu.CompilerParams(dimension_semantics=sem, vmem_limit_bytes=VMEM_LIMIT)


def _dot(a, b):
    return jnp.dot(a, b, preferred_element_type=F32)


def _dot_nt(a, b):
    return lax.dot_general(a, b, (((1,), (1,)), ((), ())), preferred_element_type=F32)


def _dot_tn(a, b):
    return lax.dot_general(a, b, (((0,), (0,)), ((), ())), preferred_element_type=F32)


def _split_dot(fixed_bf16, x, terms):
    acc = None
    rem = x
    for _ in range(terms):
        piece = rem.astype(BF16)
        rem = rem - piece.astype(F32)
        part = _dot(fixed_bf16, piece)
        acc = part if acc is None else acc + part
    return acc


def _split_dot_right(x, fixed_bf16, terms):
    acc = None
    rem = x
    for _ in range(terms):
        piece = rem.astype(BF16)
        rem = rem - piece.astype(F32)
        part = _dot(piece, fixed_bf16)
        acc = part if acc is None else acc + part
    return acc


def _log_sigmoid(x):
    return jnp.minimum(x, 0.0) - jnp.log(1.0 + jnp.exp(-jnp.abs(x)))


def _sigmoid(x):
    return 1.0 / (1.0 + jnp.exp(-x))


def _silu(x):
    return x * _sigmoid(x)


def _rms(x, eps):
    return x * lax.rsqrt(jnp.mean(x * x, axis=-1, keepdims=True) + eps)


def _tri_incl(n):
    r = lax.broadcasted_iota(jnp.int32, (n, n), 0)
    c = lax.broadcasted_iota(jnp.int32, (n, n), 1)
    return c <= r


def _rmsnorm_kernel(x_ref, g_ref, o_ref):
    o_ref[...] = (_rms(x_ref[...], NORM_EPS) * g_ref[...]).astype(o_ref.dtype)


def _rmsnorm(x, g, out_dtype):
    m, d = x.shape
    tm = min(m, 1024)
    return pl.pallas_call(
        _rmsnorm_kernel,
        grid=(m // tm,),
        in_specs=[pl.BlockSpec((tm, d), lambda i: (i, 0)), pl.BlockSpec((1, d), lambda i: (0, 0))],
        out_specs=pl.BlockSpec((tm, d), lambda i: (i, 0)),
        out_shape=jax.ShapeDtypeStruct((m, d), out_dtype),
        compiler_params=_params(("arbitrary",)),
        name="rmsnorm",
    )(x, g.reshape(1, d))


def _mm_kernel(a_ref, b_ref, *rest, epilogue, n_extra):
    o_ref = rest[n_extra]
    acc = _dot(a_ref[...], b_ref[...])
    if epilogue is not None:
        acc = epilogue(acc, *[e[...] for e in rest[:n_extra]])
    o_ref[...] = acc.astype(o_ref.dtype)


def _matmul(a, b, extras=(), epilogue=None, out_dtype=F32, tm=1024, tn=1024, name="matmul"):
    m, k = a.shape
    n = b.shape[1]
    tm, tn = min(tm, m), min(tn, n)
    assert m % tm == 0 and n % tn == 0, (m, n, tm, tn)
    in_specs = [pl.BlockSpec((tm, k), lambda i, j: (i, 0)), pl.BlockSpec((k, tn), lambda i, j: (0, j))]
    for e in extras:
        if e.shape[0] == 1:
            in_specs.append(pl.BlockSpec((1, tn), lambda i, j: (0, j)))
        else:
            in_specs.append(pl.BlockSpec((tm, tn), lambda i, j: (i, j)))
    return pl.pallas_call(
        functools.partial(_mm_kernel, epilogue=epilogue, n_extra=len(extras)),
        grid=(m // tm, n // tn),
        in_specs=in_specs,
        out_specs=pl.BlockSpec((tm, tn), lambda i, j: (i, j)),
        out_shape=jax.ShapeDtypeStruct((m, n), out_dtype),
        compiler_params=_params(("arbitrary", "arbitrary")),
        name=name,
    )(a, b, *extras)


def _rwkv_pre_kernel(x_ref, st_ref, g_ref, mix_ref, *rest):
    outs, hl_ref, carry_ref = rest[:6], rest[6], rest[7]
    tt = x_ref.shape[0]

    @pl.when(pl.program_id(1) == 0)
    def _():
        carry_ref[...] = st_ref[...]

    h = _rms(x_ref[...], NORM_EPS) * g_ref[...]
    row = lax.broadcasted_iota(jnp.int32, (tt, 1), 0)
    h_prev = jnp.where(row == 0, carry_ref[...], pltpu.roll(h, 1, axis=0))
    carry_ref[...] = h[tt - 1:tt, :]
    xx = h_prev - h
    for i, o_ref in enumerate(outs):
        o_ref[...] = (h + xx * mix_ref[i:i + 1, :]).astype(o_ref.dtype)
    hl_ref[...] = h[tt - 8:, :]


def _rwkv_pre(x, shift_state, g, mix):
    b, t, d = x.shape
    tt = min(t, 512)
    row_spec = pl.BlockSpec((None, tt, d), lambda i, j: (i, j, 0))
    outs = pl.pallas_call(
        _rwkv_pre_kernel,
        grid=(b, t // tt),
        in_specs=[row_spec,
                  pl.BlockSpec((None, 1, d), lambda i, j: (i, 0, 0)),
                  pl.BlockSpec((1, d), lambda i, j: (0, 0)),
                  pl.BlockSpec((6, d), lambda i, j: (0, 0))],
        out_specs=[row_spec] * 6 + [pl.BlockSpec((None, 8, d), lambda i, j: (i, 0, 0))],
        out_shape=[jax.ShapeDtypeStruct((b, t, d), BF16)] * 6 + [jax.ShapeDtypeStruct((b, 8, d), F32)],
        scratch_shapes=[pltpu.VMEM((1, d), F32)],
        compiler_params=_params(("arbitrary", "arbitrary")),
        name="rwkv_pre",
    )(x, shift_state.reshape(b, 1, d), g.reshape(1, d), mix)
    return outs[:6], outs[6][:, 7, :]


def _wkv7_kernel(r_ref, w_ref, k_ref, v_ref, a_ref, g_ref, kk_ref, ka_ref, rk_ref, lw_ref, lb_ref,
                 s0_ref, y_ref, s_ref):
    c_len, d = r_ref.shape
    n_groups = d // MXU_DIM

    @pl.when(pl.program_id(1) == 0)
    def _():
        s_ref[...] = s0_ref[...]

    ri = lax.broadcasted_iota(jnp.int32, (MXU_DIM, MXU_DIM), 0)
    ci = lax.broadcasted_iota(jnp.int32, (MXU_DIM, MXU_DIM), 1)
    head_bits = RWKV_HEAD.bit_length() - 1
    same_head = (ri >> head_bits) == (ci >> head_bits)
    strict = same_head & (ci < ri)
    incl = same_head & (ci <= ri)
    bd = jnp.where(same_head, 1.0, 0.0)
    bd_b = bd.astype(BF16)
    eye = jnp.where(ri == ci, 1.0, 0.0)
    tri_b = jnp.where(_tri_incl(c_len), 1.0, 0.0).astype(BF16)

    def stack(x):
        return jnp.concatenate([x] * HEADS_PER_GROUP, axis=0)

    def stack_masked(x):
        return stack(x) * bd

    def head_sum(x):
        return _split_dot_right(x, bd_b, 2)

    for grp in range(n_groups):
        sl = slice(grp * MXU_DIM, (grp + 1) * MXU_DIM)
        r, w, k, v, a = r_ref[:, sl], w_ref[:, sl], k_ref[:, sl], v_ref[:, sl], a_ref[:, sl]
        kk = k * kk_ref[:, sl]
        kk = kk / jnp.maximum(jnp.sqrt(head_sum(kk * kk)), 1e-12)
        kmod = k * (1.0 + (a - 1.0) * ka_ref[:, sl])
        lw = -jnp.exp(w)
        cum = _split_dot(tri_b, lw, 3)
        last = cum[c_len - 1:c_len, :]
        e_neg = jnp.exp(-cum)
        e_last = jnp.exp(last - cum)
        kka = kk * a
        a_t = -kk * jnp.exp(cum - lw)
        r_t = r * jnp.exp(cum)

        ar = jnp.concatenate([stack_masked(a_t), stack_masked(r_t)], axis=0).astype(BF16)
        bk = jnp.concatenate([stack(kka * e_neg), stack(kmod * e_neg)], axis=0).astype(BF16)
        z = _dot_nt(ar, bk)
        a_ab = jnp.where(strict, z[:MXU_DIM, :MXU_DIM], 0.0)
        a_ak = jnp.where(strict, z[:MXU_DIM, MXU_DIM:], 0.0)
        a_r = jnp.concatenate([jnp.where(incl, z[MXU_DIM:, :MXU_DIM], 0.0),
                               jnp.where(incl, z[MXU_DIM:, MXU_DIM:], 0.0)], axis=1).astype(BF16)

        tinv = eye + a_ab
        p_b = a_ab.astype(BF16)
        p = _dot(p_b, p_b)
        for _ in range(4):
            p_b = p.astype(BF16)
            x = _dot(p_b, jnp.concatenate([p_b, tinv.astype(BF16)], axis=1))
            p = x[:, :MXU_DIM]
            tinv = tinv + x[:, MXU_DIM:]
        tinv = tinv + _dot(p.astype(BF16), tinv.astype(BF16))

        v_st = stack_masked(v)
        s = s_ref[grp]
        y = _dot_nt(ar, s.astype(BF16))
        wm = y[:MXU_DIM] + _dot(a_ak.astype(BF16), v_st.astype(BF16))
        u = _dot(tinv.astype(BF16), wm.astype(BF16))
        uv = jnp.concatenate([u, v_st], axis=0).astype(BF16)
        o_st = y[MXU_DIM:] + _dot(a_r, uv)
        bk_hat = jnp.concatenate([stack_masked(kka * e_last), stack_masked(kmod * e_last)],
                                 axis=0).astype(BF16)
        s_ref[grp] = s * jnp.exp(last) + _dot_tn(uv, bk_hat)

        o = o_st[0:c_len]
        for hd in range(1, HEADS_PER_GROUP):
            o = o + o_st[hd * c_len:(hd + 1) * c_len]

        inv_n = 1.0 / RWKV_HEAD
        mu = head_sum(o) * inv_n
        dev = o - mu
        var = head_sum(dev * dev) * inv_n
        o_n = dev * lax.rsqrt(var + RWKV_GN_EPS) * lw_ref[:, sl] + lb_ref[:, sl]
        bonus = head_sum(r * kmod * rk_ref[:, sl]) * v
        y_ref[:, sl] = ((o_n + bonus) * g_ref[:, sl]).astype(y_ref.dtype)


def _wkv7(r, w, k, v, a, g, k_k, k_a, r_k, lnx_w, lnx_b, s0_bd):
    b, t, d = r.shape
    n_groups = d // MXU_DIM
    row_spec = pl.BlockSpec((None, CHUNK, d), lambda i, j: (i, j, 0))
    vec_spec = pl.BlockSpec((1, d), lambda i, j: (0, 0))
    st_spec = pl.BlockSpec((None, n_groups, MXU_DIM, MXU_DIM), lambda i, j: (i, 0, 0, 0))
    vec = lambda p: p.reshape(1, d)
    return pl.pallas_call(
        _wkv7_kernel,
        grid=(b, t // CHUNK),
        in_specs=[row_spec] * 6 + [vec_spec] * 5 + [st_spec],
        out_specs=[row_spec, st_spec],
        out_shape=[jax.ShapeDtypeStruct((b, t, d), BF16),
                   jax.ShapeDtypeStruct(s0_bd.shape, F32)],
        compiler_params=_params(("arbitrary", "arbitrary")),
        name="wkv7",
    )(r, w, k, v, a, g, vec(k_k), vec(k_a), vec(r_k), vec(lnx_w), vec(lnx_b), s0_bd)


def _wkv_state_to_blockdiag(s):
    b, h, n, _ = s.shape
    g = h // HEADS_PER_GROUP
    s = s.reshape(b, g, HEADS_PER_GROUP, n, n)
    eye = jnp.eye(HEADS_PER_GROUP, dtype=s.dtype)
    return jnp.einsum("bghvk,hj->bghvjk", s, eye).reshape(b, g, MXU_DIM, MXU_DIM)


def _wkv_state_from_blockdiag(s_bd):
    b, g = s_bd.shape[:2]
    n = RWKV_HEAD
    s = s_bd.reshape(b, g, HEADS_PER_GROUP, n, HEADS_PER_GROUP, n)
    s = jnp.stack([s[:, :, i, :, i, :] for i in range(HEADS_PER_GROUP)], axis=2)
    return s.reshape(b, g * HEADS_PER_GROUP, n, n)


def _gla_kernel(q_ref, k_ref, v_ref, gate_ref, lr_ref, w2_ref, gb_ref, hn_ref, s0_ref, y_ref, s_ref):
    tc, dk = q_ref.shape
    n_chunks = tc // CHUNK

    @pl.when(pl.program_id(2) == 0)
    def _():
        s_ref[...] = s0_ref[...]

    tri = _tri_incl(CHUNK)
    tri_b = jnp.where(tri, 1.0, 0.0).astype(BF16)
    scale = dk ** -0.5
    for c in range(n_chunks):
        rows = slice(c * CHUNK, (c + 1) * CHUNK)
        q, k, v = q_ref[rows, :], k_ref[rows, :], v_ref[rows, :]
        logits = _dot(lr_ref[rows, :].astype(BF16), w2_ref[...]) + gb_ref[...]
        gk = _log_sigmoid(logits) * (1.0 / GLA_GATE_NORMALIZER)
        cum = _split_dot(tri_b, gk, 3)
        last = cum[CHUNK - 1:CHUNK, :]
        qe = (q * scale * jnp.exp(cum)).astype(BF16)
        ke = (k * jnp.exp(-cum)).astype(BF16)
        k2 = (k * jnp.exp(last - cum)).astype(BF16)
        vb = v.astype(BF16)
        scores = jnp.where(tri, _dot_nt(qe, ke), 0.0)
        s_t = s_ref[...]
        o = _dot(scores.astype(BF16), vb) + _dot_nt(qe, s_t.astype(BF16))
        s_ref[...] = s_t * jnp.exp(last) + _dot_tn(vb, k2)
        o = _rms(o, GLA_NORM_EPS) * hn_ref[...]
        y_ref[rows, :] = (o * _silu(gate_ref[rows, :])).astype(y_ref.dtype)


def _gla(qkvg, lr, gk_w2, gk_b, head_norm, s0_t):
    b, t, _ = qkvg.shape
    h, dv, dk = s0_t.shape[1:]
    tc = min(t, 4 * CHUNK)
    kq, kv = (h * dk) // dk, (2 * h * dk) // dv
    st_spec = pl.BlockSpec((None, None, dv, dk), lambda i, j, c: (i, j, 0, 0))
    return pl.pallas_call(
        _gla_kernel,
        grid=(b, h, t // tc),
        in_specs=[pl.BlockSpec((None, tc, dk), lambda i, j, c: (i, c, j)),
                  pl.BlockSpec((None, tc, dk), lambda i, j, c: (i, c, kq + j)),
                  pl.BlockSpec((None, tc, dv), lambda i, j, c: (i, c, kv + j)),
                  pl.BlockSpec((None, tc, dv), lambda i, j, c: (i, c, kv + h + j)),
                  pl.BlockSpec((None, tc, LANES), lambda i, j, c: (i, c, 0)),
                  pl.BlockSpec((LANES, dk), lambda i, j, c: (0, j)),
                  pl.BlockSpec((1, dk), lambda i, j, c: (0, j)),
                  pl.BlockSpec((1, dv), lambda i, j, c: (0, 0)),
                  st_spec],
        out_specs=[pl.BlockSpec((None, tc, dv), lambda i, j, c: (i, c, j)), st_spec],
        out_shape=[jax.ShapeDtypeStruct((b, t, h * dv), BF16),
                   jax.ShapeDtypeStruct(s0_t.shape, F32)],
        compiler_params=_params(("arbitrary", "arbitrary", "arbitrary")),
        name="gla",
    )(qkvg, qkvg, qkvg, qkvg, lr, gk_w2, gk_b.reshape(1, -1), head_norm.reshape(1, dv), s0_t)


def _ffn_kernel(x_ref, g_ref, sv_ref, sg_ref, wv_ref, wg_ref, cwv_ref, cwg_ref, cbv_ref, cbg_ref,
                wd_ref, o_ref, nsv_ref, nsg_ref, h_ref, carry_ref):
    nseq, seq_len, d = x_ref.shape
    rows = nseq * seq_len
    tn = wv_ref.shape[1]
    j = pl.program_id(2)

    @pl.when(j == 0)
    def _():
        x = x_ref[...].reshape(rows, d)
        h_ref[...] = (_rms(x, NORM_EPS) * g_ref[...]).astype(BF16)
        o_ref[...] = x_ref[...]

    @pl.when(pl.program_id(1) == 0)
    def _():
        carry_ref[2 * j] = sv_ref[...]
        carry_ref[2 * j + 1] = sg_ref[...]

    h = h_ref[...]
    tpos = lax.broadcasted_iota(jnp.int32, (1, seq_len, 1), 1)

    def conv(w_ref, cw_ref, cb_ref, slot, ns_ref):
        u = _dot(h, w_ref[...]).reshape(nseq, seq_len, tn)
        p2, p1 = carry_ref[slot, :, 0:1, :], carry_ref[slot, :, 1:2, :]
        u1 = jnp.where(tpos == 0, p1, pltpu.roll(u, 1, axis=1))
        u2 = jnp.where(tpos == 0, p2, jnp.where(tpos == 1, p1, pltpu.roll(u, 2, axis=1)))
        tail = u[:, seq_len - 2:, :]
        carry_ref[slot] = tail
        ns_ref[...] = tail
        cw = cw_ref[...]
        return cb_ref[...] + cw[0:1, :] * u2 + cw[1:2, :] * u1 + cw[2:3, :] * u

    val = conv(wv_ref, cwv_ref, cbv_ref, 2 * j, nsv_ref)
    gate = conv(wg_ref, cwg_ref, cbg_ref, 2 * j + 1, nsg_ref)
    act = (_silu(gate) * val).reshape(rows, tn).astype(BF16)
    o_ref[...] += _dot(act, wd_ref[...]).reshape(nseq, seq_len, d)


def _conv_ffn(x, conv_state, g, w_up, conv_w, conv_b, w_down, nseq, seq_len, tn=512):
    b, t, d = x.shape
    f = w_down.shape[0]
    nj = f // tn
    assert b % nseq == 0 and t % seq_len == 0 and f % tn == 0
    x_spec = pl.BlockSpec((nseq, seq_len, d), lambda i, s, j: (i, s, 0))
    st_v = pl.BlockSpec((nseq, 2, tn), lambda i, s, j: (i, 0, j))
    st_g = pl.BlockSpec((nseq, 2, tn), lambda i, s, j: (i, 0, nj + j))
    n_s = t // seq_len
    tail_spec = pl.BlockSpec((nseq, None, 2, tn), lambda i, s, j: (i, s, 0, j))
    col_v = lambda n: pl.BlockSpec((n, tn), lambda i, s, j: (0, j))
    col_g = lambda n: pl.BlockSpec((n, tn), lambda i, s, j: (0, nj + j))
    y, ns_v, ns_g = pl.pallas_call(
        _ffn_kernel,
        grid=(b // nseq, n_s, nj),
        in_specs=[x_spec, pl.BlockSpec((1, d), lambda i, s, j: (0, 0)), st_v, st_g,
                  col_v(d), col_g(d), col_v(3), col_g(3), col_v(1), col_g(1),
                  pl.BlockSpec((tn, d), lambda i, s, j: (j, 0))],
        out_specs=[x_spec, tail_spec, tail_spec],
        out_shape=[jax.ShapeDtypeStruct((b, t, d), F32),
                   jax.ShapeDtypeStruct((b, n_s, 2, f), F32),
                   jax.ShapeDtypeStruct((b, n_s, 2, f), F32)],
        scratch_shapes=[pltpu.VMEM((nseq * seq_len, d), BF16),
                        pltpu.VMEM((2 * nj, nseq, 2, tn), F32)],
        compiler_params=_params(("arbitrary", "arbitrary", "arbitrary")),
        name="conv_ffn",
    )(x, g.reshape(1, d), conv_state, conv_state, w_up, w_up, conv_w, conv_w,
      conv_b.reshape(1, -1), conv_b.reshape(1, -1), w_down)
    return y, jnp.concatenate([ns_v[:, -1], ns_g[:, -1]], axis=-1)


def _pad_cols(w, n):
    return jnp.pad(w, ((0, 0), (0, n - w.shape[1])))


def _pad_rows(w, n):
    return jnp.pad(w, ((0, n - w.shape[0]), (0, 0)))


def _rwkv_layer(x, shift_st, wkv_st, v_first, p):
    b, t, d = x.shape
    m = b * t
    (xr, xw, xk, xv, xa, xg), new_shift = _rwkv_pre(x, shift_st, p["norm"], p["mix"])
    flat = lambda z: z.reshape(m, d)
    row = lambda z: z.reshape(1, d)
    r = _matmul(flat(xr), p["wr"], name="rwkv_r")
    k = _matmul(flat(xk), p["wk"], name="rwkv_k")
    v = _matmul(flat(xv), p["wv"], name="rwkv_v")
    w_lora = _matmul(flat(xw), p["w1"], epilogue=jnp.tanh, out_dtype=BF16, name="rwkv_w1")
    w = _matmul(w_lora, p["w2"], extras=(row(p["w0"]),),
                epilogue=lambda acc, w0: _log_sigmoid(w0 + acc) - 0.5, name="rwkv_w2")
    a_lora = _matmul(flat(xa), p["a1"], out_dtype=BF16, name="rwkv_a1")
    a = _matmul(a_lora, p["a2"], extras=(row(p["a0"]),),
                epilogue=lambda acc, a0: _sigmoid(a0 + acc), name="rwkv_a2")
    g_lora = _matmul(flat(xg), p["g1"], epilogue=_sigmoid, out_dtype=BF16, name="rwkv_g1")
    g = _matmul(g_lora, p["g2"], name="rwkv_g2")
    if p["v1"] is None:
        v_first = v
    else:
        v_lora = _matmul(flat(xv), p["v1"], out_dtype=BF16, name="rwkv_v1")
        v = _matmul(v_lora, p["v2"], extras=(row(p["v0"]), v, v_first), tn=512,
                    epilogue=lambda acc, v0, vv, vf: vv + (vf - vv) * _sigmoid(v0 + acc),
                    name="rwkv_v2")
    seq = lambda z: z.reshape(b, t, d)
    y, s_bd = _wkv7(seq(r), seq(w), seq(k), seq(v), seq(a), seq(g), p["k_k"], p["k_a"], p["r_k"],
                    p["lnx_w"], p["lnx_b"], _wkv_state_to_blockdiag(wkv_st))
    x_new = _matmul(flat(y), p["wo"], extras=(flat(x),), tn=512,
                    epilogue=lambda acc, res: res + acc, name="rwkv_o")
    return seq(x_new), new_shift, _wkv_state_from_blockdiag(s_bd), v_first


def _gla_layer(x, gla_st, p):
    b, t, d = x.shape
    m = b * t
    h = _rmsnorm(x.reshape(m, d), p["norm"], BF16)
    qkvg = _matmul(h, p["w_main"], name="gla_in")
    lr = _matmul(h, p["w_lr"], name="gla_lr")
    y, s_t = _gla(qkvg.reshape(b, t, -1), lr.reshape(b, t, LANES), p["gk_w2"], p["gk_b"],
                  p["head_norm"], jnp.swapaxes(gla_st, -1, -2))
    x_new = _matmul(y.reshape(m, -1), p["wo"], extras=(x.reshape(m, d),), tn=512,
                    epilogue=lambda acc, res: res + acc, name="gla_o")
    return x_new.reshape(b, t, d), jnp.swapaxes(s_t, -1, -2)


FFN_TILE_ROWS = 512


def _ffn_tile(b, t):
    if t >= FFN_TILE_ROWS:
        return 1, FFN_TILE_ROWS
    return min(b, FFN_TILE_ROWS // t), t


def _run_trunk(x, shift_st, wkv_st, gla_st, conv_st, layers, norm_final, ffn_nseq, ffn_len):
    new_shift, new_wkv, new_gla, new_conv = [], [], [], []
    v_first = None
    for i, (mixer, ffn) in enumerate(layers):
        j = i // 2
        if i % 2 == 0:
            x, s_shift, s_wkv, v_first = _rwkv_layer(x, shift_st[j], wkv_st[j], v_first, mixer)
            new_shift.append(s_shift)
            new_wkv.append(s_wkv)
        else:
            x, s_gla = _gla_layer(x, gla_st[j], mixer)
            new_gla.append(s_gla)
        x, s_conv = _conv_ffn(x, conv_st[i], ffn["norm"], ffn["w_up"], ffn["conv_w"], ffn["conv_b"],
                              ffn["w_down"], ffn_nseq, ffn_len)
        new_conv.append(s_conv)
    b, t, d = x.shape
    y = _rmsnorm(x.reshape(b * t, d), norm_final, F32).reshape(b, t, d)
    return y, jnp.stack(new_shift), jnp.stack(new_wkv), jnp.stack(new_gla), jnp.stack(new_conv)


def kernel(x_prompt, x_sample, state_rwkv_shift, state_rwkv_wkv, state_gla, state_ffn_conv, norm_mix, norm_ffn, norm_final, rwkv_mix, rwkv_w0, rwkv_w1, rwkv_w2, rwkv_a0, rwkv_a1, rwkv_a2, rwkv_v0, rwkv_v1, rwkv_v2, rwkv_g1, rwkv_g2, rwkv_k_k, rwkv_k_a, rwkv_r_k, rwkv_wr, rwkv_wk, rwkv_wv, rwkv_wo, rwkv_lnx_w, rwkv_lnx_b, gla_w_in, gla_gk_w2, gla_gk_b, gla_head_norm, gla_wo, ffn_w_up, ffn_conv_w, ffn_conv_b, ffn_w_down):
    depth = norm_mix.shape[0]
    d = x_prompt.shape[-1]
    bf = lambda w: w.astype(BF16)
    dk_total = gla_gk_w2.shape[-1]
    n_main = gla_w_in.shape[-1] - GLA_GATE_RANK
    layers = []
    for i in range(depth):
        j = i // 2
        if i % 2 == 0:
            has_vres = j > 0
            mixer = dict(
                norm=norm_mix[i], mix=rwkv_mix[j],
                wr=bf(rwkv_wr[j]), wk=bf(rwkv_wk[j]), wv=bf(rwkv_wv[j]), wo=bf(rwkv_wo[j]),
                w0=rwkv_w0[j], w1=bf(_pad_cols(rwkv_w1[j], LANES)), w2=bf(_pad_rows(rwkv_w2[j], LANES)),
                a0=rwkv_a0[j], a1=bf(_pad_cols(rwkv_a1[j], LANES)), a2=bf(_pad_rows(rwkv_a2[j], LANES)),
                g1=bf(rwkv_g1[j]), g2=bf(rwkv_g2[j]),
                v0=rwkv_v0[j - 1] if has_vres else None,
                v1=bf(_pad_cols(rwkv_v1[j - 1], LANES)) if has_vres else None,
                v2=bf(_pad_rows(rwkv_v2[j - 1], LANES)) if has_vres else None,
                k_k=rwkv_k_k[j], k_a=rwkv_k_a[j], r_k=rwkv_r_k[j].reshape(d),
                lnx_w=rwkv_lnx_w[j], lnx_b=rwkv_lnx_b[j])
        else:
            mixer = dict(
                norm=norm_mix[i],
                w_main=bf(gla_w_in[j][:, :n_main]),
                w_lr=bf(_pad_cols(gla_w_in[j][:, n_main:], LANES)),
                gk_w2=bf(_pad_rows(gla_gk_w2[j], LANES)), gk_b=gla_gk_b[j],
                head_norm=gla_head_norm[j], wo=bf(gla_wo[j]))
        ffn = dict(norm=norm_ffn[i], w_up=bf(ffn_w_up[i]), conv_w=ffn_conv_w[i], conv_b=ffn_conv_b[i],
                   w_down=bf(ffn_w_down[i]))
        layers.append((mixer, ffn))

    bp, tp, _ = x_prompt.shape
    bs, ts, _ = x_sample.shape
    zeros_like_state = lambda s: jnp.zeros((s.shape[0], bp) + s.shape[2:], s.dtype)
    out_p = _run_trunk(x_prompt, zeros_like_state(state_rwkv_shift), zeros_like_state(state_rwkv_wkv),
                       zeros_like_state(state_gla), zeros_like_state(state_ffn_conv), layers,
                       norm_final, *_ffn_tile(bp, tp))
    out_s = _run_trunk(x_sample, state_rwkv_shift, state_rwkv_wkv, state_gla, state_ffn_conv, layers,
                       norm_final, *_ffn_tile(bs, ts))
    return (out_p[0], out_s[0]) + tuple(out_p[1:]) + tuple(out_s[1:])
```

```python
import functools

import jax
import jax.numpy as jnp
from jax import lax
from jax.experimental import pallas as pl
from jax.experimental.pallas import tpu as pltpu

F32 = jnp.float32
BF16 = jnp.bfloat16

CHUNK = 64
RWKV_HEAD = 64
RWKV_GN_EPS = 64e-5
GLA_HEADS = 4
GLA_GATE_RANK = 16
GLA_GATE_NORMALIZER = 16.0
GLA_NORM_EPS = 1e-5
NORM_EPS = 1e-6

LANES = 128
MXU_DIM = 256
HEADS_PER_GROUP = MXU_DIM // RWKV_HEAD
VMEM_LIMIT = 56 * 1024 * 1024


def _params(sem):
    return pltpu.CompilerParams(dimension_semantics=sem, vmem_limit_bytes=VMEM_LIMIT)


def _dot(a, b):
    return jnp.dot(a, b, preferred_element_type=F32)


def _dot_nt(a, b):
    return lax.dot_general(a, b, (((1,), (1,)), ((), ())), preferred_element_type=F32)


def _dot_tn(a, b):
    return lax.dot_general(a, b, (((0,), (0,)), ((), ())), preferred_element_type=F32)


def _split_dot(fixed_bf16, x, terms):
    acc = None
    rem = x
    for _ in range(terms):
        piece = rem.astype(BF16)
        rem = rem - piece.astype(F32)
        part = _dot(fixed_bf16, piece)
        acc = part if acc is None else acc + part
    return acc


def _split_dot_right(x, fixed_bf16, terms):
    acc = None
    rem = x
    for _ in range(terms):
        piece = rem.astype(BF16)
        rem = rem - piece.astype(F32)
        part = _dot(piece, fixed_bf16)
        acc = part if acc is None else acc + part
    return acc


def _log_sigmoid(x):
    return jnp.minimum(x, 0.0) - jnp.log(1.0 + jnp.exp(-jnp.abs(x)))


def _sigmoid(x):
    return 1.0 / (1.0 + jnp.exp(-x))


def _silu(x):
    return x * _sigmoid(x)


def _rms(x, eps):
    return x * lax.rsqrt(jnp.mean(x * x, axis=-1, keepdims=True) + eps)


def _tri_incl(n):
    r = lax.broadcasted_iota(jnp.int32, (n, n), 0)
    c = lax.broadcasted_iota(jnp.int32, (n, n), 1)
    return c <= r


def _rmsnorm_kernel(x_ref, g_ref, o_ref):
    o_ref[...] = (_rms(x_ref[...], NORM_EPS) * g_ref[...]).astype(o_ref.dtype)


def _rmsnorm(x, g, out_dtype):
    m, d = x.shape
    tm = min(m, 1024)
    return pl.pallas_call(
        _rmsnorm_kernel,
        grid=(m // tm,),
        in_specs=[pl.BlockSpec((tm, d), lambda i: (i, 0)), pl.BlockSpec((1, d), lambda i: (0, 0))],
        out_specs=pl.BlockSpec((tm, d), lambda i: (i, 0)),
        out_shape=jax.ShapeDtypeStruct((m, d), out_dtype),
        compiler_params=_params(("arbitrary",)),
        name="rmsnorm",
    )(x, g.reshape(1, d))


def _mm_kernel(a_ref, b_ref, *rest, epilogue, n_extra):
    o_ref = rest[n_extra]
    acc = _dot(a_ref[...], b_ref[...])
    if epilogue is not None:
        acc = epilogue(acc, *[e[...] for e in rest[:n_extra]])
    o_ref[...] = acc.astype(o_ref.dtype)


def _col_tiles(w, tn=1024):
    k, n = w.shape
    tn = min(tn, n)
    return w.astype(BF16).reshape(k, n // tn, tn).transpose(1, 0, 2)


def _matmul(a, b, extras=(), epilogue=None, out_dtype=F32, tm=1024, name="matmul"):
    m, k = a.shape
    n_tiles, _, tn = b.shape
    n = n_tiles * tn
    tm = min(tm, m)
    assert m % tm == 0, (m, tm)
    in_specs = [pl.BlockSpec((tm, k), lambda i, j: (i, 0)),
                pl.BlockSpec((None, k, tn), lambda i, j: (j, 0, 0))]
    for e in extras:
        if e.shape[0] == 1:
            in_specs.append(pl.BlockSpec((1, tn), lambda i, j: (0, j)))
        else:
            in_specs.append(pl.BlockSpec((tm, tn), lambda i, j: (i, j)))
    return pl.pallas_call(
        functools.partial(_mm_kernel, epilogue=epilogue, n_extra=len(extras)),
        grid=(m // tm, n // tn),
        in_specs=in_specs,
        out_specs=pl.BlockSpec((tm, tn), lambda i, j: (i, j)),
        out_shape=jax.ShapeDtypeStruct((m, n), out_dtype),
        compiler_params=_params(("arbitrary", "arbitrary")),
        name=name,
    )(a, b, *extras)


def _rwkv_pre_kernel(x_ref, st_ref, g_ref, mix_ref, *rest):
    outs, hl_ref, carry_ref = rest[:6], rest[6], rest[7]
    tt = x_ref.shape[0]

    @pl.when(pl.program_id(1) == 0)
    def _():
        carry_ref[...] = st_ref[...]

    h = _rms(x_ref[...], NORM_EPS) * g_ref[...]
    row = lax.broadcasted_iota(jnp.int32, (tt, 1), 0)
    h_prev = jnp.where(row == 0, carry_ref[...], pltpu.roll(h, 1, axis=0))
    carry_ref[...] = h[tt - 1:tt, :]
    xx = h_prev - h
    for i, o_ref in enumerate(outs):
        o_ref[...] = (h + xx * mix_ref[i:i + 1, :]).astype(o_ref.dtype)
    hl_ref[...] = h[tt - 8:, :]


def _rwkv_pre(x, shift_state, g, mix):
    b, t, d = x.shape
    tt = min(t, 512)
    row_spec = pl.BlockSpec((None, tt, d), lambda i, j: (i, j, 0))
    outs = pl.pallas_call(
        _rwkv_pre_kernel,
        grid=(b, t // tt),
        in_specs=[row_spec,
                  pl.BlockSpec((None, 1, d), lambda i, j: (i, 0, 0)),
                  pl.BlockSpec((1, d), lambda i, j: (0, 0)),
                  pl.BlockSpec((6, d), lambda i, j: (0, 0))],
        out_specs=[row_spec] * 6 + [pl.BlockSpec((None, 8, d), lambda i, j: (i, 0, 0))],
        out_shape=[jax.ShapeDtypeStruct((b, t, d), BF16)] * 6 + [jax.ShapeDtypeStruct((b, 8, d), F32)],
        scratch_shapes=[pltpu.VMEM((1, d), F32)],
        compiler_params=_params(("arbitrary", "arbitrary")),
        name="rwkv_pre",
    )(x, shift_state.reshape(b, 1, d), g.reshape(1, d), mix)
    return outs[:6], outs[6][:, 7, :]


def _wkv7_kernel(r_ref, w_ref, k_ref, v_ref, a_ref, g_ref, kk_ref, ka_ref, rk_ref, lw_ref, lb_ref,
                 s0_ref, y_ref, s_ref):
    c_len, d = r_ref.shape
    n_groups = d // MXU_DIM

    @pl.when(pl.program_id(1) == 0)
    def _():
        s_ref[...] = s0_ref[...]

    ri = lax.broadcasted_iota(jnp.int32, (MXU_DIM, MXU_DIM), 0)
    ci = lax.broadcasted_iota(jnp.int32, (MXU_DIM, MXU_DIM), 1)
    head_bits = RWKV_HEAD.bit_length() - 1
    same_head = (ri >> head_bits) == (ci >> head_bits)
    strict = same_head & (ci < ri)
    incl = same_head & (ci <= ri)
    bd = jnp.where(same_head, 1.0, 0.0)
    bd_b = bd.astype(BF16)
    eye = jnp.where(ri == ci, 1.0, 0.0)
    tri_b = jnp.where(_tri_incl(c_len), 1.0, 0.0).astype(BF16)

    def stack(x):
        return jnp.concatenate([x] * HEADS_PER_GROUP, axis=0)

    def stack_masked(x):
        return stack(x) * bd

    def head_sum(x):
        return _split_dot_right(x, bd_b, 2)

    groups = range(n_groups)
    sls = [slice(grp * MXU_DIM, (grp + 1) * MXU_DIM) for grp in groups]
    r = [r_ref[:, sl] for sl in sls]
    k = [k_ref[:, sl] for sl in sls]
    v = [v_ref[:, sl] for sl in sls]
    a = [a_ref[:, sl] for sl in sls]
    lw = [-jnp.exp(w_ref[:, sl]) for sl in sls]
    kk = [k[i] * kk_ref[:, sls[i]] for i in groups]
    kk_ss = [head_sum(x * x) for x in kk]
    cum = [_split_dot(tri_b, x, 3) for x in lw]
    kk = [kk[i] / jnp.maximum(jnp.sqrt(kk_ss[i]), 1e-12) for i in groups]
    kmod = [k[i] * (1.0 + (a[i] - 1.0) * ka_ref[:, sls[i]]) for i in groups]
    kka = [kk[i] * a[i] for i in groups]
    last = [x[c_len - 1:c_len, :] for x in cum]
    e_neg = [jnp.exp(-x) for x in cum]
    ar = [jnp.concatenate([stack_masked(-kk[i] * jnp.exp(cum[i] - lw[i])),
                           stack_masked(r[i] * jnp.exp(cum[i]))], axis=0).astype(BF16) for i in groups]
    bk = [jnp.concatenate([stack(kka[i] * e_neg[i]), stack(kmod[i] * e_neg[i])], axis=0).astype(BF16)
          for i in groups]
    z = [_dot_nt(ar[i], bk[i]) for i in groups]
    a_ab = [jnp.where(strict, x[:MXU_DIM, :MXU_DIM], 0.0) for x in z]
    a_ak = [jnp.where(strict, x[:MXU_DIM, MXU_DIM:], 0.0).astype(BF16) for x in z]
    a_r = [jnp.concatenate([jnp.where(incl, x[MXU_DIM:, :MXU_DIM], 0.0),
                            jnp.where(incl, x[MXU_DIM:, MXU_DIM:], 0.0)], axis=1).astype(BF16) for x in z]

    tinv = [eye + x for x in a_ab]
    p_b = [x.astype(BF16) for x in a_ab]
    p = [_dot(x, x) for x in p_b]
    for _ in range(4):
        p_b = [x.astype(BF16) for x in p]
        x2 = [_dot(p_b[i], jnp.concatenate([p_b[i], tinv[i].astype(BF16)], axis=1)) for i in groups]
        p = [x[:, :MXU_DIM] for x in x2]
        tinv = [tinv[i] + x2[i][:, MXU_DIM:] for i in groups]
    tinv = [tinv[i] + _dot(p[i].astype(BF16), tinv[i].astype(BF16)) for i in groups]

    v_st = [stack_masked(x) for x in v]
    s = [s_ref[i] for i in groups]
    y = [_dot_nt(ar[i], s[i].astype(BF16)) for i in groups]
    av = [_dot(a_ak[i], v_st[i].astype(BF16)) for i in groups]
    u = [_dot(tinv[i].astype(BF16), (y[i][:MXU_DIM] + av[i]).astype(BF16)) for i in groups]
    uv = [jnp.concatenate([u[i], v_st[i]], axis=0).astype(BF16) for i in groups]
    o_st = [y[i][MXU_DIM:] + _dot(a_r[i], uv[i]) for i in groups]
    e_last = [jnp.exp(last[i] - cum[i]) for i in groups]
    bk_hat = [jnp.concatenate([stack_masked(kka[i] * e_last[i]), stack_masked(kmod[i] * e_last[i])],
                              axis=0).astype(BF16) for i in groups]
    for i in groups:
        s_ref[i] = s[i] * jnp.exp(last[i]) + _dot_tn(uv[i], bk_hat[i])

    o = [functools.reduce(lambda p, q: p + q,
                          [x[hd * c_len:(hd + 1) * c_len] for hd in range(HEADS_PER_GROUP)]) for x in o_st]
    inv_n = 1.0 / RWKV_HEAD
    mu = [head_sum(x) * inv_n for x in o]
    bonus_dot = [head_sum(r[i] * kmod[i] * rk_ref[:, sls[i]]) for i in groups]
    dev = [o[i] - mu[i] for i in groups]
    var = [head_sum(x * x) * inv_n for x in dev]
    for i in groups:
        sl = sls[i]
        o_n = dev[i] * lax.rsqrt(var[i] + RWKV_GN_EPS) * lw_ref[:, sl] + lb_ref[:, sl]
        y_ref[:, sl] = ((o_n + bonus_dot[i] * v[i]) * g_ref[:, sl]).astype(y_ref.dtype)


def _wkv7(r, w, k, v, a, g, k_k, k_a, r_k, lnx_w, lnx_b, s0_bd):
    b, t, d = r.shape
    n_groups = d // MXU_DIM
    row_spec = pl.BlockSpec((None, CHUNK, d), lambda i, j: (i, j, 0))
    vec_spec = pl.BlockSpec((1, d), lambda i, j: (0, 0))
    st_spec = pl.BlockSpec((None, n_groups, MXU_DIM, MXU_DIM), lambda i, j: (i, 0, 0, 0))
    vec = lambda p: p.reshape(1, d)
    return pl.pallas_call(
        _wkv7_kernel,
        grid=(b, t // CHUNK),
        in_specs=[row_spec] * 6 + [vec_spec] * 5 + [st_spec],
        out_specs=[row_spec, st_spec],
        out_shape=[jax.ShapeDtypeStruct((b, t, d), BF16),
                   jax.ShapeDtypeStruct(s0_bd.shape, F32)],
        compiler_params=_params(("arbitrary", "arbitrary")),
        name="wkv7",
    )(r, w, k, v, a, g, vec(k_k), vec(k_a), vec(r_k), vec(lnx_w), vec(lnx_b), s0_bd)


def _wkv_state_to_blockdiag(s):
    b, h, n, _ = s.shape
    g = h // HEADS_PER_GROUP
    s = s.reshape(b, g, HEADS_PER_GROUP, n, n)
    eye = jnp.eye(HEADS_PER_GROUP, dtype=s.dtype)
    return jnp.einsum("bghvk,hj->bghvjk", s, eye).reshape(b, g, MXU_DIM, MXU_DIM)


def _wkv_state_from_blockdiag(s_bd):
    b, g = s_bd.shape[:2]
    n = RWKV_HEAD
    s = s_bd.reshape(b, g, HEADS_PER_GROUP, n, HEADS_PER_GROUP, n)
    s = jnp.stack([s[:, :, i, :, i, :] for i in range(HEADS_PER_GROUP)], axis=2)
    return s.reshape(b, g * HEADS_PER_GROUP, n, n)


def _gla_kernel(q_ref, k_ref, v_ref, gate_ref, lr_ref, w2_ref, gb_ref, hn_ref, s0_ref, y_ref, s_ref):
    tc = q_ref.shape[0]
    n_heads, dv, dk = s_ref.shape
    heads = range(n_heads)
    ksl = [slice(h * dk, (h + 1) * dk) for h in heads]
    vsl = [slice(h * dv, (h + 1) * dv) for h in heads]

    @pl.when(pl.program_id(1) == 0)
    def _():
        s_ref[...] = s0_ref[...]

    tri = _tri_incl(CHUNK)
    tri_b = jnp.where(tri, 1.0, 0.0).astype(BF16)
    scale = dk ** -0.5
    for c in range(tc // CHUNK):
        rows = slice(c * CHUNK, (c + 1) * CHUNK)
        lr = lr_ref[rows, :].astype(BF16)
        gk = [_log_sigmoid(_dot(lr, w2_ref[:, sl]) + gb_ref[:, sl]) * (1.0 / GLA_GATE_NORMALIZER)
              for sl in ksl]
        cum = [_split_dot(tri_b, x, 3) for x in gk]
        last = [x[CHUNK - 1:CHUNK, :] for x in cum]
        qe = [(q_ref[rows, ksl[h]] * scale * jnp.exp(cum[h])).astype(BF16) for h in heads]
        ke = [(k_ref[rows, ksl[h]] * jnp.exp(-cum[h])).astype(BF16) for h in heads]
        k2 = [(k_ref[rows, ksl[h]] * jnp.exp(last[h] - cum[h])).astype(BF16) for h in heads]
        vb = [v_ref[rows, sl].astype(BF16) for sl in vsl]
        scores = [jnp.where(tri, _dot_nt(qe[h], ke[h]), 0.0).astype(BF16) for h in heads]
        s_t = [s_ref[h] for h in heads]
        o = [_dot(scores[h], vb[h]) + _dot_nt(qe[h], s_t[h].astype(BF16)) for h in heads]
        for h in heads:
            s_ref[h] = s_t[h] * jnp.exp(last[h]) + _dot_tn(vb[h], k2[h])
        for h in heads:
            o_n = _rms(o[h], GLA_NORM_EPS) * hn_ref[...]
            y_ref[rows, vsl[h]] = (o_n * _silu(gate_ref[rows, vsl[h]])).astype(y_ref.dtype)


def _gla(qkvg, lr, gk_w2, gk_b, head_norm, s0_t):
    b, t, _ = qkvg.shape
    h, dv, dk = s0_t.shape[1:]
    tc = min(t, 2 * CHUNK)
    dk_all, dv_all = h * dk, h * dv
    assert dv_all == 2 * dk_all
    st_spec = pl.BlockSpec((None, h, dv, dk), lambda i, c: (i, 0, 0, 0))
    return pl.pallas_call(
        _gla_kernel,
        grid=(b, t // tc),
        in_specs=[pl.BlockSpec((None, tc, dk_all), lambda i, c: (i, c, 0)),
                  pl.BlockSpec((None, tc, dk_all), lambda i, c: (i, c, 1)),
                  pl.BlockSpec((None, tc, dv_all), lambda i, c: (i, c, 1)),
                  pl.BlockSpec((None, tc, dv_all), lambda i, c: (i, c, 2)),
                  pl.BlockSpec((None, tc, LANES), lambda i, c: (i, c, 0)),
                  pl.BlockSpec((LANES, dk_all), lambda i, c: (0, 0)),
                  pl.BlockSpec((1, dk_all), lambda i, c: (0, 0)),
                  pl.BlockSpec((1, dv), lambda i, c: (0, 0)),
                  st_spec],
        out_specs=[pl.BlockSpec((None, tc, dv_all), lambda i, c: (i, c, 0)), st_spec],
        out_shape=[jax.ShapeDtypeStruct((b, t, dv_all), BF16),
                   jax.ShapeDtypeStruct(s0_t.shape, F32)],
        compiler_params=_params(("arbitrary", "arbitrary")),
        name="gla",
    )(qkvg, qkvg, qkvg, qkvg, lr, gk_w2, gk_b.reshape(1, -1), head_norm.reshape(1, dv), s0_t)


def _ffn_kernel(x_ref, g_ref, sv_ref, sg_ref, wv_ref, wg_ref, cwv_ref, cwg_ref, cbv_ref, cbg_ref,
                wd_ref, o_ref, nsv_ref, nsg_ref, h_ref, carry_ref):
    nseq, seq_len, d = x_ref.shape
    rows = nseq * seq_len
    tn = wv_ref.shape[1]
    j = pl.program_id(2)

    @pl.when(j == 0)
    def _():
        x = x_ref[...].reshape(rows, d)
        h_ref[...] = (_rms(x, NORM_EPS) * g_ref[...]).astype(BF16)
        o_ref[...] = x_ref[...]

    @pl.when(pl.program_id(1) == 0)
    def _():
        carry_ref[2 * j] = sv_ref[...]
        carry_ref[2 * j + 1] = sg_ref[...]

    h = h_ref[...]
    tpos = lax.broadcasted_iota(jnp.int32, (1, seq_len, 1), 1)

    cols = [slice(c * MXU_DIM, (c + 1) * MXU_DIM) for c in range(tn // MXU_DIM)]

    def up(w_ref):
        return [_dot(h, w_ref[:, cs]).reshape(nseq, seq_len, MXU_DIM) for cs in cols]

    def conv(u, cs, cw_ref, cb_ref, slot, ns_ref):
        p2, p1 = carry_ref[slot, :, 0:1, cs], carry_ref[slot, :, 1:2, cs]
        u1 = jnp.where(tpos == 0, p1, pltpu.roll(u, 1, axis=1))
        u2 = jnp.where(tpos == 0, p2, jnp.where(tpos == 1, p1, pltpu.roll(u, 2, axis=1)))
        tail = u[:, seq_len - 2:, :]
        carry_ref[slot, :, :, cs] = tail
        ns_ref[:, :, cs] = tail
        return cb_ref[:, cs] + cw_ref[0:1, cs] * u2 + cw_ref[1:2, cs] * u1 + cw_ref[2:3, cs] * u

    u_val, u_gate = up(wv_ref), up(wg_ref)
    acc = None
    for c, cs in enumerate(cols):
        val = conv(u_val[c], cs, cwv_ref, cbv_ref, 2 * j, nsv_ref)
        gate = conv(u_gate[c], cs, cwg_ref, cbg_ref, 2 * j + 1, nsg_ref)
        act = (_silu(gate) * val).reshape(rows, MXU_DIM).astype(BF16)
        part = _dot(act, wd_ref[cs, :])
        acc = part if acc is None else acc + part
    o_ref[...] += acc.reshape(nseq, seq_len, d)


FFN_TILE_COLS = 512


def _ffn_up_tiles(w_up):
    d, f2 = w_up.shape
    return w_up.reshape(d, f2 // FFN_TILE_COLS, FFN_TILE_COLS).transpose(1, 0, 2)


def _conv_ffn(x, conv_state, g, w_up_tiles, conv_w, conv_b, w_down, nseq, seq_len):
    b, t, d = x.shape
    f = w_down.shape[0]
    tn = w_up_tiles.shape[2]
    nj = f // tn
    assert b % nseq == 0 and t % seq_len == 0 and f % tn == 0
    x_spec = pl.BlockSpec((nseq, seq_len, d), lambda i, s, j: (i, s, 0))
    st_v = pl.BlockSpec((nseq, 2, tn), lambda i, s, j: (i, 0, j))
    st_g = pl.BlockSpec((nseq, 2, tn), lambda i, s, j: (i, 0, nj + j))
    n_s = t // seq_len
    tail_spec = pl.BlockSpec((nseq, None, 2, tn), lambda i, s, j: (i, s, 0, j))
    col_v = lambda n: pl.BlockSpec((n, tn), lambda i, s, j: (0, j))
    col_g = lambda n: pl.BlockSpec((n, tn), lambda i, s, j: (0, nj + j))
    y, ns_v, ns_g = pl.pallas_call(
        _ffn_kernel,
        grid=(b // nseq, n_s, nj),
        in_specs=[x_spec, pl.BlockSpec((1, d), lambda i, s, j: (0, 0)), st_v, st_g,
                  pl.BlockSpec((None, d, tn), lambda i, s, j: (j, 0, 0)),
                  pl.BlockSpec((None, d, tn), lambda i, s, j: (nj + j, 0, 0)),
                  col_v(3), col_g(3), col_v(1), col_g(1),
                  pl.BlockSpec((tn, d), lambda i, s, j: (j, 0))],
        out_specs=[x_spec, tail_spec, tail_spec],
        out_shape=[jax.ShapeDtypeStruct((b, t, d), F32),
                   jax.ShapeDtypeStruct((b, n_s, 2, f), F32),
                   jax.ShapeDtypeStruct((b, n_s, 2, f), F32)],
        scratch_shapes=[pltpu.VMEM((nseq * seq_len, d), BF16),
                        pltpu.VMEM((2 * nj, nseq, 2, tn), F32)],
        compiler_params=_params(("arbitrary", "arbitrary", "arbitrary")),
        name="conv_ffn",
    )(x, g.reshape(1, d), conv_state, conv_state, w_up_tiles, w_up_tiles, conv_w, conv_w,
      conv_b.reshape(1, -1), conv_b.reshape(1, -1), w_down)
    return y, jnp.concatenate([ns_v[:, -1], ns_g[:, -1]], axis=-1)


def _pad_cols(w, n):
    return jnp.pad(w, ((0, 0), (0, n - w.shape[1])))


def _pad_rows(w, n):
    return jnp.pad(w, ((0, n - w.shape[0]), (0, 0)))


def _rwkv_layer(x, shift_st, wkv_st, v_first, p):
    b, t, d = x.shape
    m = b * t
    (xr, xw, xk, xv, xa, xg), new_shift = _rwkv_pre(x, shift_st, p["norm"], p["mix"])
    flat = lambda z: z.reshape(m, d)
    row = lambda z: z.reshape(1, d)
    r = _matmul(flat(xr), p["wr"], name="rwkv_r")
    k = _matmul(flat(xk), p["wk"], name="rwkv_k")
    v = _matmul(flat(xv), p["wv"], name="rwkv_v")
    w_lora = _matmul(flat(xw), p["w1"], epilogue=jnp.tanh, out_dtype=BF16, name="rwkv_w1")
    w = _matmul(w_lora, p["w2"], extras=(row(p["w0"]),),
                epilogue=lambda acc, w0: _log_sigmoid(w0 + acc) - 0.5, name="rwkv_w2")
    a_lora = _matmul(flat(xa), p["a1"], out_dtype=BF16, name="rwkv_a1")
    a = _matmul(a_lora, p["a2"], extras=(row(p["a0"]),),
                epilogue=lambda acc, a0: _sigmoid(a0 + acc), name="rwkv_a2")
    g_lora = _matmul(flat(xg), p["g1"], epilogue=_sigmoid, out_dtype=BF16, name="rwkv_g1")
    g = _matmul(g_lora, p["g2"], name="rwkv_g2")
    if p["v1"] is None:
        v_first = v
    else:
        v_lora = _matmul(flat(xv), p["v1"], out_dtype=BF16, name="rwkv_v1")
        v = _matmul(v_lora, p["v2"], extras=(row(p["v0"]), v, v_first),
                    epilogue=lambda acc, v0, vv, vf: vv + (vf - vv) * _sigmoid(v0 + acc),
                    name="rwkv_v2")
    seq = lambda z: z.reshape(b, t, d)
    y, s_bd = _wkv7(seq(r), seq(w), seq(k), seq(v), seq(a), seq(g), p["k_k"], p["k_a"], p["r_k"],
                    p["lnx_w"], p["lnx_b"], _wkv_state_to_blockdiag(wkv_st))
    x_new = _matmul(flat(y), p["wo"], extras=(flat(x),),
                    epilogue=lambda acc, res: res + acc, name="rwkv_o")
    return seq(x_new), new_shift, _wkv_state_from_blockdiag(s_bd), v_first


def _gla_layer(x, gla_st, p):
    b, t, d = x.shape
    m = b * t
    h = _rmsnorm(x.reshape(m, d), p["norm"], BF16)
    qkvg = _matmul(h, p["w_main"], name="gla_in")
    lr = _matmul(h, p["w_lr"], name="gla_lr")
    y, s_t = _gla(qkvg.reshape(b, t, -1), lr.reshape(b, t, LANES), p["gk_w2"], p["gk_b"],
                  p["head_norm"], jnp.swapaxes(gla_st, -1, -2))
    x_new = _matmul(y.reshape(m, -1), p["wo"], extras=(x.reshape(m, d),),
                    epilogue=lambda acc, res: res + acc, name="gla_o")
    return x_new.reshape(b, t, d), jnp.swapaxes(s_t, -1, -2)


FFN_TILE_ROWS = 512


def _ffn_tile(b, t):
    if t >= FFN_TILE_ROWS:
        return 1, FFN_TILE_ROWS
    return min(b, FFN_TILE_ROWS // t), t


def _run_trunk(x, shift_st, wkv_st, gla_st, conv_st, layers, norm_final, ffn_nseq, ffn_len):
    new_shift, new_wkv, new_gla, new_conv = [], [], [], []
    v_first = None
    for i, (mixer, ffn) in enumerate(layers):
        j = i // 2
        if i % 2 == 0:
            x, s_shift, s_wkv, v_first = _rwkv_layer(x, shift_st[j], wkv_st[j], v_first, mixer)
            new_shift.append(s_shift)
            new_wkv.append(s_wkv)
        else:
            x, s_gla = _gla_layer(x, gla_st[j], mixer)
            new_gla.append(s_gla)
        x, s_conv = _conv_ffn(x, conv_st[i], ffn["norm"], ffn["w_up"], ffn["conv_w"], ffn["conv_b"],
                              ffn["w_down"], ffn_nseq, ffn_len)
        new_conv.append(s_conv)
    b, t, d = x.shape
    y = _rmsnorm(x.reshape(b * t, d), norm_final, F32).reshape(b, t, d)
    return y, jnp.stack(new_shift), jnp.stack(new_wkv), jnp.stack(new_gla), jnp.stack(new_conv)


def kernel(x_prompt, x_sample, state_rwkv_shift, state_rwkv_wkv, state_gla, state_ffn_conv, norm_mix, norm_ffn, norm_final, rwkv_mix, rwkv_w0, rwkv_w1, rwkv_w2, rwkv_a0, rwkv_a1, rwkv_a2, rwkv_v0, rwkv_v1, rwkv_v2, rwkv_g1, rwkv_g2, rwkv_k_k, rwkv_k_a, rwkv_r_k, rwkv_wr, rwkv_wk, rwkv_wv, rwkv_wo, rwkv_lnx_w, rwkv_lnx_b, gla_w_in, gla_gk_w2, gla_gk_b, gla_head_norm, gla_wo, ffn_w_up, ffn_conv_w, ffn_conv_b, ffn_w_down):
    depth = norm_mix.shape[0]
    d = x_prompt.shape[-1]
    bf = lambda w: w.astype(BF16)
    dk_total = gla_gk_w2.shape[-1]
    n_main = gla_w_in.shape[-1] - GLA_GATE_RANK
    layers = []
    for i in range(depth):
        j = i // 2
        if i % 2 == 0:
            has_vres = j > 0
            mixer = dict(
                norm=norm_mix[i], mix=rwkv_mix[j],
                wr=_col_tiles(rwkv_wr[j]), wk=_col_tiles(rwkv_wk[j]), wv=_col_tiles(rwkv_wv[j]),
                wo=_col_tiles(rwkv_wo[j], 512),
                w0=rwkv_w0[j], w1=_col_tiles(_pad_cols(rwkv_w1[j], LANES)),
                w2=_col_tiles(_pad_rows(rwkv_w2[j], LANES)),
                a0=rwkv_a0[j], a1=_col_tiles(_pad_cols(rwkv_a1[j], LANES)),
                a2=_col_tiles(_pad_rows(rwkv_a2[j], LANES)),
                g1=_col_tiles(rwkv_g1[j]), g2=_col_tiles(rwkv_g2[j]),
                v0=rwkv_v0[j - 1] if has_vres else None,
                v1=_col_tiles(_pad_cols(rwkv_v1[j - 1], LANES)) if has_vres else None,
                v2=_col_tiles(_pad_rows(rwkv_v2[j - 1], LANES), 512) if has_vres else None,
                k_k=rwkv_k_k[j], k_a=rwkv_k_a[j], r_k=rwkv_r_k[j].reshape(d),
                lnx_w=rwkv_lnx_w[j], lnx_b=rwkv_lnx_b[j])
        else:
            mixer = dict(
                norm=norm_mix[i],
                w_main=_col_tiles(gla_w_in[j][:, :n_main]),
                w_lr=_col_tiles(_pad_cols(gla_w_in[j][:, n_main:], LANES)),
                gk_w2=bf(_pad_rows(gla_gk_w2[j], LANES)), gk_b=gla_gk_b[j],
                head_norm=gla_head_norm[j], wo=_col_tiles(gla_wo[j], 512))
        ffn = dict(norm=norm_ffn[i], w_up=_ffn_up_tiles(bf(ffn_w_up[i])), conv_w=ffn_conv_w[i],
                   conv_b=ffn_conv_b[i],
                   w_down=bf(ffn_w_down[i]))
        layers.append((mixer, ffn))

    bp, tp, _ = x_prompt.shape
    bs, ts, _ = x_sample.shape
    zeros_like_state = lambda s: jnp.zeros((s.shape[0], bp) + s.shape[2:], s.dtype)
    out_p = _run_trunk(x_prompt, zeros_like_state(state_rwkv_shift), zeros_like_state(state_rwkv_wkv),
                       zeros_like_state(state_gla), zeros_like_state(state_ffn_conv), layers,
                       norm_final, *_ffn_tile(bp, tp))
    out_s = _run_trunk(x_sample, state_rwkv_shift, state_rwkv_wkv, state_gla, state_ffn_conv, layers,
                       norm_final, *_ffn_tile(bs, ts))
    return (out_p[0], out_s[0]) + tuple(out_p[1:]) + tuple(out_s[1:])
```

```python
import functools

import jax
import jax.numpy as jnp
from jax import lax
from jax.experimental import pallas as pl
from jax.experimental.pallas import tpu as pltpu

F32 = jnp.float32
BF16 = jnp.bfloat16

CHUNK = 64
RWKV_HEAD = 64
RWKV_GN_EPS = 64e-5
GLA_HEADS = 4
GLA_GATE_RANK = 16
GLA_GATE_NORMALIZER = 16.0
GLA_NORM_EPS = 1e-5
NORM_EPS = 1e-6

LANES = 128
MXU_DIM = 256
HEADS_PER_GROUP = MXU_DIM // RWKV_HEAD
VMEM_LIMIT = 56 * 1024 * 1024


def _params(sem):
    return pltpu.CompilerParams(dimension_semantics=sem, vmem_limit_bytes=VMEM_LIMIT)


def _dot(a, b):
    return jnp.dot(a, b, preferred_element_type=F32)


def _dot_nt(a, b):
    return lax.dot_general(a, b, (((1,), (1,)), ((), ())), preferred_element_type=F32)


def _dot_tn(a, b):
    return lax.dot_general(a, b, (((0,), (0,)), ((), ())), preferred_element_type=F32)


def _split_dot(fixed_bf16, x, terms):
    acc = None
    rem = x
    for _ in range(terms):
        piece = rem.astype(BF16)
        rem = rem - piece.astype(F32)
        part = _dot(fixed_bf16, piece)
        acc = part if acc is None else acc + part
    return acc


def _split_dot_right(x, fixed_bf16, terms):
    acc = None
    rem = x
    for _ in range(terms):
        piece = rem.astype(BF16)
        rem = rem - piece.astype(F32)
        part = _dot(piece, fixed_bf16)
        acc = part if acc is None else acc + part
    return acc


def _log_sigmoid(x):
    return jnp.minimum(x, 0.0) - jnp.log(1.0 + jnp.exp(-jnp.abs(x)))


def _sigmoid(x):
    return 1.0 / (1.0 + jnp.exp(-x))


def _silu(x):
    return x * _sigmoid(x)


def _rms(x, eps):
    return x * lax.rsqrt(jnp.mean(x * x, axis=-1, keepdims=True) + eps)


def _tri_incl(n):
    r = lax.broadcasted_iota(jnp.int32, (n, n), 0)
    c = lax.broadcasted_iota(jnp.int32, (n, n), 1)
    return c <= r


def _rmsnorm_kernel(x_ref, g_ref, o_ref):
    o_ref[...] = (_rms(x_ref[...], NORM_EPS) * g_ref[...]).astype(o_ref.dtype)


def _rmsnorm(x, g, out_dtype):
    m, d = x.shape
    tm = min(m, 1024)
    return pl.pallas_call(
        _rmsnorm_kernel,
        grid=(m // tm,),
        in_specs=[pl.BlockSpec((tm, d), lambda i: (i, 0)), pl.BlockSpec((1, d), lambda i: (0, 0))],
        out_specs=pl.BlockSpec((tm, d), lambda i: (i, 0)),
        out_shape=jax.ShapeDtypeStruct((m, d), out_dtype),
        compiler_params=_params(("arbitrary",)),
        name="rmsnorm",
    )(x, g.reshape(1, d))


def _mm_kernel(a_ref, b_ref, *rest, epilogue, n_extra):
    o_ref = rest[n_extra]
    acc = _dot(a_ref[...], b_ref[...])
    if epilogue is not None:
        acc = epilogue(acc, *[e[...] for e in rest[:n_extra]])
    o_ref[...] = acc.astype(o_ref.dtype)


def _matmul(a, b, extras=(), epilogue=None, out_dtype=F32, tm=1024, tn=1024, name="matmul"):
    m, k = a.shape
    n = b.shape[1]
    tm, tn = min(tm, m), min(tn, n)
    assert m % tm == 0 and n % tn == 0, (m, n, tm, tn)
    in_specs = [pl.BlockSpec((tm, k), lambda i, j: (i, 0)), pl.BlockSpec((k, tn), lambda i, j: (0, j))]
    for e in extras:
        if e.shape[0] == 1:
            in_specs.append(pl.BlockSpec((1, tn), lambda i, j: (0, j)))
        else:
            in_specs.append(pl.BlockSpec((tm, tn), lambda i, j: (i, j)))
    return pl.pallas_call(
        functools.partial(_mm_kernel, epilogue=epilogue, n_extra=len(extras)),
        grid=(m // tm, n // tn),
        in_specs=in_specs,
        out_specs=pl.BlockSpec((tm, tn), lambda i, j: (i, j)),
        out_shape=jax.ShapeDtypeStruct((m, n), out_dtype),
        compiler_params=_params(("arbitrary", "arbitrary")),
        name=name,
    )(a, b, *extras)


def _rwkv_pre_kernel(x_ref, st_ref, g_ref, mix_ref, *rest):
    outs, hl_ref, carry_ref = rest[:6], rest[6], rest[7]
    tt = x_ref.shape[0]

    @pl.when(pl.program_id(1) == 0)
    def _():
        carry_ref[...] = st_ref[...]

    h = _rms(x_ref[...], NORM_EPS) * g_ref[...]
    row = lax.broadcasted_iota(jnp.int32, (tt, 1), 0)
    h_prev = jnp.where(row == 0, carry_ref[...], pltpu.roll(h, 1, axis=0))
    carry_ref[...] = h[tt - 1:tt, :]
    xx = h_prev - h
    for i, o_ref in enumerate(outs):
        o_ref[...] = (h + xx * mix_ref[i:i + 1, :]).astype(o_ref.dtype)
    hl_ref[...] = h[tt - 8:, :]


def _rwkv_pre(x, shift_state, g, mix):
    b, t, d = x.shape
    tt = min(t, 512)
    row_spec = pl.BlockSpec((None, tt, d), lambda i, j: (i, j, 0))
    outs = pl.pallas_call(
        _rwkv_pre_kernel,
        grid=(b, t // tt),
        in_specs=[row_spec,
                  pl.BlockSpec((None, 1, d), lambda i, j: (i, 0, 0)),
                  pl.BlockSpec((1, d), lambda i, j: (0, 0)),
                  pl.BlockSpec((6, d), lambda i, j: (0, 0))],
        out_specs=[row_spec] * 6 + [pl.BlockSpec((None, 8, d), lambda i, j: (i, 0, 0))],
        out_shape=[jax.ShapeDtypeStruct((b, t, d), BF16)] * 6 + [jax.ShapeDtypeStruct((b, 8, d), F32)],
        scratch_shapes=[pltpu.VMEM((1, d), F32)],
        compiler_params=_params(("arbitrary", "arbitrary")),
        name="rwkv_pre",
    )(x, shift_state.reshape(b, 1, d), g.reshape(1, d), mix)
    return outs[:6], outs[6][:, 7, :]


def _wkv7_kernel(r_ref, w_ref, k_ref, v_ref, a_ref, g_ref, kk_ref, ka_ref, rk_ref, lw_ref, lb_ref,
                 s0_ref, y_ref, s_ref):
    c_len, d = r_ref.shape
    n_groups = d // MXU_DIM

    @pl.when(pl.program_id(1) == 0)
    def _():
        s_ref[...] = s0_ref[...]

    ri = lax.broadcasted_iota(jnp.int32, (MXU_DIM, MXU_DIM), 0)
    ci = lax.broadcasted_iota(jnp.int32, (MXU_DIM, MXU_DIM), 1)
    head_bits = RWKV_HEAD.bit_length() - 1
    same_head = (ri >> head_bits) == (ci >> head_bits)
    bd = jnp.where(same_head, 1.0, 0.0)
    bd_b = bd.astype(BF16)
    tri_b = jnp.where(_tri_incl(c_len), 1.0, 0.0).astype(BF16)

    assert c_len == RWKV_HEAD and 2 * RWKV_HEAD == LANES
    t_c = lax.broadcasted_iota(jnp.int32, (MXU_DIM, LANES), 0) & (c_len - 1)
    lane_c = lax.broadcasted_iota(jnp.int32, (MXU_DIM, LANES), 1)
    s_c = lane_c & (RWKV_HEAD - 1)
    low_half = lane_c < RWKV_HEAD
    strict_c = s_c < t_c
    incl_c = s_c <= t_c
    eye_c = jnp.where(s_c == t_c, 1.0, 0.0)
    lane_t = lax.broadcasted_iota(jnp.int32, (c_len, LANES), 1)
    half_masks = (lane_t < RWKV_HEAD, lane_t >= RWKV_HEAD)

    def stack(x):
        return jnp.concatenate([x] * HEADS_PER_GROUP, axis=0)

    def stack_masked(x):
        return stack(x) * bd

    def head_sum(x):
        return _split_dot_right(x, bd_b, 1)

    def expand(xc, half):
        zeros = jnp.zeros((c_len, LANES), F32)
        blocks = []
        for hd in range(HEADS_PER_GROUP):
            src = xc[hd * c_len:(hd + 1) * c_len]
            if hd % 2 != half:
                src = pltpu.roll(src, RWKV_HEAD, axis=1)
            tile = jnp.where(half_masks[hd % 2], src, 0.0)
            blocks.append(jnp.concatenate(
                [tile if lt == hd // 2 else zeros for lt in range(MXU_DIM // LANES)], axis=1))
        return jnp.concatenate(blocks, axis=0)

    groups = range(n_groups)
    sls = [slice(grp * MXU_DIM, (grp + 1) * MXU_DIM) for grp in groups]
    r = [r_ref[:, sl] for sl in sls]
    k = [k_ref[:, sl] for sl in sls]
    v = [v_ref[:, sl] for sl in sls]
    a = [a_ref[:, sl] for sl in sls]
    lw = [-jnp.exp(w_ref[:, sl]) for sl in sls]
    kk = [k[i] * kk_ref[:, sls[i]] for i in groups]
    kk_ss = [head_sum(x * x) for x in kk]
    cum = [_split_dot(tri_b, x, 3) for x in lw]
    kk = [kk[i] / jnp.maximum(jnp.sqrt(kk_ss[i]), 1e-12) for i in groups]
    kmod = [k[i] * (1.0 + (a[i] - 1.0) * ka_ref[:, sls[i]]) for i in groups]
    kka = [kk[i] * a[i] for i in groups]
    last = [x[c_len - 1:c_len, :] for x in cum]
    e_neg = [jnp.exp(-x) for x in cum]
    ar = [jnp.concatenate([stack_masked(-kk[i] * jnp.exp(cum[i] - lw[i])),
                           stack_masked(r[i] * jnp.exp(cum[i]))], axis=0).astype(BF16) for i in groups]
    bk = [jnp.concatenate([kka[i] * e_neg[i], kmod[i] * e_neg[i]], axis=0).astype(BF16) for i in groups]
    z = [_dot_nt(ar[i], bk[i]) for i in groups]
    za = [jnp.where(strict_c, x[:MXU_DIM], 0.0) for x in z]
    zr = [jnp.where(incl_c, x[MXU_DIM:], 0.0) for x in z]
    a_ak = [expand(x, 1).astype(BF16) for x in za]
    a_r = [jnp.concatenate([expand(x, 0), expand(x, 1)], axis=1).astype(BF16) for x in zr]

    rhs = [jnp.where(low_half, x, eye_c) for x in za]
    p_bd = [expand(x, 0).astype(BF16) for x in za]
    n_steps = (c_len - 1).bit_length()
    for step in range(n_steps):
        x2 = [_dot(p_bd[i], rhs[i].astype(BF16)) for i in groups]
        rhs = [x2[i] + jnp.where(low_half, 0.0, rhs[i]) for i in groups]
        if step + 1 < n_steps:
            p_bd = [expand(x, 0).astype(BF16) for x in x2]
    tinv = [expand(x, 1) for x in rhs]

    v_st = [stack_masked(x) for x in v]
    s = [s_ref[i] for i in groups]
    y = [_dot_nt(ar[i], s[i].astype(BF16)) for i in groups]
    av = [_dot(a_ak[i], v_st[i].astype(BF16)) for i in groups]
    u = [_dot(tinv[i].astype(BF16), (y[i][:MXU_DIM] + av[i]).astype(BF16)) for i in groups]
    uv = [jnp.concatenate([u[i], v_st[i]], axis=0).astype(BF16) for i in groups]
    o_st = [y[i][MXU_DIM:] + _dot(a_r[i], uv[i]) for i in groups]
    e_last = [jnp.exp(last[i] - cum[i]) for i in groups]
    bk_hat = [jnp.concatenate([stack_masked(kka[i] * e_last[i]), stack_masked(kmod[i] * e_last[i])],
                              axis=0).astype(BF16) for i in groups]
    for i in groups:
        s_ref[i] = s[i] * jnp.exp(last[i]) + _dot_tn(uv[i], bk_hat[i])

    o = [functools.reduce(lambda p, q: p + q,
                          [x[hd * c_len:(hd + 1) * c_len] for hd in range(HEADS_PER_GROUP)]) for x in o_st]
    inv_n = 1.0 / RWKV_HEAD
    mu = [head_sum(x) * inv_n for x in o]
    bonus_dot = [head_sum(r[i] * kmod[i] * rk_ref[:, sls[i]]) for i in groups]
    dev = [o[i] - mu[i] for i in groups]
    var = [head_sum(x * x) * inv_n for x in dev]
    for i in groups:
        sl = sls[i]
        o_n = dev[i] * lax.rsqrt(var[i] + RWKV_GN_EPS) * lw_ref[:, sl] + lb_ref[:, sl]
        y_ref[:, sl] = ((o_n + bonus_dot[i] * v[i]) * g_ref[:, sl]).astype(y_ref.dtype)


def _wkv7(r, w, k, v, a, g, k_k, k_a, r_k, lnx_w, lnx_b, s0_bd):
    b, t, d = r.shape
    n_groups = d // MXU_DIM
    row_spec = pl.BlockSpec((None, CHUNK, d), lambda i, j: (i, j, 0))
    vec_spec = pl.BlockSpec((1, d), lambda i, j: (0, 0))
    st_spec = pl.BlockSpec((None, n_groups, MXU_DIM, MXU_DIM), lambda i, j: (i, 0, 0, 0))
    vec = lambda p: p.reshape(1, d)
    return pl.pallas_call(
        _wkv7_kernel,
        grid=(b, t // CHUNK),
        in_specs=[row_spec] * 6 + [vec_spec] * 5 + [st_spec],
        out_specs=[row_spec, st_spec],
        out_shape=[jax.ShapeDtypeStruct((b, t, d), BF16),
                   jax.ShapeDtypeStruct(s0_bd.shape, F32)],
        compiler_params=_params(("arbitrary", "arbitrary")),
        name="wkv7",
    )(r, w, k, v, a, g, vec(k_k), vec(k_a), vec(r_k), vec(lnx_w), vec(lnx_b), s0_bd)


def _wkv_state_to_blockdiag(s):
    b, h, n, _ = s.shape
    g = h // HEADS_PER_GROUP
    s = s.reshape(b, g, HEADS_PER_GROUP, n, n)
    eye = jnp.eye(HEADS_PER_GROUP, dtype=s.dtype)
    return jnp.einsum("bghvk,hj->bghvjk", s, eye).reshape(b, g, MXU_DIM, MXU_DIM)


def _wkv_state_from_blockdiag(s_bd):
    b, g = s_bd.shape[:2]
    n = RWKV_HEAD
    s = s_bd.reshape(b, g, HEADS_PER_GROUP, n, HEADS_PER_GROUP, n)
    s = jnp.stack([s[:, :, i, :, i, :] for i in range(HEADS_PER_GROUP)], axis=2)
    return s.reshape(b, g * HEADS_PER_GROUP, n, n)


def _gla_kernel(q_ref, k_ref, v_ref, gate_ref, lr_ref, w2_ref, gb_ref, hn_ref, s0_ref, y_ref, s_ref):
    tc = q_ref.shape[0]
    n_heads, dv, dk = s_ref.shape
    heads = range(n_heads)
    ksl = [slice(h * dk, (h + 1) * dk) for h in heads]
    vsl = [slice(h * dv, (h + 1) * dv) for h in heads]

    @pl.when(pl.program_id(1) == 0)
    def _():
        s_ref[...] = s0_ref[...]

    tri = _tri_incl(CHUNK)
    tri_b = jnp.where(tri, 1.0, 0.0).astype(BF16)
    scale = dk ** -0.5
    for c in range(tc // CHUNK):
        rows = slice(c * CHUNK, (c + 1) * CHUNK)
        lr = lr_ref[rows, :].astype(BF16)
        gk = [_log_sigmoid(_dot(lr, w2_ref[:, sl]) + gb_ref[:, sl]) * (1.0 / GLA_GATE_NORMALIZER)
              for sl in ksl]
        cum = [_split_dot(tri_b, x, 3) for x in gk]
        last = [x[CHUNK - 1:CHUNK, :] for x in cum]
        qe = [(q_ref[rows, ksl[h]] * scale * jnp.exp(cum[h])).astype(BF16) for h in heads]
        ke = [(k_ref[rows, ksl[h]] * jnp.exp(-cum[h])).astype(BF16) for h in heads]
        k2 = [(k_ref[rows, ksl[h]] * jnp.exp(last[h] - cum[h])).astype(BF16) for h in heads]
        vb = [v_ref[rows, sl].astype(BF16) for sl in vsl]
        scores = [jnp.where(tri, _dot_nt(qe[h], ke[h]), 0.0).astype(BF16) for h in heads]
        s_t = [s_ref[h] for h in heads]
        o = [_dot(scores[h], vb[h]) + _dot_nt(qe[h], s_t[h].astype(BF16)) for h in heads]
        for h in heads:
            s_ref[h] = s_t[h] * jnp.exp(last[h]) + _dot_tn(vb[h], k2[h])
        for h in heads:
            o_n = _rms(o[h], GLA_NORM_EPS) * hn_ref[...]
            y_ref[rows, vsl[h]] = (o_n * _silu(gate_ref[rows, vsl[h]])).astype(y_ref.dtype)


def _gla(qkvg, lr, gk_w2, gk_b, head_norm, s0_t):
    b, t, _ = qkvg.shape
    h, dv, dk = s0_t.shape[1:]
    tc = min(t, 2 * CHUNK)
    dk_all, dv_all = h * dk, h * dv
    assert dv_all == 2 * dk_all
    st_spec = pl.BlockSpec((None, h, dv, dk), lambda i, c: (i, 0, 0, 0))
    return pl.pallas_call(
        _gla_kernel,
        grid=(b, t // tc),
        in_specs=[pl.BlockSpec((None, tc, dk_all), lambda i, c: (i, c, 0)),
                  pl.BlockSpec((None, tc, dk_all), lambda i, c: (i, c, 1)),
                  pl.BlockSpec((None, tc, dv_all), lambda i, c: (i, c, 1)),
                  pl.BlockSpec((None, tc, dv_all), lambda i, c: (i, c, 2)),
                  pl.BlockSpec((None, tc, LANES), lambda i, c: (i, c, 0)),
                  pl.BlockSpec((LANES, dk_all), lambda i, c: (0, 0)),
                  pl.BlockSpec((1, dk_all), lambda i, c: (0, 0)),
                  pl.BlockSpec((1, dv), lambda i, c: (0, 0)),
                  st_spec],
        out_specs=[pl.BlockSpec((None, tc, dv_all), lambda i, c: (i, c, 0)), st_spec],
        out_shape=[jax.ShapeDtypeStruct((b, t, dv_all), BF16),
                   jax.ShapeDtypeStruct(s0_t.shape, F32)],
        compiler_params=_params(("arbitrary", "arbitrary")),
        name="gla",
    )(qkvg, qkvg, qkvg, qkvg, lr, gk_w2, gk_b.reshape(1, -1), head_norm.reshape(1, dv), s0_t)


def _ffn_kernel(x_ref, g_ref, sv_ref, sg_ref, wv_ref, wg_ref, cwv_ref, cwg_ref, cbv_ref, cbg_ref,
                wd_ref, o_ref, nsv_ref, nsg_ref, h_ref, ua_ref, ub_ref, carry_ref, *, nj):
    nseq, seq_len, d = x_ref.shape
    rows = nseq * seq_len
    tn = wv_ref.shape[1]
    j = pl.program_id(2)
    jt = jnp.maximum(j - 1, 0)

    @pl.when(j == 0)
    def _():
        x = x_ref[...].reshape(rows, d)
        h_ref[...] = (_rms(x, NORM_EPS) * g_ref[...]).astype(BF16)
        o_ref[...] = x_ref[...]

    @pl.when((pl.program_id(1) == 0) & (j > 0))
    def _():
        carry_ref[2 * jt] = sv_ref[...]
        carry_ref[2 * jt + 1] = sg_ref[...]

    cols = [slice(c * MXU_DIM, (c + 1) * MXU_DIM) for c in range(tn // MXU_DIM)]
    first = lax.broadcasted_iota(jnp.int32, (1, 8, 1), 1)

    def conv(u, cs, cw_ref, cb_ref, slot, ns_ref):
        u = u.reshape(nseq, seq_len, MXU_DIM)
        w0, w1, w2, cb = cw_ref[0:1, cs], cw_ref[1:2, cs], cw_ref[2:3, cs], cb_ref[:, cs]
        r1, r2 = pltpu.roll(u, 1, axis=1), pltpu.roll(u, 2, axis=1)
        body = cb + w0 * r2 + w1 * r1 + w2 * u
        p2, p1 = carry_ref[slot, :, 0:1, cs], carry_ref[slot, :, 1:2, cs]
        u1 = jnp.where(first == 0, p1, r1[:, :8, :])
        u2 = jnp.where(first == 0, p2, jnp.where(first == 1, p1, r2[:, :8, :]))
        head = cb + w0 * u2 + w1 * u1 + w2 * u[:, :8, :]
        tail = u[:, seq_len - 2:, :]
        carry_ref[slot, :, :, cs] = tail
        ns_ref[:, :, cs] = tail
        return jnp.concatenate([head, body[:, 8:, :]], axis=1)

    def step(up_ref, dn_ref):
        h = None if up_ref is None else h_ref[...]
        acc = None
        for cs in cols:
            if up_ref is not None:
                up_ref[0, :, cs] = _dot(h, wv_ref[:, cs])
            if dn_ref is not None:
                val = conv(dn_ref[0, :, cs], cs, cwv_ref, cbv_ref, 2 * jt, nsv_ref)
            if up_ref is not None:
                up_ref[1, :, cs] = _dot(h, wg_ref[:, cs])
            if dn_ref is not None:
                gate = conv(dn_ref[1, :, cs], cs, cwg_ref, cbg_ref, 2 * jt + 1, nsg_ref)
                act = (_silu(gate) * val).reshape(rows, MXU_DIM).astype(BF16)
                part = _dot(act, wd_ref[cs, :])
                acc = part if acc is None else acc + part
        if dn_ref is not None:
            o_ref[...] += acc.reshape(nseq, seq_len, d)

    even = (j & 1) == 0
    inner = (j > 0) & (j < nj)

    @pl.when(j == 0)
    def _():
        step(ua_ref, None)

    @pl.when(inner & even)
    def _():
        step(ua_ref, ub_ref)

    @pl.when(inner & jnp.logical_not(even))
    def _():
        step(ub_ref, ua_ref)

    @pl.when(j == nj)
    def _():
        step(None, ua_ref if (nj - 1) % 2 == 0 else ub_ref)


FFN_TILE_COLS = 512


def _conv_ffn(x, conv_state, g, w_up, conv_w, conv_b, w_down, nseq, seq_len):
    b, t, d = x.shape
    f = w_down.shape[0]
    tn = FFN_TILE_COLS
    nj = f // tn
    assert b % nseq == 0 and t % seq_len == 0 and f % tn == 0
    x_spec = pl.BlockSpec((nseq, seq_len, d), lambda i, s, j: (i, s, 0))
    up_tile = lambda j: jnp.minimum(j, nj - 1)
    dn_tile = lambda j: jnp.maximum(j - 1, 0)
    st_v = pl.BlockSpec((nseq, 2, tn), lambda i, s, j: (i, 0, dn_tile(j)))
    st_g = pl.BlockSpec((nseq, 2, tn), lambda i, s, j: (i, 0, nj + dn_tile(j)))
    n_s = t // seq_len
    tail_spec = pl.BlockSpec((nseq, None, 2, tn), lambda i, s, j: (i, s, 0, dn_tile(j)))
    col_v = lambda n: pl.BlockSpec((n, tn), lambda i, s, j: (0, dn_tile(j)))
    col_g = lambda n: pl.BlockSpec((n, tn), lambda i, s, j: (0, nj + dn_tile(j)))
    y, ns_v, ns_g = pl.pallas_call(
        functools.partial(_ffn_kernel, nj=nj),
        grid=(b // nseq, n_s, nj + 1),
        in_specs=[x_spec, pl.BlockSpec((1, d), lambda i, s, j: (0, 0)), st_v, st_g,
                  pl.BlockSpec((d, tn), lambda i, s, j: (0, up_tile(j))),
                  pl.BlockSpec((d, tn), lambda i, s, j: (0, nj + up_tile(j))),
                  col_v(3), col_g(3), col_v(1), col_g(1),
                  pl.BlockSpec((tn, d), lambda i, s, j: (dn_tile(j), 0))],
        out_specs=[x_spec, tail_spec, tail_spec],
        out_shape=[jax.ShapeDtypeStruct((b, t, d), F32),
                   jax.ShapeDtypeStruct((b, n_s, 2, f), F32),
                   jax.ShapeDtypeStruct((b, n_s, 2, f), F32)],
        scratch_shapes=[pltpu.VMEM((nseq * seq_len, d), BF16),
                        pltpu.VMEM((2, nseq * seq_len, tn), F32),
                        pltpu.VMEM((2, nseq * seq_len, tn), F32),
                        pltpu.VMEM((2 * nj, nseq, 2, tn), F32)],
        compiler_params=_params(("arbitrary", "arbitrary", "arbitrary")),
        name="conv_ffn",
    )(x, g.reshape(1, d), conv_state, conv_state, w_up, w_up, conv_w, conv_w,
      conv_b.reshape(1, -1), conv_b.reshape(1, -1), w_down)
    return y, jnp.concatenate([ns_v[:, -1], ns_g[:, -1]], axis=-1)


def _pad_cols(w, n):
    return jnp.pad(w, ((0, 0), (0, n - w.shape[1])))


def _pad_rows(w, n):
    return jnp.pad(w, ((0, n - w.shape[0]), (0, 0)))


def _rwkv_layer(x, shift_st, wkv_st, v_first, p):
    b, t, d = x.shape
    m = b * t
    (xr, xw, xk, xv, xa, xg), new_shift = _rwkv_pre(x, shift_st, p["norm"], p["mix"])
    flat = lambda z: z.reshape(m, d)
    row = lambda z: z.reshape(1, d)
    r = _matmul(flat(xr), p["wr"], name="rwkv_r")
    k = _matmul(flat(xk), p["wk"], name="rwkv_k")
    v = _matmul(flat(xv), p["wv"], name="rwkv_v")
    w_lora = _matmul(flat(xw), p["w1"], epilogue=jnp.tanh, out_dtype=BF16, name="rwkv_w1")
    w = _matmul(w_lora, p["w2"], extras=(row(p["w0"]),),
                epilogue=lambda acc, w0: _log_sigmoid(w0 + acc) - 0.5, name="rwkv_w2")
    a_lora = _matmul(flat(xa), p["a1"], out_dtype=BF16, name="rwkv_a1")
    a = _matmul(a_lora, p["a2"], extras=(row(p["a0"]),),
                epilogue=lambda acc, a0: _sigmoid(a0 + acc), name="rwkv_a2")
    g_lora = _matmul(flat(xg), p["g1"], epilogue=_sigmoid, out_dtype=BF16, name="rwkv_g1")
    g = _matmul(g_lora, p["g2"], name="rwkv_g2")
    if p["v1"] is None:
        v_first = v
    else:
        v_lora = _matmul(flat(xv), p["v1"], out_dtype=BF16, name="rwkv_v1")
        v = _matmul(v_lora, p["v2"], extras=(row(p["v0"]), v, v_first), tn=512,
                    epilogue=lambda acc, v0, vv, vf: vv + (vf - vv) * _sigmoid(v0 + acc),
                    name="rwkv_v2")
    seq = lambda z: z.reshape(b, t, d)
    y, s_bd = _wkv7(seq(r), seq(w), seq(k), seq(v), seq(a), seq(g), p["k_k"], p["k_a"], p["r_k"],
                    p["lnx_w"], p["lnx_b"], _wkv_state_to_blockdiag(wkv_st))
    x_new = _matmul(flat(y), p["wo"], extras=(flat(x),), tn=512,
                    epilogue=lambda acc, res: res + acc, name="rwkv_o")
    return seq(x_new), new_shift, _wkv_state_from_blockdiag(s_bd), v_first


def _gla_layer(x, gla_st, p):
    b, t, d = x.shape
    m = b * t
    h = _rmsnorm(x.reshape(m, d), p["norm"], BF16)
    qkvg = _matmul(h, p["w_main"], name="gla_in")
    lr = _matmul(h, p["w_lr"], name="gla_lr")
    y, s_t = _gla(qkvg.reshape(b, t, -1), lr.reshape(b, t, LANES), p["gk_w2"], p["gk_b"],
                  p["head_norm"], jnp.swapaxes(gla_st, -1, -2))
    x_new = _matmul(y.reshape(m, -1), p["wo"], extras=(x.reshape(m, d),), tn=512,
                    epilogue=lambda acc, res: res + acc, name="gla_o")
    return x_new.reshape(b, t, d), jnp.swapaxes(s_t, -1, -2)


FFN_TILE_ROWS = 512


def _ffn_tile(b, t):
    if t >= FFN_TILE_ROWS:
        return 1, FFN_TILE_ROWS
    return min(b, FFN_TILE_ROWS // t), t


def _run_trunk(x, shift_st, wkv_st, gla_st, conv_st, layers, norm_final, ffn_nseq, ffn_len):
    new_shift, new_wkv, new_gla, new_conv = [], [], [], []
    v_first = None
    for i, (mixer, ffn) in enumerate(layers):
        j = i // 2
        if i % 2 == 0:
            x, s_shift, s_wkv, v_first = _rwkv_layer(x, shift_st[j], wkv_st[j], v_first, mixer)
            new_shift.append(s_shift)
            new_wkv.append(s_wkv)
        else:
            x, s_gla = _gla_layer(x, gla_st[j], mixer)
            new_gla.append(s_gla)
        x, s_conv = _conv_ffn(x, conv_st[i], ffn["norm"], ffn["w_up"], ffn["conv_w"], ffn["conv_b"],
                              ffn["w_down"], ffn_nseq, ffn_len)
        new_conv.append(s_conv)
    b, t, d = x.shape
    y = _rmsnorm(x.reshape(b * t, d), norm_final, F32).reshape(b, t, d)
    return y, jnp.stack(new_shift), jnp.stack(new_wkv), jnp.stack(new_gla), jnp.stack(new_conv)


def kernel(x_prompt, x_sample, state_rwkv_shift, state_rwkv_wkv, state_gla, state_ffn_conv, norm_mix, norm_ffn, norm_final, rwkv_mix, rwkv_w0, rwkv_w1, rwkv_w2, rwkv_a0, rwkv_a1, rwkv_a2, rwkv_v0, rwkv_v1, rwkv_v2, rwkv_g1, rwkv_g2, rwkv_k_k, rwkv_k_a, rwkv_r_k, rwkv_wr, rwkv_wk, rwkv_wv, rwkv_wo, rwkv_lnx_w, rwkv_lnx_b, gla_w_in, gla_gk_w2, gla_gk_b, gla_head_norm, gla_wo, ffn_w_up, ffn_conv_w, ffn_conv_b, ffn_w_down):
    depth = norm_mix.shape[0]
    d = x_prompt.shape[-1]
    bf = lambda w: w.astype(BF16)
    dk_total = gla_gk_w2.shape[-1]
    n_main = gla_w_in.shape[-1] - GLA_GATE_RANK
    layers = []
    for i in range(depth):
        j = i // 2
        if i % 2 == 0:
            has_vres = j > 0
            mixer = dict(
                norm=norm_mix[i], mix=rwkv_mix[j],
                wr=bf(rwkv_wr[j]), wk=bf(rwkv_wk[j]), wv=bf(rwkv_wv[j]), wo=bf(rwkv_wo[j]),
                w0=rwkv_w0[j], w1=bf(_pad_cols(rwkv_w1[j], LANES)), w2=bf(_pad_rows(rwkv_w2[j], LANES)),
                a0=rwkv_a0[j], a1=bf(_pad_cols(rwkv_a1[j], LANES)), a2=bf(_pad_rows(rwkv_a2[j], LANES)),
                g1=bf(rwkv_g1[j]), g2=bf(rwkv_g2[j]),
                v0=rwkv_v0[j - 1] if has_vres else None,
                v1=bf(_pad_cols(rwkv_v1[j - 1], LANES)) if has_vres else None,
                v2=bf(_pad_rows(rwkv_v2[j - 1], LANES)) if has_vres else None,
                k_k=rwkv_k_k[j], k_a=rwkv_k_a[j], r_k=rwkv_r_k[j].reshape(d),
                lnx_w=rwkv_lnx_w[j], lnx_b=rwkv_lnx_b[j])
        else:
            mixer = dict(
                norm=norm_mix[i],
                w_main=bf(gla_w_in[j][:, :n_main]),
                w_lr=bf(_pad_cols(gla_w_in[j][:, n_main:], LANES)),
                gk_w2=bf(_pad_rows(gla_gk_w2[j], LANES)), gk_b=gla_gk_b[j],
                head_norm=gla_head_norm[j], wo=bf(gla_wo[j]))
        ffn = dict(norm=norm_ffn[i], w_up=bf(ffn_w_up[i]), conv_w=ffn_conv_w[i],
                   conv_b=ffn_conv_b[i],
                   w_down=bf(ffn_w_down[i]))
        layers.append((mixer, ffn))

    bp, tp, _ = x_prompt.shape
    bs, ts, _ = x_sample.shape
    zeros_like_state = lambda s: jnp.zeros((s.shape[0], bp) + s.shape[2:], s.dtype)
    out_p = _run_trunk(x_prompt, zeros_like_state(state_rwkv_shift), zeros_like_state(state_rwkv_wkv),
                       zeros_like_state(state_gla), zeros_like_state(state_ffn_conv), layers,
                       norm_final, *_ffn_tile(bp, tp))
    out_s = _run_trunk(x_sample, state_rwkv_shift, state_rwkv_wkv, state_gla, state_ffn_conv, layers,
                       norm_final, *_ffn_tile(bs, ts))
    return (out_p[0], out_s[0]) + tuple(out_p[1:]) + tuple(out_s[1:])
```

```python
import functools

import jax
import jax.numpy as jnp
from jax import lax
from jax.experimental import pallas as pl
from jax.experimental.pallas import tpu as pltpu

F32 = jnp.float32
BF16 = jnp.bfloat16

CHUNK = 64
RWKV_HEAD = 64
RWKV_GN_EPS = 64e-5
GLA_HEADS = 4
GLA_GATE_RANK = 16
GLA_GATE_NORMALIZER = 16.0
GLA_NORM_EPS = 1e-5
NORM_EPS = 1e-6

LANES = 128
MXU_DIM = 256
HEADS_PER_GROUP = MXU_DIM // RWKV_HEAD
VMEM_LIMIT = 56 * 1024 * 1024


def _params(sem):
    return pltpu.CompilerParams(dimension_semantics=sem, vmem_limit_bytes=VMEM_LIMIT)


def _dot(a, b):
    return jnp.dot(a, b, preferred_element_type=F32)


def _dot_nt(a, b):
    return lax.dot_general(a, b, (((1,), (1,)), ((), ())), preferred_element_type=F32)


def _dot_tn(a, b):
    return lax.dot_general(a, b, (((0,), (0,)), ((), ())), preferred_element_type=F32)


def _split_dot(fixed_bf16, x, terms):
    acc = None
    rem = x
    for _ in range(terms):
        piece = rem.astype(BF16)
        rem = rem - piece.astype(F32)
        part = _dot(fixed_bf16, piece)
        acc = part if acc is None else acc + part
    return acc


def _split_dot_right(x, fixed_bf16, terms):
    acc = None
    rem = x
    for _ in range(terms):
        piece = rem.astype(BF16)
        rem = rem - piece.astype(F32)
        part = _dot(piece, fixed_bf16)
        acc = part if acc is None else acc + part
    return acc


def _log_sigmoid(x):
    return jnp.minimum(x, 0.0) - jnp.log(1.0 + jnp.exp(-jnp.abs(x)))


def _sigmoid(x):
    return 1.0 / (1.0 + jnp.exp(-x))


def _silu(x):
    return x * _sigmoid(x)


def _rms(x, eps):
    return x * lax.rsqrt(jnp.mean(x * x, axis=-1, keepdims=True) + eps)


def _tri_incl(n):
    r = lax.broadcasted_iota(jnp.int32, (n, n), 0)
    c = lax.broadcasted_iota(jnp.int32, (n, n), 1)
    return c <= r


def _rmsnorm_kernel(x_ref, g_ref, o_ref):
    o_ref[...] = (_rms(x_ref[...], NORM_EPS) * g_ref[...]).astype(o_ref.dtype)


def _rmsnorm(x, g, out_dtype):
    m, d = x.shape
    tm = min(m, 1024)
    return pl.pallas_call(
        _rmsnorm_kernel,
        grid=(m // tm,),
        in_specs=[pl.BlockSpec((tm, d), lambda i: (i, 0)), pl.BlockSpec((1, d), lambda i: (0, 0))],
        out_specs=pl.BlockSpec((tm, d), lambda i: (i, 0)),
        out_shape=jax.ShapeDtypeStruct((m, d), out_dtype),
        compiler_params=_params(("arbitrary",)),
        name="rmsnorm",
    )(x, g.reshape(1, d))


def _mm_kernel(a_ref, b_ref, *rest, epilogue, n_extra):
    o_ref = rest[n_extra]
    acc = _dot(a_ref[...], b_ref[...])
    if epilogue is not None:
        acc = epilogue(acc, *[e[...] for e in rest[:n_extra]])
    o_ref[...] = acc.astype(o_ref.dtype)


def _matmul(a, b, extras=(), epilogue=None, out_dtype=F32, tm=1024, tn=1024, name="matmul"):
    m, k = a.shape
    n = b.shape[1]
    tm, tn = min(tm, m), min(tn, n)
    assert m % tm == 0 and n % tn == 0, (m, n, tm, tn)
    in_specs = [pl.BlockSpec((tm, k), lambda i, j: (i, 0)), pl.BlockSpec((k, tn), lambda i, j: (0, j))]
    for e in extras:
        if e.shape[0] == 1:
            in_specs.append(pl.BlockSpec((1, tn), lambda i, j: (0, j)))
        else:
            in_specs.append(pl.BlockSpec((tm, tn), lambda i, j: (i, j)))
    return pl.pallas_call(
        functools.partial(_mm_kernel, epilogue=epilogue, n_extra=len(extras)),
        grid=(m // tm, n // tn),
        in_specs=in_specs,
        out_specs=pl.BlockSpec((tm, tn), lambda i, j: (i, j)),
        out_shape=jax.ShapeDtypeStruct((m, n), out_dtype),
        compiler_params=_params(("arbitrary", "arbitrary")),
        name=name,
    )(a, b, *extras)


def _rwkv_pre_kernel(x_ref, st_ref, g_ref, mix_ref, w1_ref, a1_ref, g1_ref, *rest, has_v1):
    if has_v1:
        v1_ref, rest = rest[0], rest[1:]
    xr_ref, xk_ref, xv_ref, lw_ref, la_ref, lg_ref = rest[:6]
    rest = rest[6:]
    if has_v1:
        lv_ref, rest = rest[0], rest[1:]
    hl_ref, carry_ref = rest
    tt = x_ref.shape[0]

    @pl.when(pl.program_id(1) == 0)
    def _():
        carry_ref[...] = st_ref[...]

    h = _rms(x_ref[...], NORM_EPS) * g_ref[...]
    row = lax.broadcasted_iota(jnp.int32, (tt, 1), 0)
    h_prev = jnp.where(row == 0, carry_ref[...], pltpu.roll(h, 1, axis=0))
    carry_ref[...] = h[tt - 1:tt, :]
    hl_ref[...] = h[tt - 8:, :]
    xx = h_prev - h
    mixed = lambda i: (h + xx * mix_ref[i:i + 1, :]).astype(BF16)
    xr_ref[...] = mixed(0)
    xk_ref[...] = mixed(2)
    xv = mixed(3)
    xv_ref[...] = xv
    lw_ref[...] = jnp.tanh(_dot(mixed(1), w1_ref[...])).astype(BF16)
    la_ref[...] = _dot(mixed(4), a1_ref[...]).astype(BF16)
    lg_ref[...] = _sigmoid(_dot(mixed(5), g1_ref[...])).astype(BF16)
    if has_v1:
        lv_ref[...] = _dot(xv, v1_ref[...]).astype(BF16)


def _rwkv_pre(x, shift_state, g, mix, w1, a1, g1, v1):
    b, t, d = x.shape
    tt = min(t, 512)
    nt = t // tt
    has_v1 = v1 is not None
    lora = [w1, a1, g1] + ([v1] if has_v1 else [])
    flat_spec = lambda n: pl.BlockSpec((tt, n), lambda i, j: (i * nt + j, 0))
    full_spec = lambda w: pl.BlockSpec(w.shape, lambda i, j: (0, 0))
    wide = [jax.ShapeDtypeStruct((b * t, d), BF16)] * 3
    narrow = [jax.ShapeDtypeStruct((b * t, w.shape[1]), BF16) for w in lora]
    outs = pl.pallas_call(
        functools.partial(_rwkv_pre_kernel, has_v1=has_v1),
        grid=(b, nt),
        in_specs=[pl.BlockSpec((None, tt, d), lambda i, j: (i, j, 0)),
                  pl.BlockSpec((None, 1, d), lambda i, j: (i, 0, 0)),
                  pl.BlockSpec((1, d), lambda i, j: (0, 0)),
                  pl.BlockSpec((6, d), lambda i, j: (0, 0))] + [full_spec(w) for w in lora],
        out_specs=[flat_spec(d)] * 3 + [flat_spec(w.shape[1]) for w in lora]
                  + [pl.BlockSpec((None, 8, d), lambda i, j: (i, 0, 0))],
        out_shape=wide + narrow + [jax.ShapeDtypeStruct((b, 8, d), F32)],
        scratch_shapes=[pltpu.VMEM((1, d), F32)],
        compiler_params=_params(("arbitrary", "arbitrary")),
        name="rwkv_pre",
    )(x, shift_state.reshape(b, 1, d), g.reshape(1, d), mix, *lora)
    return outs[:-1], outs[-1][:, 7, :]


def _wkv7_kernel(r_ref, w_ref, k_ref, v_ref, a_ref, g_ref, kk_ref, ka_ref, rk_ref, lw_ref, lb_ref,
                 s0_ref, y_ref, s_ref):
    c_len, d = r_ref.shape
    n_groups = d // MXU_DIM

    @pl.when(pl.program_id(1) == 0)
    def _():
        s_ref[...] = s0_ref[...]

    ri = lax.broadcasted_iota(jnp.int32, (MXU_DIM, MXU_DIM), 0)
    ci = lax.broadcasted_iota(jnp.int32, (MXU_DIM, MXU_DIM), 1)
    head_bits = RWKV_HEAD.bit_length() - 1
    same_head = (ri >> head_bits) == (ci >> head_bits)
    bd = jnp.where(same_head, 1.0, 0.0)
    bd_b = bd.astype(BF16)
    tri_b = jnp.where(_tri_incl(c_len), 1.0, 0.0).astype(BF16)

    assert c_len == RWKV_HEAD and 2 * RWKV_HEAD == LANES
    t_c = lax.broadcasted_iota(jnp.int32, (MXU_DIM, LANES), 0) & (c_len - 1)
    lane_c = lax.broadcasted_iota(jnp.int32, (MXU_DIM, LANES), 1)
    s_c = lane_c & (RWKV_HEAD - 1)
    low_half = lane_c < RWKV_HEAD
    strict_c = s_c < t_c
    incl_c = s_c <= t_c
    eye_c = jnp.where(s_c == t_c, 1.0, 0.0)
    lane_t = lax.broadcasted_iota(jnp.int32, (c_len, LANES), 1)
    half_masks = (lane_t < RWKV_HEAD, lane_t >= RWKV_HEAD)

    def stack(x):
        return jnp.concatenate([x] * HEADS_PER_GROUP, axis=0)

    def stack_masked(x):
        return stack(x) * bd

    def head_sum(x):
        return _split_dot_right(x, bd_b, 1)

    def expand(xc, half):
        zeros = jnp.zeros((c_len, LANES), F32)
        blocks = []
        for hd in range(HEADS_PER_GROUP):
            src = xc[hd * c_len:(hd + 1) * c_len]
            if hd % 2 != half:
                src = pltpu.roll(src, RWKV_HEAD, axis=1)
            tile = jnp.where(half_masks[hd % 2], src, 0.0)
            blocks.append(jnp.concatenate(
                [tile if lt == hd // 2 else zeros for lt in range(MXU_DIM // LANES)], axis=1))
        return jnp.concatenate(blocks, axis=0)

    groups = range(n_groups)
    sls = [slice(grp * MXU_DIM, (grp + 1) * MXU_DIM) for grp in groups]
    r = [r_ref[:, sl] for sl in sls]
    k = [k_ref[:, sl] for sl in sls]
    v = [v_ref[:, sl] for sl in sls]
    a = [a_ref[:, sl] for sl in sls]
    lw = [-jnp.exp(w_ref[:, sl]) for sl in sls]
    kk = [k[i] * kk_ref[:, sls[i]] for i in groups]
    kk_ss = [head_sum(x * x) for x in kk]
    cum = [_split_dot(tri_b, x, 3) for x in lw]
    kk = [kk[i] / jnp.maximum(jnp.sqrt(kk_ss[i]), 1e-12) for i in groups]
    kmod = [k[i] * (1.0 + (a[i] - 1.0) * ka_ref[:, sls[i]]) for i in groups]
    kka = [kk[i] * a[i] for i in groups]
    last = [x[c_len - 1:c_len, :] for x in cum]
    e_neg = [jnp.exp(-x) for x in cum]
    ar = [jnp.concatenate([stack_masked(-kk[i] * jnp.exp(cum[i] - lw[i])),
                           stack_masked(r[i] * jnp.exp(cum[i]))], axis=0).astype(BF16) for i in groups]
    bk = [jnp.concatenate([kka[i] * e_neg[i], kmod[i] * e_neg[i]], axis=0).astype(BF16) for i in groups]
    z = [_dot_nt(ar[i], bk[i]) for i in groups]
    za = [jnp.where(strict_c, x[:MXU_DIM], 0.0) for x in z]
    zr = [jnp.where(incl_c, x[MXU_DIM:], 0.0) for x in z]
    a_ak = [expand(x, 1).astype(BF16) for x in za]
    a_r = [jnp.concatenate([expand(x, 0), expand(x, 1)], axis=1).astype(BF16) for x in zr]

    rhs = [jnp.where(low_half, x, eye_c) for x in za]
    p_bd = [expand(x, 0).astype(BF16) for x in za]
    n_steps = (c_len - 1).bit_length()
    for step in range(n_steps):
        x2 = [_dot(p_bd[i], rhs[i].astype(BF16)) for i in groups]
        rhs = [x2[i] + jnp.where(low_half, 0.0, rhs[i]) for i in groups]
        if step + 1 < n_steps:
            p_bd = [expand(x, 0).astype(BF16) for x in x2]
    tinv = [expand(x, 1) for x in rhs]

    v_st = [stack_masked(x) for x in v]
    s = [s_ref[i] for i in groups]
    y = [_dot_nt(ar[i], s[i].astype(BF16)) for i in groups]
    av = [_dot(a_ak[i], v_st[i].astype(BF16)) for i in groups]
    u = [_dot(tinv[i].astype(BF16), (y[i][:MXU_DIM] + av[i]).astype(BF16)) for i in groups]
    uv = [jnp.concatenate([u[i], v_st[i]], axis=0).astype(BF16) for i in groups]
    o_st = [y[i][MXU_DIM:] + _dot(a_r[i], uv[i]) for i in groups]
    e_last = [jnp.exp(last[i] - cum[i]) for i in groups]
    bk_hat = [jnp.concatenate([stack_masked(kka[i] * e_last[i]), stack_masked(kmod[i] * e_last[i])],
                              axis=0).astype(BF16) for i in groups]
    for i in groups:
        s_ref[i] = s[i] * jnp.exp(last[i]) + _dot_tn(uv[i], bk_hat[i])

    o = [functools.reduce(lambda p, q: p + q,
                          [x[hd * c_len:(hd + 1) * c_len] for hd in range(HEADS_PER_GROUP)]) for x in o_st]
    inv_n = 1.0 / RWKV_HEAD
    mu = [head_sum(x) * inv_n for x in o]
    bonus_dot = [head_sum(r[i] * kmod[i] * rk_ref[:, sls[i]]) for i in groups]
    dev = [o[i] - mu[i] for i in groups]
    var = [head_sum(x * x) * inv_n for x in dev]
    for i in groups:
        sl = sls[i]
        o_n = dev[i] * lax.rsqrt(var[i] + RWKV_GN_EPS) * lw_ref[:, sl] + lb_ref[:, sl]
        y_ref[:, sl] = ((o_n + bonus_dot[i] * v[i]) * g_ref[:, sl]).astype(y_ref.dtype)


def _wkv7(r, w, k, v, a, g, k_k, k_a, r_k, lnx_w, lnx_b, s0_bd):
    b, t, d = r.shape
    n_groups = d // MXU_DIM
    row_spec = pl.BlockSpec((None, CHUNK, d), lambda i, j: (i, j, 0))
    vec_spec = pl.BlockSpec((1, d), lambda i, j: (0, 0))
    st_spec = pl.BlockSpec((None, n_groups, MXU_DIM, MXU_DIM), lambda i, j: (i, 0, 0, 0))
    vec = lambda p: p.reshape(1, d)
    return pl.pallas_call(
        _wkv7_kernel,
        grid=(b, t // CHUNK),
        in_specs=[row_spec] * 6 + [vec_spec] * 5 + [st_spec],
        out_specs=[pl.BlockSpec((CHUNK, d), lambda i, j: (i * (t // CHUNK) + j, 0)), st_spec],
        out_shape=[jax.ShapeDtypeStruct((b * t, d), BF16),
                   jax.ShapeDtypeStruct(s0_bd.shape, F32)],
        compiler_params=_params(("arbitrary", "arbitrary")),
        name="wkv7",
    )(r, w, k, v, a, g, vec(k_k), vec(k_a), vec(r_k), vec(lnx_w), vec(lnx_b), s0_bd)


def _wkv_state_to_blockdiag(s):
    b, h, n, _ = s.shape
    g = h // HEADS_PER_GROUP
    s = s.reshape(b, g, HEADS_PER_GROUP, n, n)
    eye = jnp.eye(HEADS_PER_GROUP, dtype=s.dtype)
    return jnp.einsum("bghvk,hj->bghvjk", s, eye).reshape(b, g, MXU_DIM, MXU_DIM)


def _wkv_state_from_blockdiag(s_bd):
    b, g = s_bd.shape[:2]
    n = RWKV_HEAD
    s = s_bd.reshape(b, g, HEADS_PER_GROUP, n, HEADS_PER_GROUP, n)
    s = jnp.stack([s[:, :, i, :, i, :] for i in range(HEADS_PER_GROUP)], axis=2)
    return s.reshape(b, g * HEADS_PER_GROUP, n, n)


def _gla_kernel(q_ref, k_ref, v_ref, gate_ref, lr_ref, w2_ref, gb_ref, hn_ref, s0_ref, y_ref, s_ref):
    tc = q_ref.shape[0]
    n_heads, dv, dk = s_ref.shape
    heads = range(n_heads)
    ksl = [slice(h * dk, (h + 1) * dk) for h in heads]
    vsl = [slice(h * dv, (h + 1) * dv) for h in heads]

    @pl.when(pl.program_id(1) == 0)
    def _():
        s_ref[...] = s0_ref[...]

    tri = _tri_incl(CHUNK)
    tri_b = jnp.where(tri, 1.0, 0.0).astype(BF16)
    scale = dk ** -0.5
    for c in range(tc // CHUNK):
        rows = slice(c * CHUNK, (c + 1) * CHUNK)
        lr = lr_ref[rows, :].astype(BF16)
        gk = [_log_sigmoid(_dot(lr, w2_ref[:, sl]) + gb_ref[:, sl]) * (1.0 / GLA_GATE_NORMALIZER)
              for sl in ksl]
        cum = [_split_dot(tri_b, x, 3) for x in gk]
        last = [x[CHUNK - 1:CHUNK, :] for x in cum]
        qe = [(q_ref[rows, ksl[h]] * scale * jnp.exp(cum[h])).astype(BF16) for h in heads]
        ke = [(k_ref[rows, ksl[h]] * jnp.exp(-cum[h])).astype(BF16) for h in heads]
        k2 = [(k_ref[rows, ksl[h]] * jnp.exp(last[h] - cum[h])).astype(BF16) for h in heads]
        vb = [v_ref[rows, sl].astype(BF16) for sl in vsl]
        scores = [jnp.where(tri, _dot_nt(qe[h], ke[h]), 0.0).astype(BF16) for h in heads]
        s_t = [s_ref[h] for h in heads]
        o = [_dot(scores[h], vb[h]) + _dot_nt(qe[h], s_t[h].astype(BF16)) for h in heads]
        for h in heads:
            s_ref[h] = s_t[h] * jnp.exp(last[h]) + _dot_tn(vb[h], k2[h])
        for h in heads:
            o_n = _rms(o[h], GLA_NORM_EPS) * hn_ref[...]
            y_ref[rows, vsl[h]] = (o_n * _silu(gate_ref[rows, vsl[h]])).astype(y_ref.dtype)


def _gla(qkvg, lr, gk_w2, gk_b, head_norm, s0_t):
    b, t, _ = qkvg.shape
    h, dv, dk = s0_t.shape[1:]
    tc = min(t, 2 * CHUNK)
    dk_all, dv_all = h * dk, h * dv
    assert dv_all == 2 * dk_all
    st_spec = pl.BlockSpec((None, h, dv, dk), lambda i, c: (i, 0, 0, 0))
    return pl.pallas_call(
        _gla_kernel,
        grid=(b, t // tc),
        in_specs=[pl.BlockSpec((None, tc, dk_all), lambda i, c: (i, c, 0)),
                  pl.BlockSpec((None, tc, dk_all), lambda i, c: (i, c, 1)),
                  pl.BlockSpec((None, tc, dv_all), lambda i, c: (i, c, 1)),
                  pl.BlockSpec((None, tc, dv_all), lambda i, c: (i, c, 2)),
                  pl.BlockSpec((None, tc, LANES), lambda i, c: (i, c, 0)),
                  pl.BlockSpec((LANES, dk_all), lambda i, c: (0, 0)),
                  pl.BlockSpec((1, dk_all), lambda i, c: (0, 0)),
                  pl.BlockSpec((1, dv), lambda i, c: (0, 0)),
                  st_spec],
        out_specs=[pl.BlockSpec((tc, dv_all), lambda i, c: (i * (t // tc) + c, 0)), st_spec],
        out_shape=[jax.ShapeDtypeStruct((b * t, dv_all), BF16),
                   jax.ShapeDtypeStruct(s0_t.shape, F32)],
        compiler_params=_params(("arbitrary", "arbitrary")),
        name="gla",
    )(qkvg, qkvg, qkvg, qkvg, lr, gk_w2, gk_b.reshape(1, -1), head_norm.reshape(1, dv), s0_t)


def _ffn_kernel(x_ref, g_ref, sv_ref, sg_ref, wv_ref, wg_ref, cwv_ref, cwg_ref, cbv_ref, cbg_ref,
                wd_ref, og_ref, o_ref, nsv_ref, nsg_ref, h_ref, ua_ref, ub_ref, carry_ref, *, nj,
                norm_output):
    nseq, seq_len, d = x_ref.shape
    rows = nseq * seq_len
    tn = wv_ref.shape[1]
    j = pl.program_id(2)
    jt = jnp.maximum(j - 1, 0)

    @pl.when(j == 0)
    def _():
        x = x_ref[...].reshape(rows, d)
        h_ref[...] = (_rms(x, NORM_EPS) * g_ref[...]).astype(BF16)
        o_ref[...] = x_ref[...]

    @pl.when((pl.program_id(1) == 0) & (j > 0))
    def _():
        carry_ref[2 * jt] = sv_ref[...]
        carry_ref[2 * jt + 1] = sg_ref[...]

    cols = [slice(c * MXU_DIM, (c + 1) * MXU_DIM) for c in range(tn // MXU_DIM)]
    first = lax.broadcasted_iota(jnp.int32, (1, 8, 1), 1)

    def conv(u, cs, cw_ref, cb_ref, slot, ns_ref):
        u = u.reshape(nseq, seq_len, MXU_DIM)
        w0, w1, w2, cb = cw_ref[0:1, cs], cw_ref[1:2, cs], cw_ref[2:3, cs], cb_ref[:, cs]
        r1, r2 = pltpu.roll(u, 1, axis=1), pltpu.roll(u, 2, axis=1)
        body = cb + w0 * r2 + w1 * r1 + w2 * u
        p2, p1 = carry_ref[slot, :, 0:1, cs], carry_ref[slot, :, 1:2, cs]
        u1 = jnp.where(first == 0, p1, r1[:, :8, :])
        u2 = jnp.where(first == 0, p2, jnp.where(first == 1, p1, r2[:, :8, :]))
        head = cb + w0 * u2 + w1 * u1 + w2 * u[:, :8, :]
        tail = u[:, seq_len - 2:, :]
        carry_ref[slot, :, :, cs] = tail
        ns_ref[:, :, cs] = tail
        return jnp.concatenate([head, body[:, 8:, :]], axis=1)

    def step(up_ref, dn_ref):
        h = None if up_ref is None else h_ref[...]
        acc = None
        for cs in cols:
            if up_ref is not None:
                up_ref[0, :, cs] = _dot(h, wv_ref[:, cs])
            if dn_ref is not None:
                val = conv(dn_ref[0, :, cs], cs, cwv_ref, cbv_ref, 2 * jt, nsv_ref)
            if up_ref is not None:
                up_ref[1, :, cs] = _dot(h, wg_ref[:, cs])
            if dn_ref is not None:
                gate = conv(dn_ref[1, :, cs], cs, cwg_ref, cbg_ref, 2 * jt + 1, nsg_ref)
                act = (_silu(gate) * val).reshape(rows, MXU_DIM).astype(BF16)
                part = _dot(act, wd_ref[cs, :])
                acc = part if acc is None else acc + part
        if dn_ref is not None:
            o_ref[...] += acc.reshape(nseq, seq_len, d)

    even = (j & 1) == 0
    inner = (j > 0) & (j < nj)

    @pl.when(j == 0)
    def _():
        step(ua_ref, None)

    @pl.when(inner & even)
    def _():
        step(ua_ref, ub_ref)

    @pl.when(inner & jnp.logical_not(even))
    def _():
        step(ub_ref, ua_ref)

    @pl.when(j == nj)
    def _():
        step(None, ua_ref if (nj - 1) % 2 == 0 else ub_ref)
        if norm_output:
            done = o_ref[...].reshape(rows, d)
            o_ref[...] = (_rms(done, NORM_EPS) * og_ref[...]).reshape(nseq, seq_len, d)


FFN_TILE_COLS = 512


def _conv_ffn(x, conv_state, g, w_up, conv_w, conv_b, w_down, out_g, norm_output, nseq, seq_len):
    b, t, d = x.shape
    f = w_down.shape[0]
    tn = FFN_TILE_COLS
    nj = f // tn
    assert b % nseq == 0 and t % seq_len == 0 and f % tn == 0
    x_spec = pl.BlockSpec((nseq, seq_len, d), lambda i, s, j: (i, s, 0))
    up_tile = lambda j: jnp.minimum(j, nj - 1)
    dn_tile = lambda j: jnp.maximum(j - 1, 0)
    st_v = pl.BlockSpec((nseq, 2, tn), lambda i, s, j: (i, 0, dn_tile(j)))
    st_g = pl.BlockSpec((nseq, 2, tn), lambda i, s, j: (i, 0, nj + dn_tile(j)))
    n_s = t // seq_len
    tail_spec = pl.BlockSpec((nseq, None, 2, tn), lambda i, s, j: (i, s, 0, dn_tile(j)))
    col_v = lambda n: pl.BlockSpec((n, tn), lambda i, s, j: (0, dn_tile(j)))
    col_g = lambda n: pl.BlockSpec((n, tn), lambda i, s, j: (0, nj + dn_tile(j)))
    y, ns_v, ns_g = pl.pallas_call(
        functools.partial(_ffn_kernel, nj=nj, norm_output=norm_output),
        grid=(b // nseq, n_s, nj + 1),
        in_specs=[x_spec, pl.BlockSpec((1, d), lambda i, s, j: (0, 0)), st_v, st_g,
                  pl.BlockSpec((d, tn), lambda i, s, j: (0, up_tile(j))),
                  pl.BlockSpec((d, tn), lambda i, s, j: (0, nj + up_tile(j))),
                  col_v(3), col_g(3), col_v(1), col_g(1),
                  pl.BlockSpec((tn, d), lambda i, s, j: (dn_tile(j), 0)),
                  pl.BlockSpec((1, d), lambda i, s, j: (0, 0))],
        out_specs=[x_spec, tail_spec, tail_spec],
        out_shape=[jax.ShapeDtypeStruct((b, t, d), F32),
                   jax.ShapeDtypeStruct((b, n_s, 2, f), F32),
                   jax.ShapeDtypeStruct((b, n_s, 2, f), F32)],
        scratch_shapes=[pltpu.VMEM((nseq * seq_len, d), BF16),
                        pltpu.VMEM((2, nseq * seq_len, tn), F32),
                        pltpu.VMEM((2, nseq * seq_len, tn), F32),
                        pltpu.VMEM((2 * nj, nseq, 2, tn), F32)],
        compiler_params=_params(("arbitrary", "arbitrary", "arbitrary")),
        name="conv_ffn",
    )(x, g.reshape(1, d), conv_state, conv_state, w_up, w_up, conv_w, conv_w,
      conv_b.reshape(1, -1), conv_b.reshape(1, -1), w_down, out_g.reshape(1, d))
    return y, jnp.concatenate([ns_v[:, -1], ns_g[:, -1]], axis=-1)


def _pad_cols(w, n):
    return jnp.pad(w, ((0, 0), (0, n - w.shape[1])))


def _pad_rows(w, n):
    return jnp.pad(w, ((0, n - w.shape[0]), (0, 0)))


def _rwkv_layer(x, shift_st, wkv_st, v_first, p):
    b, t, d = x.shape
    m = b * t
    pre, new_shift = _rwkv_pre(x, shift_st, p["norm"], p["mix"], p["w1"], p["a1"], p["g1"], p["v1"])
    xr, xk, xv, w_lora, a_lora, g_lora = pre[:6]
    flat = lambda z: z.reshape(m, d)
    row = lambda z: z.reshape(1, d)
    r = _matmul(xr, p["wr"], name="rwkv_r")
    k = _matmul(xk, p["wk"], name="rwkv_k")
    v = _matmul(xv, p["wv"], name="rwkv_v")
    w = _matmul(w_lora, p["w2"], extras=(row(p["w0"]),),
                epilogue=lambda acc, w0: _log_sigmoid(w0 + acc) - 0.5, name="rwkv_w2")
    a = _matmul(a_lora, p["a2"], extras=(row(p["a0"]),),
                epilogue=lambda acc, a0: _sigmoid(a0 + acc), name="rwkv_a2")
    g = _matmul(g_lora, p["g2"], name="rwkv_g2")
    if p["v1"] is None:
        v_first = v
    else:
        v = _matmul(pre[6], p["v2"], extras=(row(p["v0"]), v, v_first), tn=512,
                    epilogue=lambda acc, v0, vv, vf: vv + (vf - vv) * _sigmoid(v0 + acc),
                    name="rwkv_v2")
    seq = lambda z: z.reshape(b, t, d)
    y, s_bd = _wkv7(seq(r), seq(w), seq(k), seq(v), seq(a), seq(g), p["k_k"], p["k_a"], p["r_k"],
                    p["lnx_w"], p["lnx_b"], _wkv_state_to_blockdiag(wkv_st))
    x_new = _matmul(y, p["wo"], extras=(flat(x),), tn=512,
                    epilogue=lambda acc, res: res + acc, name="rwkv_o")
    return seq(x_new), new_shift, _wkv_state_from_blockdiag(s_bd), v_first


def _gla_layer(x, gla_st, p):
    b, t, d = x.shape
    m = b * t
    h = _rmsnorm(x.reshape(m, d), p["norm"], BF16)
    qkvg = _matmul(h, p["w_main"], name="gla_in")
    lr = _matmul(h, p["w_lr"], name="gla_lr")
    y, s_t = _gla(qkvg.reshape(b, t, -1), lr.reshape(b, t, LANES), p["gk_w2"], p["gk_b"],
                  p["head_norm"], jnp.swapaxes(gla_st, -1, -2))
    x_new = _matmul(y, p["wo"], extras=(x.reshape(m, d),), tn=512,
                    epilogue=lambda acc, res: res + acc, name="gla_o")
    return x_new.reshape(b, t, d), jnp.swapaxes(s_t, -1, -2)


FFN_TILE_ROWS = 512


def _ffn_tile(b, t):
    if t >= FFN_TILE_ROWS:
        return 1, FFN_TILE_ROWS
    return min(b, FFN_TILE_ROWS // t), t


def _run_trunk(x, shift_st, wkv_st, gla_st, conv_st, layers, norm_final, ffn_nseq, ffn_len):
    new_shift, new_wkv, new_gla, new_conv = [], [], [], []
    v_first = None
    for i, (mixer, ffn) in enumerate(layers):
        j = i // 2
        if i % 2 == 0:
            x, s_shift, s_wkv, v_first = _rwkv_layer(x, shift_st[j], wkv_st[j], v_first, mixer)
            new_shift.append(s_shift)
            new_wkv.append(s_wkv)
        else:
            x, s_gla = _gla_layer(x, gla_st[j], mixer)
            new_gla.append(s_gla)
        x, s_conv = _conv_ffn(x, conv_st[i], ffn["norm"], ffn["w_up"], ffn["conv_w"], ffn["conv_b"],
                              ffn["w_down"], norm_final, i == len(layers) - 1, ffn_nseq, ffn_len)
        new_conv.append(s_conv)
    return x, jnp.stack(new_shift), jnp.stack(new_wkv), jnp.stack(new_gla), jnp.stack(new_conv)


def kernel(x_prompt, x_sample, state_rwkv_shift, state_rwkv_wkv, state_gla, state_ffn_conv, norm_mix, norm_ffn, norm_final, rwkv_mix, rwkv_w0, rwkv_w1, rwkv_w2, rwkv_a0, rwkv_a1, rwkv_a2, rwkv_v0, rwkv_v1, rwkv_v2, rwkv_g1, rwkv_g2, rwkv_k_k, rwkv_k_a, rwkv_r_k, rwkv_wr, rwkv_wk, rwkv_wv, rwkv_wo, rwkv_lnx_w, rwkv_lnx_b, gla_w_in, gla_gk_w2, gla_gk_b, gla_head_norm, gla_wo, ffn_w_up, ffn_conv_w, ffn_conv_b, ffn_w_down):
    depth = norm_mix.shape[0]
    d = x_prompt.shape[-1]
    bf = lambda w: w.astype(BF16)
    dk_total = gla_gk_w2.shape[-1]
    n_main = gla_w_in.shape[-1] - GLA_GATE_RANK
    layers = []
    for i in range(depth):
        j = i // 2
        if i % 2 == 0:
            has_vres = j > 0
            mixer = dict(
                norm=norm_mix[i], mix=rwkv_mix[j],
                wr=bf(rwkv_wr[j]), wk=bf(rwkv_wk[j]), wv=bf(rwkv_wv[j]), wo=bf(rwkv_wo[j]),
                w0=rwkv_w0[j], w1=bf(_pad_cols(rwkv_w1[j], LANES)), w2=bf(_pad_rows(rwkv_w2[j], LANES)),
                a0=rwkv_a0[j], a1=bf(_pad_cols(rwkv_a1[j], LANES)), a2=bf(_pad_rows(rwkv_a2[j], LANES)),
                g1=bf(rwkv_g1[j]), g2=bf(rwkv_g2[j]),
                v0=rwkv_v0[j - 1] if has_vres else None,
                v1=bf(_pad_cols(rwkv_v1[j - 1], LANES)) if has_vres else None,
                v2=bf(_pad_rows(rwkv_v2[j - 1], LANES)) if has_vres else None,
                k_k=rwkv_k_k[j], k_a=rwkv_k_a[j], r_k=rwkv_r_k[j].reshape(d),
                lnx_w=rwkv_lnx_w[j], lnx_b=rwkv_lnx_b[j])
        else:
            mixer = dict(
                norm=norm_mix[i],
                w_main=bf(gla_w_in[j][:, :n_main]),
                w_lr=bf(_pad_cols(gla_w_in[j][:, n_main:], LANES)),
                gk_w2=bf(_pad_rows(gla_gk_w2[j], LANES)), gk_b=gla_gk_b[j],
                head_norm=gla_head_norm[j], wo=bf(gla_wo[j]))
        ffn = dict(norm=norm_ffn[i], w_up=bf(ffn_w_up[i]), conv_w=ffn_conv_w[i],
                   conv_b=ffn_conv_b[i],
                   w_down=bf(ffn_w_down[i]))
        layers.append((mixer, ffn))

    bp, tp, _ = x_prompt.shape
    bs, ts, _ = x_sample.shape
    zeros_like_state = lambda s: jnp.zeros((s.shape[0], bp) + s.shape[2:], s.dtype)
    out_p = _run_trunk(x_prompt, zeros_like_state(state_rwkv_shift), zeros_like_state(state_rwkv_wkv),
                       zeros_like_state(state_gla), zeros_like_state(state_ffn_conv), layers,
                       norm_final, *_ffn_tile(bp, tp))
    out_s = _run_trunk(x_sample, state_rwkv_shift, state_rwkv_wkv, state_gla, state_ffn_conv, layers,
                       norm_final, *_ffn_tile(bs, ts))
    return (out_p[0], out_s[0]) + tuple(out_p[1:]) + tuple(out_s[1:])
```

```python
import functools

import jax
import jax.numpy as jnp
from jax import lax
from jax.experimental import pallas as pl
from jax.experimental.pallas import tpu as pltpu

F32 = jnp.float32
BF16 = jnp.bfloat16

CHUNK = 64
RWKV_HEAD = 64
RWKV_GN_EPS = 64e-5
GLA_HEADS = 4
GLA_GATE_RANK = 16
GLA_GATE_NORMALIZER = 16.0
GLA_NORM_EPS = 1e-5
NORM_EPS = 1e-6

LANES = 128
MXU_DIM = 256
HEADS_PER_GROUP = MXU_DIM // RWKV_HEAD
VMEM_LIMIT = 56 * 1024 * 1024


def _params(sem):
    return pltpu.CompilerParams(dimension_semantics=sem, vmem_limit_bytes=VMEM_LIMIT)


def _dot(a, b):
    return jnp.dot(a, b, preferred_element_type=F32)


def _dot_nt(a, b):
    return lax.dot_general(a, b, (((1,), (1,)), ((), ())), preferred_element_type=F32)


def _dot_tn(a, b):
    return lax.dot_general(a, b, (((0,), (0,)), ((), ())), preferred_element_type=F32)


def _split_dot(fixed_bf16, x, terms):
    acc = None
    rem = x
    for _ in range(terms):
        piece = rem.astype(BF16)
        rem = rem - piece.astype(F32)
        part = _dot(fixed_bf16, piece)
        acc = part if acc is None else acc + part
    return acc


def _split_dot_right(x, fixed_bf16, terms):
    acc = None
    rem = x
    for _ in range(terms):
        piece = rem.astype(BF16)
        rem = rem - piece.astype(F32)
        part = _dot(piece, fixed_bf16)
        acc = part if acc is None else acc + part
    return acc


def _log_sigmoid(x):
    return jnp.minimum(x, 0.0) - jnp.log(1.0 + jnp.exp(-jnp.abs(x)))


def _sigmoid(x):
    return 1.0 / (1.0 + jnp.exp(-x))


def _silu(x):
    return x * _sigmoid(x)


def _rms(x, eps):
    return x * lax.rsqrt(jnp.mean(x * x, axis=-1, keepdims=True) + eps)


def _tri_incl(n):
    r = lax.broadcasted_iota(jnp.int32, (n, n), 0)
    c = lax.broadcasted_iota(jnp.int32, (n, n), 1)
    return c <= r


def _rmsnorm_kernel(x_ref, g_ref, o_ref):
    o_ref[...] = (_rms(x_ref[...], NORM_EPS) * g_ref[...]).astype(o_ref.dtype)


def _rmsnorm(x, g, out_dtype):
    m, d = x.shape
    tm = min(m, 1024)
    return pl.pallas_call(
        _rmsnorm_kernel,
        grid=(m // tm,),
        in_specs=[pl.BlockSpec((tm, d), lambda i: (i, 0)), pl.BlockSpec((1, d), lambda i: (0, 0))],
        out_specs=pl.BlockSpec((tm, d), lambda i: (i, 0)),
        out_shape=jax.ShapeDtypeStruct((m, d), out_dtype),
        compiler_params=_params(("arbitrary",)),
        name="rmsnorm",
    )(x, g.reshape(1, d))


def _mm_kernel(a_ref, b_ref, *rest, epilogue, n_extra):
    o_ref = rest[n_extra]
    acc = _dot(a_ref[...], b_ref[...])
    if epilogue is not None:
        acc = epilogue(acc, *[e[...] for e in rest[:n_extra]])
    o_ref[...] = acc.astype(o_ref.dtype)


def _matmul(a, b, extras=(), epilogue=None, out_dtype=F32, tm=1024, tn=1024, name="matmul"):
    m, k = a.shape
    n = b.shape[1]
    tm, tn = min(tm, m), min(tn, n)
    assert m % tm == 0 and n % tn == 0, (m, n, tm, tn)
    in_specs = [pl.BlockSpec((tm, k), lambda i, j: (i, 0)), pl.BlockSpec((k, tn), lambda i, j: (0, j))]
    for e in extras:
        if e.shape[0] == 1:
            in_specs.append(pl.BlockSpec((1, tn), lambda i, j: (0, j)))
        else:
            in_specs.append(pl.BlockSpec((tm, tn), lambda i, j: (i, j)))
    return pl.pallas_call(
        functools.partial(_mm_kernel, epilogue=epilogue, n_extra=len(extras)),
        grid=(m // tm, n // tn),
        in_specs=in_specs,
        out_specs=pl.BlockSpec((tm, tn), lambda i, j: (i, j)),
        out_shape=jax.ShapeDtypeStruct((m, n), out_dtype),
        compiler_params=_params(("arbitrary", "arbitrary")),
        name=name,
    )(a, b, *extras)


def _rwkv_pre_kernel(x_ref, st_ref, g_ref, mix_ref, w1_ref, a1_ref, g1_ref, *rest, has_v1):
    if has_v1:
        v1_ref, rest = rest[0], rest[1:]
    xr_ref, xk_ref, xv_ref, lw_ref, la_ref, lg_ref = rest[:6]
    rest = rest[6:]
    if has_v1:
        lv_ref, rest = rest[0], rest[1:]
    hl_ref, carry_ref = rest
    tt = x_ref.shape[0]

    @pl.when(pl.program_id(1) == 0)
    def _():
        carry_ref[...] = st_ref[...]

    h = _rms(x_ref[...], NORM_EPS) * g_ref[...]
    row = lax.broadcasted_iota(jnp.int32, (tt, 1), 0)
    h_prev = jnp.where(row == 0, carry_ref[...], pltpu.roll(h, 1, axis=0))
    carry_ref[...] = h[tt - 1:tt, :]
    hl_ref[...] = h[tt - 8:, :]
    xx = h_prev - h
    mixed = lambda i: (h + xx * mix_ref[i:i + 1, :]).astype(BF16)
    xr_ref[...] = mixed(0)
    xk_ref[...] = mixed(2)
    xv = mixed(3)
    xv_ref[...] = xv
    lw_ref[...] = jnp.tanh(_dot(mixed(1), w1_ref[...])).astype(BF16)
    la_ref[...] = _dot(mixed(4), a1_ref[...]).astype(BF16)
    lg_ref[...] = _sigmoid(_dot(mixed(5), g1_ref[...])).astype(BF16)
    if has_v1:
        lv_ref[...] = _dot(xv, v1_ref[...]).astype(BF16)


def _rwkv_pre(x, shift_state, g, mix, w1, a1, g1, v1):
    b, t, d = x.shape
    tt = min(t, 512)
    nt = t // tt
    has_v1 = v1 is not None
    lora = [w1, a1, g1] + ([v1] if has_v1 else [])
    flat_spec = lambda n: pl.BlockSpec((tt, n), lambda i, j: (i * nt + j, 0))
    full_spec = lambda w: pl.BlockSpec(w.shape, lambda i, j: (0, 0))
    wide = [jax.ShapeDtypeStruct((b * t, d), BF16)] * 3
    narrow = [jax.ShapeDtypeStruct((b * t, w.shape[1]), BF16) for w in lora]
    outs = pl.pallas_call(
        functools.partial(_rwkv_pre_kernel, has_v1=has_v1),
        grid=(b, nt),
        in_specs=[pl.BlockSpec((None, tt, d), lambda i, j: (i, j, 0)),
                  pl.BlockSpec((None, 1, d), lambda i, j: (i, 0, 0)),
                  pl.BlockSpec((1, d), lambda i, j: (0, 0)),
                  pl.BlockSpec((6, d), lambda i, j: (0, 0))] + [full_spec(w) for w in lora],
        out_specs=[flat_spec(d)] * 3 + [flat_spec(w.shape[1]) for w in lora]
                  + [pl.BlockSpec((None, 8, d), lambda i, j: (i, 0, 0))],
        out_shape=wide + narrow + [jax.ShapeDtypeStruct((b, 8, d), F32)],
        scratch_shapes=[pltpu.VMEM((1, d), F32)],
        compiler_params=_params(("arbitrary", "arbitrary")),
        name="rwkv_pre",
    )(x, shift_state.reshape(b, 1, d), g.reshape(1, d), mix, *lora)
    return outs[:-1], outs[-1][:, 7, :]


def _wkv7_kernel(*refs, has_vres):
    refs = list(refs)
    r_ref, k_ref, v_ref, lw_ref, la_ref, lg_ref = refs[:6]
    w2_ref, a2_ref, g2_ref, w0_ref, a0_ref = refs[6:11]
    refs = refs[11:]
    if has_vres:
        vf_ref, lv_ref, v2_ref, v0_ref = refs[:4]
        refs = refs[4:]
    kk_ref, ka_ref, rk_ref, gnw_ref, gnb_ref, s0_ref, y_ref, s_ref = refs
    c_len, d = r_ref.shape
    n_groups = d // MXU_DIM

    @pl.when(pl.program_id(1) == 0)
    def _():
        s_ref[...] = s0_ref[...]

    ri = lax.broadcasted_iota(jnp.int32, (MXU_DIM, MXU_DIM), 0)
    ci = lax.broadcasted_iota(jnp.int32, (MXU_DIM, MXU_DIM), 1)
    head_bits = RWKV_HEAD.bit_length() - 1
    same_head = (ri >> head_bits) == (ci >> head_bits)
    bd = jnp.where(same_head, 1.0, 0.0)
    bd_b = bd.astype(BF16)
    tri_b = jnp.where(_tri_incl(c_len), 1.0, 0.0).astype(BF16)

    assert c_len == RWKV_HEAD and 2 * RWKV_HEAD == LANES
    t_c = lax.broadcasted_iota(jnp.int32, (MXU_DIM, LANES), 0) & (c_len - 1)
    lane_c = lax.broadcasted_iota(jnp.int32, (MXU_DIM, LANES), 1)
    s_c = lane_c & (RWKV_HEAD - 1)
    low_half = lane_c < RWKV_HEAD
    strict_c = s_c < t_c
    incl_c = s_c <= t_c
    eye_c = jnp.where(s_c == t_c, 1.0, 0.0)
    lane_t = lax.broadcasted_iota(jnp.int32, (c_len, LANES), 1)
    half_masks = (lane_t < RWKV_HEAD, lane_t >= RWKV_HEAD)

    def stack(x):
        return jnp.concatenate([x] * HEADS_PER_GROUP, axis=0)

    def stack_masked(x):
        return stack(x) * bd

    def head_sum(x):
        return _split_dot_right(x, bd_b, 1)

    def expand(xc, half):
        zeros = jnp.zeros((c_len, LANES), F32)
        blocks = []
        for hd in range(HEADS_PER_GROUP):
            src = xc[hd * c_len:(hd + 1) * c_len]
            if hd % 2 != half:
                src = pltpu.roll(src, RWKV_HEAD, axis=1)
            tile = jnp.where(half_masks[hd % 2], src, 0.0)
            blocks.append(jnp.concatenate(
                [tile if lt == hd // 2 else zeros for lt in range(MXU_DIM // LANES)], axis=1))
        return jnp.concatenate(blocks, axis=0)

    groups = range(n_groups)
    sls = [slice(grp * MXU_DIM, (grp + 1) * MXU_DIM) for grp in groups]
    r = [r_ref[:, sl] for sl in sls]
    k = [k_ref[:, sl] for sl in sls]
    v = [v_ref[:, sl] for sl in sls]
    lw_n, la_n, lg_n = lw_ref[...], la_ref[...], lg_ref[...]
    w = [_log_sigmoid(w0_ref[:, sl] + _dot(lw_n, w2_ref[:, sl])) - 0.5 for sl in sls]
    a = [_sigmoid(a0_ref[:, sl] + _dot(la_n, a2_ref[:, sl])) for sl in sls]
    g = [_dot(lg_n, g2_ref[:, sl]) for sl in sls]
    if has_vres:
        lv_n = lv_ref[...]
        mix_v = [_sigmoid(v0_ref[:, sl] + _dot(lv_n, v2_ref[:, sl])) for sl in sls]
        v = [v[i] + (vf_ref[:, sls[i]] - v[i]) * mix_v[i] for i in groups]
    lw = [-jnp.exp(x) for x in w]
    kk = [k[i] * kk_ref[:, sls[i]] for i in groups]
    kk_ss = [head_sum(x * x) for x in kk]
    cum = [_split_dot(tri_b, x, 2) for x in lw]
    kk = [kk[i] / jnp.maximum(jnp.sqrt(kk_ss[i]), 1e-12) for i in groups]
    kmod = [k[i] * (1.0 + (a[i] - 1.0) * ka_ref[:, sls[i]]) for i in groups]
    kka = [kk[i] * a[i] for i in groups]
    last = [x[c_len - 1:c_len, :] for x in cum]
    e_neg = [jnp.exp(-x) for x in cum]
    ar = [jnp.concatenate([stack_masked(-kk[i] * jnp.exp(cum[i] - lw[i])),
                           stack_masked(r[i] * jnp.exp(cum[i]))], axis=0).astype(BF16) for i in groups]
    bk = [jnp.concatenate([kka[i] * e_neg[i], kmod[i] * e_neg[i]], axis=0).astype(BF16) for i in groups]
    z = [_dot_nt(ar[i], bk[i]) for i in groups]
    za = [jnp.where(strict_c, x[:MXU_DIM], 0.0) for x in z]
    zr = [jnp.where(incl_c, x[MXU_DIM:], 0.0) for x in z]
    a_ak = [expand(x, 1).astype(BF16) for x in za]
    a_r = [jnp.concatenate([expand(x, 0), expand(x, 1)], axis=1).astype(BF16) for x in zr]

    rhs = [jnp.where(low_half, x, eye_c) for x in za]
    p_bd = [expand(x, 0).astype(BF16) for x in za]
    n_steps = (c_len - 1).bit_length()
    for step in range(n_steps):
        x2 = [_dot(p_bd[i], rhs[i].astype(BF16)) for i in groups]
        rhs = [x2[i] + jnp.where(low_half, 0.0, rhs[i]) for i in groups]
        if step + 1 < n_steps:
            p_bd = [expand(x, 0).astype(BF16) for x in x2]
    tinv = [expand(x, 1) for x in rhs]

    v_st = [stack_masked(x) for x in v]
    s = [s_ref[i] for i in groups]
    y = [_dot_nt(ar[i], s[i].astype(BF16)) for i in groups]
    av = [_dot(a_ak[i], v_st[i].astype(BF16)) for i in groups]
    u = [_dot(tinv[i].astype(BF16), (y[i][:MXU_DIM] + av[i]).astype(BF16)) for i in groups]
    uv = [jnp.concatenate([u[i], v_st[i]], axis=0).astype(BF16) for i in groups]
    o_st = [y[i][MXU_DIM:] + _dot(a_r[i], uv[i]) for i in groups]
    e_last = [jnp.exp(last[i] - cum[i]) for i in groups]
    bk_hat = [jnp.concatenate([stack_masked(kka[i] * e_last[i]), stack_masked(kmod[i] * e_last[i])],
                              axis=0).astype(BF16) for i in groups]
    for i in groups:
        s_ref[i] = s[i] * jnp.exp(last[i]) + _dot_tn(uv[i], bk_hat[i])

    o = [functools.reduce(lambda p, q: p + q,
                          [x[hd * c_len:(hd + 1) * c_len] for hd in range(HEADS_PER_GROUP)]) for x in o_st]
    inv_n = 1.0 / RWKV_HEAD
    mu = [head_sum(x) * inv_n for x in o]
    bonus_dot = [head_sum(r[i] * kmod[i] * rk_ref[:, sls[i]]) for i in groups]
    dev = [o[i] - mu[i] for i in groups]
    var = [head_sum(x * x) * inv_n for x in dev]
    for i in groups:
        sl = sls[i]
        o_n = dev[i] * lax.rsqrt(var[i] + RWKV_GN_EPS) * gnw_ref[:, sl] + gnb_ref[:, sl]
        y_ref[:, sl] = ((o_n + bonus_dot[i] * v[i]) * g[i]).astype(y_ref.dtype)


def _wkv7(b, r, k, v, lora, lora_w, lora_b, vres, k_k, k_a, r_k, lnx_w, lnx_b, s0_bd):
    m, d = r.shape
    t = m // b
    n_chunks = t // CHUNK
    n_groups = d // MXU_DIM
    rows = lambda n: pl.BlockSpec((CHUNK, n), lambda i, j: (i * n_chunks + j, 0))
    whole = lambda w: pl.BlockSpec(w.shape, lambda i, j: (0, 0))
    vec_spec = pl.BlockSpec((1, d), lambda i, j: (0, 0))
    st_spec = pl.BlockSpec((None, n_groups, MXU_DIM, MXU_DIM), lambda i, j: (i, 0, 0, 0))
    vec = lambda p: p.reshape(1, d)
    args = [r, k, v, *lora, *lora_w, *[vec(x) for x in lora_b]]
    specs = ([rows(d)] * 3 + [rows(x.shape[1]) for x in lora] + [whole(w) for w in lora_w]
             + [vec_spec] * len(lora_b))
    if vres is not None:
        v_first, lv, v2, v0 = vres
        args += [v_first, lv, v2, vec(v0)]
        specs += [rows(d), rows(lv.shape[1]), whole(v2), vec_spec]
    args += [vec(k_k), vec(k_a), vec(r_k), vec(lnx_w), vec(lnx_b), s0_bd]
    specs += [vec_spec] * 5 + [st_spec]
    return pl.pallas_call(
        functools.partial(_wkv7_kernel, has_vres=vres is not None),
        grid=(b, n_chunks),
        in_specs=specs,
        out_specs=[rows(d), st_spec],
        out_shape=[jax.ShapeDtypeStruct((m, d), BF16),
                   jax.ShapeDtypeStruct(s0_bd.shape, F32)],
        compiler_params=_params(("arbitrary", "arbitrary")),
        name="wkv7",
    )(*args)


def _wkv_state_to_blockdiag(s):
    b, h, n, _ = s.shape
    g = h // HEADS_PER_GROUP
    s = s.reshape(b, g, HEADS_PER_GROUP, n, n)
    eye = jnp.eye(HEADS_PER_GROUP, dtype=s.dtype)
    return jnp.einsum("bghvk,hj->bghvjk", s, eye).reshape(b, g, MXU_DIM, MXU_DIM)


def _wkv_state_from_blockdiag(s_bd):
    b, g = s_bd.shape[:2]
    n = RWKV_HEAD
    s = s_bd.reshape(b, g, HEADS_PER_GROUP, n, HEADS_PER_GROUP, n)
    s = jnp.stack([s[:, :, i, :, i, :] for i in range(HEADS_PER_GROUP)], axis=2)
    return s.reshape(b, g * HEADS_PER_GROUP, n, n)


def _gla_kernel(q_ref, k_ref, v_ref, gate_ref, lr_ref, w2_ref, gb_ref, hn_ref, s0_ref, y_ref, s_ref):
    tc = q_ref.shape[0]
    n_heads, dv, dk = s_ref.shape
    heads = range(n_heads)
    ksl = [slice(h * dk, (h + 1) * dk) for h in heads]
    vsl = [slice(h * dv, (h + 1) * dv) for h in heads]

    @pl.when(pl.program_id(1) == 0)
    def _():
        s_ref[...] = s0_ref[...]

    tri = _tri_incl(CHUNK)
    tri_b = jnp.where(tri, 1.0, 0.0).astype(BF16)
    scale = dk ** -0.5
    for c in range(tc // CHUNK):
        rows = slice(c * CHUNK, (c + 1) * CHUNK)
        lr = lr_ref[rows, :].astype(BF16)
        gk = [_log_sigmoid(_dot(lr, w2_ref[:, sl]) + gb_ref[:, sl]) * (1.0 / GLA_GATE_NORMALIZER)
              for sl in ksl]
        cum = [_split_dot(tri_b, x, 3) for x in gk]
        last = [x[CHUNK - 1:CHUNK, :] for x in cum]
        qe = [(q_ref[rows, ksl[h]] * scale * jnp.exp(cum[h])).astype(BF16) for h in heads]
        ke = [(k_ref[rows, ksl[h]] * jnp.exp(-cum[h])).astype(BF16) for h in heads]
        k2 = [(k_ref[rows, ksl[h]] * jnp.exp(last[h] - cum[h])).astype(BF16) for h in heads]
        vb = [v_ref[rows, sl].astype(BF16) for sl in vsl]
        scores = [jnp.where(tri, _dot_nt(qe[h], ke[h]), 0.0).astype(BF16) for h in heads]
        s_t = [s_ref[h] for h in heads]
        o = [_dot(scores[h], vb[h]) + _dot_nt(qe[h], s_t[h].astype(BF16)) for h in heads]
        for h in heads:
            s_ref[h] = s_t[h] * jnp.exp(last[h]) + _dot_tn(vb[h], k2[h])
        for h in heads:
            o_n = _rms(o[h], GLA_NORM_EPS) * hn_ref[...]
            y_ref[rows, vsl[h]] = (o_n * _silu(gate_ref[rows, vsl[h]])).astype(y_ref.dtype)


def _gla(qkvg, lr, gk_w2, gk_b, head_norm, s0_t):
    b, t, _ = qkvg.shape
    h, dv, dk = s0_t.shape[1:]
    tc = min(t, 2 * CHUNK)
    dk_all, dv_all = h * dk, h * dv
    assert dv_all == 2 * dk_all
    st_spec = pl.BlockSpec((None, h, dv, dk), lambda i, c: (i, 0, 0, 0))
    return pl.pallas_call(
        _gla_kernel,
        grid=(b, t // tc),
        in_specs=[pl.BlockSpec((None, tc, dk_all), lambda i, c: (i, c, 0)),
                  pl.BlockSpec((None, tc, dk_all), lambda i, c: (i, c, 1)),
                  pl.BlockSpec((None, tc, dv_all), lambda i, c: (i, c, 1)),
                  pl.BlockSpec((None, tc, dv_all), lambda i, c: (i, c, 2)),
                  pl.BlockSpec((None, tc, LANES), lambda i, c: (i, c, 0)),
                  pl.BlockSpec((LANES, dk_all), lambda i, c: (0, 0)),
                  pl.BlockSpec((1, dk_all), lambda i, c: (0, 0)),
                  pl.BlockSpec((1, dv), lambda i, c: (0, 0)),
                  st_spec],
        out_specs=[pl.BlockSpec((tc, dv_all), lambda i, c: (i * (t // tc) + c, 0)), st_spec],
        out_shape=[jax.ShapeDtypeStruct((b * t, dv_all), BF16),
                   jax.ShapeDtypeStruct(s0_t.shape, F32)],
        compiler_params=_params(("arbitrary", "arbitrary")),
        name="gla",
    )(qkvg, qkvg, qkvg, qkvg, lr, gk_w2, gk_b.reshape(1, -1), head_norm.reshape(1, dv), s0_t)


def _ffn_kernel(x_ref, g_ref, sv_ref, sg_ref, wv_ref, wg_ref, cwv_ref, cwg_ref, cbv_ref, cbg_ref,
                wd_ref, og_ref, o_ref, nsv_ref, nsg_ref, h_ref, ua_ref, ub_ref, carry_ref, *, nj,
                norm_output):
    nseq, seq_len, d = x_ref.shape
    rows = nseq * seq_len
    tn = wv_ref.shape[1]
    j = pl.program_id(2)
    jt = jnp.maximum(j - 1, 0)

    @pl.when(j == 0)
    def _():
        x = x_ref[...].reshape(rows, d)
        h_ref[...] = (_rms(x, NORM_EPS) * g_ref[...]).astype(BF16)
        o_ref[...] = x_ref[...]

    @pl.when((pl.program_id(1) == 0) & (j > 0))
    def _():
        carry_ref[2 * jt] = sv_ref[...]
        carry_ref[2 * jt + 1] = sg_ref[...]

    cols = [slice(c * MXU_DIM, (c + 1) * MXU_DIM) for c in range(tn // MXU_DIM)]
    first = lax.broadcasted_iota(jnp.int32, (1, 8, 1), 1)

    def conv(u, cs, cw_ref, cb_ref, slot, ns_ref):
        u = u.reshape(nseq, seq_len, MXU_DIM)
        w0, w1, w2, cb = cw_ref[0:1, cs], cw_ref[1:2, cs], cw_ref[2:3, cs], cb_ref[:, cs]
        r1, r2 = pltpu.roll(u, 1, axis=1), pltpu.roll(u, 2, axis=1)
        body = cb + w0 * r2 + w1 * r1 + w2 * u
        p2, p1 = carry_ref[slot, :, 0:1, cs], carry_ref[slot, :, 1:2, cs]
        u1 = jnp.where(first == 0, p1, r1[:, :8, :])
        u2 = jnp.where(first == 0, p2, jnp.where(first == 1, p1, r2[:, :8, :]))
        head = cb + w0 * u2 + w1 * u1 + w2 * u[:, :8, :]
        tail = u[:, seq_len - 2:, :]
        carry_ref[slot, :, :, cs] = tail
        ns_ref[:, :, cs] = tail
        return jnp.concatenate([head, body[:, 8:, :]], axis=1)

    def step(up_ref, dn_ref):
        h = None if up_ref is None else h_ref[...]
        acc = None
        for cs in cols:
            if up_ref is not None:
                up_ref[0, :, cs] = _dot(h, wv_ref[:, cs])
            if dn_ref is not None:
                val = conv(dn_ref[0, :, cs], cs, cwv_ref, cbv_ref, 2 * jt, nsv_ref)
            if up_ref is not None:
                up_ref[1, :, cs] = _dot(h, wg_ref[:, cs])
            if dn_ref is not None:
                gate = conv(dn_ref[1, :, cs], cs, cwg_ref, cbg_ref, 2 * jt + 1, nsg_ref)
                act = (_silu(gate) * val).reshape(rows, MXU_DIM).astype(BF16)
                part = _dot(act, wd_ref[cs, :])
                acc = part if acc is None else acc + part
        if dn_ref is not None:
            o_ref[...] += acc.reshape(nseq, seq_len, d)

    even = (j & 1) == 0
    inner = (j > 0) & (j < nj)

    @pl.when(j == 0)
    def _():
        step(ua_ref, None)

    @pl.when(inner & even)
    def _():
        step(ua_ref, ub_ref)

    @pl.when(inner & jnp.logical_not(even))
    def _():
        step(ub_ref, ua_ref)

    @pl.when(j == nj)
    def _():
        step(None, ua_ref if (nj - 1) % 2 == 0 else ub_ref)
        if norm_output:
            done = o_ref[...].reshape(rows, d)
            o_ref[...] = (_rms(done, NORM_EPS) * og_ref[...]).reshape(nseq, seq_len, d)


FFN_TILE_COLS = 512


def _conv_ffn(x, conv_state, g, w_up, conv_w, conv_b, w_down, out_g, norm_output, nseq, seq_len):
    b, t, d = x.shape
    f = w_down.shape[0]
    tn = FFN_TILE_COLS
    nj = f // tn
    assert b % nseq == 0 and t % seq_len == 0 and f % tn == 0
    x_spec = pl.BlockSpec((nseq, seq_len, d), lambda i, s, j: (i, s, 0))
    up_tile = lambda j: jnp.minimum(j, nj - 1)
    dn_tile = lambda j: jnp.maximum(j - 1, 0)
    st_v = pl.BlockSpec((nseq, 2, tn), lambda i, s, j: (i, 0, dn_tile(j)))
    st_g = pl.BlockSpec((nseq, 2, tn), lambda i, s, j: (i, 0, nj + dn_tile(j)))
    n_s = t // seq_len
    tail_spec = pl.BlockSpec((nseq, None, 2, tn), lambda i, s, j: (i, s, 0, dn_tile(j)))
    col_v = lambda n: pl.BlockSpec((n, tn), lambda i, s, j: (0, dn_tile(j)))
    col_g = lambda n: pl.BlockSpec((n, tn), lambda i, s, j: (0, nj + dn_tile(j)))
    y, ns_v, ns_g = pl.pallas_call(
        functools.partial(_ffn_kernel, nj=nj, norm_output=norm_output),
        grid=(b // nseq, n_s, nj + 1),
        in_specs=[x_spec, pl.BlockSpec((1, d), lambda i, s, j: (0, 0)), st_v, st_g,
                  pl.BlockSpec((d, tn), lambda i, s, j: (0, up_tile(j))),
                  pl.BlockSpec((d, tn), lambda i, s, j: (0, nj + up_tile(j))),
                  col_v(3), col_g(3), col_v(1), col_g(1),
                  pl.BlockSpec((tn, d), lambda i, s, j: (dn_tile(j), 0)),
                  pl.BlockSpec((1, d), lambda i, s, j: (0, 0))],
        out_specs=[x_spec, tail_spec, tail_spec],
        out_shape=[jax.ShapeDtypeStruct((b, t, d), F32),
                   jax.ShapeDtypeStruct((b, n_s, 2, f), F32),
                   jax.ShapeDtypeStruct((b, n_s, 2, f), F32)],
        scratch_shapes=[pltpu.VMEM((nseq * seq_len, d), BF16),
                        pltpu.VMEM((2, nseq * seq_len, tn), F32),
                        pltpu.VMEM((2, nseq * seq_len, tn), F32),
                        pltpu.VMEM((2 * nj, nseq, 2, tn), F32)],
        compiler_params=_params(("arbitrary", "arbitrary", "arbitrary")),
        name="conv_ffn",
    )(x, g.reshape(1, d), conv_state, conv_state, w_up, w_up, conv_w, conv_w,
      conv_b.reshape(1, -1), conv_b.reshape(1, -1), w_down, out_g.reshape(1, d))
    return y, jnp.concatenate([ns_v[:, -1], ns_g[:, -1]], axis=-1)


def _pad_cols(w, n):
    return jnp.pad(w, ((0, 0), (0, n - w.shape[1])))


def _pad_rows(w, n):
    return jnp.pad(w, ((0, n - w.shape[0]), (0, 0)))


def _rwkv_layer(x, shift_st, wkv_st, v_first, p):
    b, t, d = x.shape
    m = b * t
    pre, new_shift = _rwkv_pre(x, shift_st, p["norm"], p["mix"], p["w1"], p["a1"], p["g1"], p["v1"])
    xr, xk, xv = pre[:3]
    r = _matmul(xr, p["wr"], name="rwkv_r")
    k = _matmul(xk, p["wk"], name="rwkv_k")
    v = _matmul(xv, p["wv"], name="rwkv_v")
    if p["v1"] is None:
        vres, v_first = None, v
    else:
        vres = (v_first, pre[6], p["v2"], p["v0"])
    y, s_bd = _wkv7(b, r, k, v, pre[3:6], (p["w2"], p["a2"], p["g2"]), (p["w0"], p["a0"]), vres,
                    p["k_k"], p["k_a"], p["r_k"], p["lnx_w"], p["lnx_b"],
                    _wkv_state_to_blockdiag(wkv_st))
    x_new = _matmul(y, p["wo"], extras=(x.reshape(m, d),),
                    epilogue=lambda acc, res: res + acc, name="rwkv_o")
    return x_new.reshape(b, t, d), new_shift, _wkv_state_from_blockdiag(s_bd), v_first


def _gla_layer(x, gla_st, p):
    b, t, d = x.shape
    m = b * t
    h = _rmsnorm(x.reshape(m, d), p["norm"], BF16)
    qkvg = _matmul(h, p["w_main"], name="gla_in")
    lr = _matmul(h, p["w_lr"], name="gla_lr")
    y, s_t = _gla(qkvg.reshape(b, t, -1), lr.reshape(b, t, LANES), p["gk_w2"], p["gk_b"],
                  p["head_norm"], jnp.swapaxes(gla_st, -1, -2))
    x_new = _matmul(y, p["wo"], extras=(x.reshape(m, d),),
                    epilogue=lambda acc, res: res + acc, name="gla_o")
    return x_new.reshape(b, t, d), jnp.swapaxes(s_t, -1, -2)


FFN_TILE_ROWS = 512


def _ffn_tile(b, t):
    if t >= FFN_TILE_ROWS:
        return 1, FFN_TILE_ROWS
    return min(b, FFN_TILE_ROWS // t), t


def _run_trunk(x, shift_st, wkv_st, gla_st, conv_st, layers, norm_final, ffn_nseq, ffn_len):
    new_shift, new_wkv, new_gla, new_conv = [], [], [], []
    v_first = None
    for i, (mixer, ffn) in enumerate(layers):
        j = i // 2
        if i % 2 == 0:
            x, s_shift, s_wkv, v_first = _rwkv_layer(x, shift_st[j], wkv_st[j], v_first, mixer)
            new_shift.append(s_shift)
            new_wkv.append(s_wkv)
        else:
            x, s_gla = _gla_layer(x, gla_st[j], mixer)
            new_gla.append(s_gla)
        x, s_conv = _conv_ffn(x, conv_st[i], ffn["norm"], ffn["w_up"], ffn["conv_w"], ffn["conv_b"],
                              ffn["w_down"], norm_final, i == len(layers) - 1, ffn_nseq, ffn_len)
        new_conv.append(s_conv)
    return x, jnp.stack(new_shift), jnp.stack(new_wkv), jnp.stack(new_gla), jnp.stack(new_conv)


def kernel(x_prompt, x_sample, state_rwkv_shift, state_rwkv_wkv, state_gla, state_ffn_conv, norm_mix, norm_ffn, norm_final, rwkv_mix, rwkv_w0, rwkv_w1, rwkv_w2, rwkv_a0, rwkv_a1, rwkv_a2, rwkv_v0, rwkv_v1, rwkv_v2, rwkv_g1, rwkv_g2, rwkv_k_k, rwkv_k_a, rwkv_r_k, rwkv_wr, rwkv_wk, rwkv_wv, rwkv_wo, rwkv_lnx_w, rwkv_lnx_b, gla_w_in, gla_gk_w2, gla_gk_b, gla_head_norm, gla_wo, ffn_w_up, ffn_conv_w, ffn_conv_b, ffn_w_down):
    depth = norm_mix.shape[0]
    d = x_prompt.shape[-1]
    bf = lambda w: w.astype(BF16)
    dk_total = gla_gk_w2.shape[-1]
    n_main = gla_w_in.shape[-1] - GLA_GATE_RANK
    layers = []
    for i in range(depth):
        j = i // 2
        if i % 2 == 0:
            has_vres = j > 0
            mixer = dict(
                norm=norm_mix[i], mix=rwkv_mix[j],
                wr=bf(rwkv_wr[j]), wk=bf(rwkv_wk[j]), wv=bf(rwkv_wv[j]), wo=bf(rwkv_wo[j]),
                w0=rwkv_w0[j], w1=bf(_pad_cols(rwkv_w1[j], LANES)), w2=bf(_pad_rows(rwkv_w2[j], LANES)),
                a0=rwkv_a0[j], a1=bf(_pad_cols(rwkv_a1[j], LANES)), a2=bf(_pad_rows(rwkv_a2[j], LANES)),
                g1=bf(rwkv_g1[j]), g2=bf(rwkv_g2[j]),
                v0=rwkv_v0[j - 1] if has_vres else None,
                v1=bf(_pad_cols(rwkv_v1[j - 1], LANES)) if has_vres else None,
                v2=bf(_pad_rows(rwkv_v2[j - 1], LANES)) if has_vres else None,
                k_k=rwkv_k_k[j], k_a=rwkv_k_a[j], r_k=rwkv_r_k[j].reshape(d),
                lnx_w=rwkv_lnx_w[j], lnx_b=rwkv_lnx_b[j])
        else:
            mixer = dict(
                norm=norm_mix[i],
                w_main=bf(gla_w_in[j][:, :n_main]),
                w_lr=bf(_pad_cols(gla_w_in[j][:, n_main:], LANES)),
                gk_w2=bf(_pad_rows(gla_gk_w2[j], LANES)), gk_b=gla_gk_b[j],
                head_norm=gla_head_norm[j], wo=bf(gla_wo[j]))
        ffn = dict(norm=norm_ffn[i], w_up=bf(ffn_w_up[i]), conv_w=ffn_conv_w[i],
                   conv_b=ffn_conv_b[i],
                   w_down=bf(ffn_w_down[i]))
        layers.append((mixer, ffn))

    bp, tp, _ = x_prompt.shape
    bs, ts, _ = x_sample.shape
    zeros_like_state = lambda s: jnp.zeros((s.shape[0], bp) + s.shape[2:], s.dtype)
    out_p = _run_trunk(x_prompt, zeros_like_state(state_rwkv_shift), zeros_like_state(state_rwkv_wkv),
                       zeros_like_state(state_gla), zeros_like_state(state_ffn_conv), layers,
                       norm_final, *_ffn_tile(bp, tp))
    out_s = _run_trunk(x_sample, state_rwkv_shift, state_rwkv_wkv, state_gla, state_ffn_conv, layers,
                       norm_final, *_ffn_tile(bs, ts))
    return (out_p[0], out_s[0]) + tuple(out_p[1:]) + tuple(out_s[1:])
```

```python
import functools

import jax
import jax.numpy as jnp
from jax import lax
from jax.experimental import pallas as pl
from jax.experimental.pallas import tpu as pltpu

F32 = jnp.float32
BF16 = jnp.bfloat16

CHUNK = 64
RWKV_HEAD = 64
RWKV_GN_EPS = 64e-5
GLA_HEADS = 4
GLA_GATE_RANK = 16
GLA_GATE_NORMALIZER = 16.0
GLA_NORM_EPS = 1e-5
NORM_EPS = 1e-6

LANES = 128
MXU_DIM = 256
HEADS_PER_GROUP = MXU_DIM // RWKV_HEAD
VMEM_LIMIT = 56 * 1024 * 1024


def _params(sem):
    return pltpu.CompilerParams(dimension_semantics=sem, vmem_limit_bytes=VMEM_LIMIT)


def _dot(a, b):
    return jnp.dot(a, b, preferred_element_type=F32)


def _dot_nt(a, b):
    return lax.dot_general(a, b, (((1,), (1,)), ((), ())), preferred_element_type=F32)


def _dot_tn(a, b):
    return lax.dot_general(a, b, (((0,), (0,)), ((), ())), preferred_element_type=F32)


def _split_dot(fixed_bf16, x, terms):
    acc = None
    rem = x
    for _ in range(terms):
        piece = rem.astype(BF16)
        rem = rem - piece.astype(F32)
        part = _dot(fixed_bf16, piece)
        acc = part if acc is None else acc + part
    return acc


def _split_dot_right(x, fixed_bf16, terms):
    acc = None
    rem = x
    for _ in range(terms):
        piece = rem.astype(BF16)
        rem = rem - piece.astype(F32)
        part = _dot(piece, fixed_bf16)
        acc = part if acc is None else acc + part
    return acc


def _log_sigmoid(x):
    return jnp.minimum(x, 0.0) - jnp.log(1.0 + jnp.exp(-jnp.abs(x)))


def _sigmoid(x):
    return 1.0 / (1.0 + jnp.exp(-x))


def _silu(x):
    return x * _sigmoid(x)


def _rms(x, eps):
    return x * lax.rsqrt(jnp.mean(x * x, axis=-1, keepdims=True) + eps)


def _tri_incl(n):
    r = lax.broadcasted_iota(jnp.int32, (n, n), 0)
    c = lax.broadcasted_iota(jnp.int32, (n, n), 1)
    return c <= r


def _rmsnorm_kernel(x_ref, g_ref, o_ref):
    o_ref[...] = (_rms(x_ref[...], NORM_EPS) * g_ref[...]).astype(o_ref.dtype)


def _rmsnorm(x, g, out_dtype):
    m, d = x.shape
    tm = min(m, 1024)
    return pl.pallas_call(
        _rmsnorm_kernel,
        grid=(m // tm,),
        in_specs=[pl.BlockSpec((tm, d), lambda i: (i, 0)), pl.BlockSpec((1, d), lambda i: (0, 0))],
        out_specs=pl.BlockSpec((tm, d), lambda i: (i, 0)),
        out_shape=jax.ShapeDtypeStruct((m, d), out_dtype),
        compiler_params=_params(("arbitrary",)),
        name="rmsnorm",
    )(x, g.reshape(1, d))


def _mm_kernel(a_ref, b_ref, *rest, epilogue, n_extra):
    o_ref = rest[n_extra]
    acc = _dot(a_ref[...], b_ref[...])
    if epilogue is not None:
        acc = epilogue(acc, *[e[...] for e in rest[:n_extra]])
    o_ref[...] = acc.astype(o_ref.dtype)


def _matmul(a, b, extras=(), epilogue=None, out_dtype=F32, tm=1024, tn=1024, name="matmul"):
    m, k = a.shape
    n = b.shape[1]
    tm, tn = min(tm, m), min(tn, n)
    assert m % tm == 0 and n % tn == 0, (m, n, tm, tn)
    in_specs = [pl.BlockSpec((tm, k), lambda i, j: (i, 0)), pl.BlockSpec((k, tn), lambda i, j: (0, j))]
    for e in extras:
        if e.shape[0] == 1:
            in_specs.append(pl.BlockSpec((1, tn), lambda i, j: (0, j)))
        else:
            in_specs.append(pl.BlockSpec((tm, tn), lambda i, j: (i, j)))
    return pl.pallas_call(
        functools.partial(_mm_kernel, epilogue=epilogue, n_extra=len(extras)),
        grid=(m // tm, n // tn),
        in_specs=in_specs,
        out_specs=pl.BlockSpec((tm, tn), lambda i, j: (i, j)),
        out_shape=jax.ShapeDtypeStruct((m, n), out_dtype),
        compiler_params=_params(("arbitrary", "arbitrary")),
        name=name,
    )(a, b, *extras)


def _rwkv_pre_kernel(x_ref, st_ref, g_ref, mix_ref, w1_ref, a1_ref, g1_ref, *rest, has_v1):
    if has_v1:
        v1_ref, rest = rest[0], rest[1:]
    xr_ref, xk_ref, xv_ref, lw_ref, la_ref, lg_ref = rest[:6]
    rest = rest[6:]
    if has_v1:
        lv_ref, rest = rest[0], rest[1:]
    hl_ref, carry_ref = rest
    tt = x_ref.shape[0]

    @pl.when(pl.program_id(1) == 0)
    def _():
        carry_ref[...] = st_ref[...]

    h = _rms(x_ref[...], NORM_EPS) * g_ref[...]
    row = lax.broadcasted_iota(jnp.int32, (tt, 1), 0)
    h_prev = jnp.where(row == 0, carry_ref[...], pltpu.roll(h, 1, axis=0))
    carry_ref[...] = h[tt - 1:tt, :]
    hl_ref[...] = h[tt - 8:, :]
    xx = h_prev - h
    mixed = lambda i: (h + xx * mix_ref[i:i + 1, :]).astype(BF16)
    xr_ref[...] = mixed(0)
    xk_ref[...] = mixed(2)
    xv = mixed(3)
    xv_ref[...] = xv
    lw_ref[...] = jnp.tanh(_dot(mixed(1), w1_ref[...])).astype(BF16)
    la_ref[...] = _dot(mixed(4), a1_ref[...]).astype(BF16)
    lg_ref[...] = _sigmoid(_dot(mixed(5), g1_ref[...])).astype(BF16)
    if has_v1:
        lv_ref[...] = _dot(xv, v1_ref[...]).astype(BF16)


def _rwkv_pre(b, x, shift_state, g, mix, w1, a1, g1, v1):
    m, d = x.shape
    t = m // b
    tt = min(t, 512)
    nt = t // tt
    has_v1 = v1 is not None
    lora = [w1, a1, g1] + ([v1] if has_v1 else [])
    flat_spec = lambda n: pl.BlockSpec((tt, n), lambda i, j: (i * nt + j, 0))
    full_spec = lambda w: pl.BlockSpec(w.shape, lambda i, j: (0, 0))
    wide = [jax.ShapeDtypeStruct((b * t, d), BF16)] * 3
    narrow = [jax.ShapeDtypeStruct((b * t, w.shape[1]), BF16) for w in lora]
    outs = pl.pallas_call(
        functools.partial(_rwkv_pre_kernel, has_v1=has_v1),
        grid=(b, nt),
        in_specs=[flat_spec(d),
                  pl.BlockSpec((None, 1, d), lambda i, j: (i, 0, 0)),
                  pl.BlockSpec((1, d), lambda i, j: (0, 0)),
                  pl.BlockSpec((6, d), lambda i, j: (0, 0))] + [full_spec(w) for w in lora],
        out_specs=[flat_spec(d)] * 3 + [flat_spec(w.shape[1]) for w in lora]
                  + [pl.BlockSpec((None, 8, d), lambda i, j: (i, 0, 0))],
        out_shape=wide + narrow + [jax.ShapeDtypeStruct((b, 8, d), F32)],
        scratch_shapes=[pltpu.VMEM((1, d), F32)],
        compiler_params=_params(("arbitrary", "arbitrary")),
        name="rwkv_pre",
    )(x, shift_state.reshape(b, 1, d), g.reshape(1, d), mix, *lora)
    return outs[:-1], outs[-1][:, 7, :]


def _wkv7_kernel(*refs, has_vres):
    refs = list(refs)
    r_ref, k_ref, v_ref, lw_ref, la_ref, lg_ref = refs[:6]
    w2_ref, a2_ref, g2_ref, w0_ref, a0_ref = refs[6:11]
    refs = refs[11:]
    if has_vres:
        vf_ref, lv_ref, v2_ref, v0_ref = refs[:4]
        refs = refs[4:]
    kk_ref, ka_ref, rk_ref, gnw_ref, gnb_ref, s0_ref, y_ref, s_ref = refs
    c_len, d = r_ref.shape
    n_groups = d // MXU_DIM

    @pl.when(pl.program_id(1) == 0)
    def _():
        s_ref[...] = s0_ref[...]

    ri = lax.broadcasted_iota(jnp.int32, (MXU_DIM, MXU_DIM), 0)
    ci = lax.broadcasted_iota(jnp.int32, (MXU_DIM, MXU_DIM), 1)
    head_bits = RWKV_HEAD.bit_length() - 1
    same_head = (ri >> head_bits) == (ci >> head_bits)
    bd = jnp.where(same_head, 1.0, 0.0)
    bd_b = bd.astype(BF16)
    tri_b = jnp.where(_tri_incl(c_len), 1.0, 0.0).astype(BF16)

    assert c_len == RWKV_HEAD and 2 * RWKV_HEAD == LANES
    t_c = lax.broadcasted_iota(jnp.int32, (MXU_DIM, LANES), 0) & (c_len - 1)
    lane_c = lax.broadcasted_iota(jnp.int32, (MXU_DIM, LANES), 1)
    s_c = lane_c & (RWKV_HEAD - 1)
    low_half = lane_c < RWKV_HEAD
    strict_c = s_c < t_c
    incl_c = s_c <= t_c
    eye_c = jnp.where(s_c == t_c, 1.0, 0.0)
    lane_t = lax.broadcasted_iota(jnp.int32, (c_len, LANES), 1)
    half_masks = (lane_t < RWKV_HEAD, lane_t >= RWKV_HEAD)

    def stack(x):
        return jnp.concatenate([x] * HEADS_PER_GROUP, axis=0)

    def stack_masked(x):
        return stack(x) * bd

    def head_sum(x):
        return _split_dot_right(x, bd_b, 1)

    def expand(xc, half):
        zeros = jnp.zeros((c_len, LANES), F32)
        blocks = []
        for hd in range(HEADS_PER_GROUP):
            src = xc[hd * c_len:(hd + 1) * c_len]
            if hd % 2 != half:
                src = pltpu.roll(src, RWKV_HEAD, axis=1)
            tile = jnp.where(half_masks[hd % 2], src, 0.0)
            blocks.append(jnp.concatenate(
                [tile if lt == hd // 2 else zeros for lt in range(MXU_DIM // LANES)], axis=1))
        return jnp.concatenate(blocks, axis=0)

    groups = range(n_groups)
    sls = [slice(grp * MXU_DIM, (grp + 1) * MXU_DIM) for grp in groups]
    r = [r_ref[:, sl] for sl in sls]
    k = [k_ref[:, sl] for sl in sls]
    v = [v_ref[:, sl] for sl in sls]
    lw_n, la_n, lg_n = lw_ref[...], la_ref[...], lg_ref[...]
    w = [_log_sigmoid(w0_ref[:, sl] + _dot(lw_n, w2_ref[:, sl])) - 0.5 for sl in sls]
    a = [_sigmoid(a0_ref[:, sl] + _dot(la_n, a2_ref[:, sl])) for sl in sls]
    g = [_dot(lg_n, g2_ref[:, sl]) for sl in sls]
    if has_vres:
        lv_n = lv_ref[...]
        mix_v = [_sigmoid(v0_ref[:, sl] + _dot(lv_n, v2_ref[:, sl])) for sl in sls]
        v = [v[i] + (vf_ref[:, sls[i]] - v[i]) * mix_v[i] for i in groups]
    lw = [-jnp.exp(x) for x in w]
    kk = [k[i] * kk_ref[:, sls[i]] for i in groups]
    kk_ss = [head_sum(x * x) for x in kk]
    cum = [_split_dot(tri_b, x, 2) for x in lw]
    kk = [kk[i] / jnp.maximum(jnp.sqrt(kk_ss[i]), 1e-12) for i in groups]
    kmod = [k[i] * (1.0 + (a[i] - 1.0) * ka_ref[:, sls[i]]) for i in groups]
    kka = [kk[i] * a[i] for i in groups]
    last = [x[c_len - 1:c_len, :] for x in cum]
    e_neg = [jnp.exp(-x) for x in cum]
    ar = [jnp.concatenate([stack_masked(-kk[i] * jnp.exp(cum[i] - lw[i])),
                           stack_masked(r[i] * jnp.exp(cum[i]))], axis=0).astype(BF16) for i in groups]
    bk = [jnp.concatenate([kka[i] * e_neg[i], kmod[i] * e_neg[i]], axis=0).astype(BF16) for i in groups]
    z = [_dot_nt(ar[i], bk[i]) for i in groups]
    za = [jnp.where(strict_c, x[:MXU_DIM], 0.0) for x in z]
    zr = [jnp.where(incl_c, x[MXU_DIM:], 0.0) for x in z]
    a_ak = [expand(x, 1).astype(BF16) for x in za]
    a_r = [jnp.concatenate([expand(x, 0), expand(x, 1)], axis=1).astype(BF16) for x in zr]

    rhs = [jnp.where(low_half, x, eye_c) for x in za]
    p_bd = [expand(x, 0).astype(BF16) for x in za]
    n_steps = (c_len - 1).bit_length()
    for step in range(n_steps):
        x2 = [_dot(p_bd[i], rhs[i].astype(BF16)) for i in groups]
        rhs = [x2[i] + jnp.where(low_half, 0.0, rhs[i]) for i in groups]
        if step + 1 < n_steps:
            p_bd = [expand(x, 0).astype(BF16) for x in x2]
    tinv = [expand(x, 1) for x in rhs]

    v_st = [stack_masked(x) for x in v]
    s = [s_ref[i] for i in groups]
    y = [_dot_nt(ar[i], s[i].astype(BF16)) for i in groups]
    av = [_dot(a_ak[i], v_st[i].astype(BF16)) for i in groups]
    u = [_dot(tinv[i].astype(BF16), (y[i][:MXU_DIM] + av[i]).astype(BF16)) for i in groups]
    uv = [jnp.concatenate([u[i], v_st[i]], axis=0).astype(BF16) for i in groups]
    o_st = [y[i][MXU_DIM:] + _dot(a_r[i], uv[i]) for i in groups]
    e_last = [jnp.exp(last[i] - cum[i]) for i in groups]
    bk_hat = [jnp.concatenate([stack_masked(kka[i] * e_last[i]), stack_masked(kmod[i] * e_last[i])],
                              axis=0).astype(BF16) for i in groups]
    for i in groups:
        s_ref[i] = s[i] * jnp.exp(last[i]) + _dot_tn(uv[i], bk_hat[i])

    o = [functools.reduce(lambda p, q: p + q,
                          [x[hd * c_len:(hd + 1) * c_len] for hd in range(HEADS_PER_GROUP)]) for x in o_st]
    inv_n = 1.0 / RWKV_HEAD
    mu = [head_sum(x) * inv_n for x in o]
    bonus_dot = [head_sum(r[i] * kmod[i] * rk_ref[:, sls[i]]) for i in groups]
    dev = [o[i] - mu[i] for i in groups]
    var = [head_sum(x * x) * inv_n for x in dev]
    for i in groups:
        sl = sls[i]
        o_n = dev[i] * lax.rsqrt(var[i] + RWKV_GN_EPS) * gnw_ref[:, sl] + gnb_ref[:, sl]
        y_ref[:, sl] = ((o_n + bonus_dot[i] * v[i]) * g[i]).astype(y_ref.dtype)


def _wkv7(b, r, k, v, lora, lora_w, lora_b, vres, k_k, k_a, r_k, lnx_w, lnx_b, s0_bd):
    m, d = r.shape
    t = m // b
    n_chunks = t // CHUNK
    n_groups = d // MXU_DIM
    rows = lambda n: pl.BlockSpec((CHUNK, n), lambda i, j: (i * n_chunks + j, 0))
    whole = lambda w: pl.BlockSpec(w.shape, lambda i, j: (0, 0))
    vec_spec = pl.BlockSpec((1, d), lambda i, j: (0, 0))
    st_spec = pl.BlockSpec((None, n_groups, MXU_DIM, MXU_DIM), lambda i, j: (i, 0, 0, 0))
    vec = lambda p: p.reshape(1, d)
    args = [r, k, v, *lora, *lora_w, *[vec(x) for x in lora_b]]
    specs = ([rows(d)] * 3 + [rows(x.shape[1]) for x in lora] + [whole(w) for w in lora_w]
             + [vec_spec] * len(lora_b))
    if vres is not None:
        v_first, lv, v2, v0 = vres
        args += [v_first, lv, v2, vec(v0)]
        specs += [rows(d), rows(lv.shape[1]), whole(v2), vec_spec]
    args += [vec(k_k), vec(k_a), vec(r_k), vec(lnx_w), vec(lnx_b), s0_bd]
    specs += [vec_spec] * 5 + [st_spec]
    return pl.pallas_call(
        functools.partial(_wkv7_kernel, has_vres=vres is not None),
        grid=(b, n_chunks),
        in_specs=specs,
        out_specs=[rows(d), st_spec],
        out_shape=[jax.ShapeDtypeStruct((m, d), BF16),
                   jax.ShapeDtypeStruct(s0_bd.shape, F32)],
        compiler_params=_params(("arbitrary", "arbitrary")),
        name="wkv7",
    )(*args)


def _wkv_state_to_blockdiag(s):
    b, h, n, _ = s.shape
    g = h // HEADS_PER_GROUP
    s = s.reshape(b, g, HEADS_PER_GROUP, n, n)
    eye = jnp.eye(HEADS_PER_GROUP, dtype=s.dtype)
    return jnp.einsum("bghvk,hj->bghvjk", s, eye).reshape(b, g, MXU_DIM, MXU_DIM)


def _wkv_state_from_blockdiag(s_bd):
    b, g = s_bd.shape[:2]
    n = RWKV_HEAD
    s = s_bd.reshape(b, g, HEADS_PER_GROUP, n, HEADS_PER_GROUP, n)
    s = jnp.stack([s[:, :, i, :, i, :] for i in range(HEADS_PER_GROUP)], axis=2)
    return s.reshape(b, g * HEADS_PER_GROUP, n, n)


def _gla_kernel(q_ref, k_ref, v_ref, gate_ref, lr_ref, w2_ref, gb_ref, hn_ref, s0_ref, y_ref, s_ref):
    tc = q_ref.shape[0]
    n_heads, dv, dk = s_ref.shape
    heads = range(n_heads)
    ksl = [slice(h * dk, (h + 1) * dk) for h in heads]
    vsl = [slice(h * dv, (h + 1) * dv) for h in heads]

    @pl.when(pl.program_id(1) == 0)
    def _():
        s_ref[...] = s0_ref[...]

    tri = _tri_incl(CHUNK)
    tri_b = jnp.where(tri, 1.0, 0.0).astype(BF16)
    scale = dk ** -0.5
    for c in range(tc // CHUNK):
        rows = slice(c * CHUNK, (c + 1) * CHUNK)
        lr = lr_ref[rows, :].astype(BF16)
        gk = [_log_sigmoid(_dot(lr, w2_ref[:, sl]) + gb_ref[:, sl]) * (1.0 / GLA_GATE_NORMALIZER)
              for sl in ksl]
        cum = [_split_dot(tri_b, x, 3) for x in gk]
        last = [x[CHUNK - 1:CHUNK, :] for x in cum]
        qe = [(q_ref[rows, ksl[h]] * scale * jnp.exp(cum[h])).astype(BF16) for h in heads]
        ke = [(k_ref[rows, ksl[h]] * jnp.exp(-cum[h])).astype(BF16) for h in heads]
        k2 = [(k_ref[rows, ksl[h]] * jnp.exp(last[h] - cum[h])).astype(BF16) for h in heads]
        vb = [v_ref[rows, sl].astype(BF16) for sl in vsl]
        scores = [jnp.where(tri, _dot_nt(qe[h], ke[h]), 0.0).astype(BF16) for h in heads]
        s_t = [s_ref[h] for h in heads]
        o = [_dot(scores[h], vb[h]) + _dot_nt(qe[h], s_t[h].astype(BF16)) for h in heads]
        for h in heads:
            s_ref[h] = s_t[h] * jnp.exp(last[h]) + _dot_tn(vb[h], k2[h])
        for h in heads:
            o_n = _rms(o[h], GLA_NORM_EPS) * hn_ref[...]
            y_ref[rows, vsl[h]] = (o_n * _silu(gate_ref[rows, vsl[h]])).astype(y_ref.dtype)


def _gla(b, qkvg, lr, gk_w2, gk_b, head_norm, s0_t):
    t = qkvg.shape[0] // b
    h, dv, dk = s0_t.shape[1:]
    tc = min(t, 2 * CHUNK)
    dk_all, dv_all = h * dk, h * dv
    assert dv_all == 2 * dk_all
    st_spec = pl.BlockSpec((None, h, dv, dk), lambda i, c: (i, 0, 0, 0))
    n_steps = t // tc
    rows = lambda n, col: pl.BlockSpec((tc, n), lambda i, c: (i * n_steps + c, col))
    return pl.pallas_call(
        _gla_kernel,
        grid=(b, t // tc),
        in_specs=[rows(dk_all, 0), rows(dk_all, 1), rows(dv_all, 1), rows(dv_all, 2), rows(LANES, 0),
                  pl.BlockSpec((LANES, dk_all), lambda i, c: (0, 0)),
                  pl.BlockSpec((1, dk_all), lambda i, c: (0, 0)),
                  pl.BlockSpec((1, dv), lambda i, c: (0, 0)),
                  st_spec],
        out_specs=[rows(dv_all, 0), st_spec],
        out_shape=[jax.ShapeDtypeStruct((b * t, dv_all), BF16),
                   jax.ShapeDtypeStruct(s0_t.shape, F32)],
        compiler_params=_params(("arbitrary", "arbitrary")),
        name="gla",
    )(qkvg, qkvg, qkvg, qkvg, lr, gk_w2, gk_b.reshape(1, -1), head_norm.reshape(1, dv), s0_t)


def _ffn_kernel(x_ref, g_ref, sv_ref, sg_ref, wv_ref, wg_ref, cwv_ref, cwg_ref, cbv_ref, cbg_ref,
                wd_ref, og_ref, o_ref, nsv_ref, nsg_ref, h_ref, acc_ref, slab_ref, ua_ref, ub_ref, carry_ref,
                *, nj, nseq, seq_len, norm_output):
    rows, d = x_ref.shape
    tn = wv_ref.shape[1]
    j = pl.program_id(2)
    jt = jnp.maximum(j - 1, 0)

    n_ph = min(8, seq_len // 16)
    q_len = seq_len // n_ph
    phase_rows = lambda p: pl.ds(p, nseq * q_len, stride=n_ph)
    blk = lambda p: slice(p * q_len, (p + 1) * q_len)
    slabs = [slice(c * LANES, (c + 1) * LANES) for c in range(d // LANES)]

    @pl.when(j == 0)
    def _():
        for c, cl in enumerate(slabs):
            slab_ref[c] = x_ref[:, cl]
        for c, cl in enumerate(slabs):
            for p in range(n_ph):
                acc_ref[:, blk(p), cl] = slab_ref[c, phase_rows(p), :].reshape(nseq, q_len, LANES)
        h_ref[...] = (_rms(acc_ref[...], NORM_EPS) * g_ref[...]).astype(BF16)

    @pl.when((pl.program_id(1) == 0) & (j > 0))
    def _():
        carry_ref[2 * jt] = sv_ref[...]
        carry_ref[2 * jt + 1] = sg_ref[...]

    cols = [slice(c * MXU_DIM, (c + 1) * MXU_DIM) for c in range(tn // MXU_DIM)]
    first = lax.broadcasted_iota(jnp.int32, (1, q_len, 1), 1) == 0

    def conv(u, cs, cw_ref, cb_ref, slot, ns_ref):
        u = u.reshape(nseq, seq_len, MXU_DIM)
        w0, w1, w2, cb = cw_ref[0:1, cs], cw_ref[1:2, cs], cw_ref[2:3, cs], cb_ref[:, cs]
        p2, p1 = carry_ref[slot, :, 0:1, cs], carry_ref[slot, :, 1:2, cs]
        back1 = jnp.where(first, p1, pltpu.roll(u[:, blk(n_ph - 1), :], 1, axis=1))
        back2 = jnp.where(first, p2, pltpu.roll(u[:, blk(n_ph - 2), :], 1, axis=1))
        r1 = jnp.concatenate([back1, u[:, :(n_ph - 1) * q_len, :]], axis=1)
        r2 = jnp.concatenate([back2, back1, u[:, :(n_ph - 2) * q_len, :]], axis=1)
        last = seq_len - 1
        before_last = (n_ph - 1) * q_len - 1
        for row, src in ((0, before_last), (1, last)):
            carry_ref[slot, :, row:row + 1, cs] = u[:, src:src + 1, :]
            ns_ref[:, row:row + 1, cs] = u[:, src:src + 1, :]
        return cb + w0 * r2 + w1 * r1 + w2 * u

    def step(up_ref, dn_ref):
        h = None if up_ref is None else h_ref[...].reshape(rows, d)
        acc = None
        for cs in cols:
            if up_ref is not None:
                up_ref[0, :, cs] = _dot(h, wv_ref[:, cs])
            if dn_ref is not None:
                val = conv(dn_ref[0, :, cs], cs, cwv_ref, cbv_ref, 2 * jt, nsv_ref)
            if up_ref is not None:
                up_ref[1, :, cs] = _dot(h, wg_ref[:, cs])
            if dn_ref is not None:
                gate = conv(dn_ref[1, :, cs], cs, cwg_ref, cbg_ref, 2 * jt + 1, nsg_ref)
                act = (_silu(gate) * val).reshape(rows, MXU_DIM).astype(BF16)
                part = _dot(act, wd_ref[cs, :])
                acc = part if acc is None else acc + part
        if dn_ref is not None:
            acc_ref[...] += acc.reshape(nseq, seq_len, d)

    even = (j & 1) == 0
    inner = (j > 0) & (j < nj)

    @pl.when(j == 0)
    def _():
        step(ua_ref, None)

    @pl.when(inner & even)
    def _():
        step(ua_ref, ub_ref)

    @pl.when(inner & jnp.logical_not(even))
    def _():
        step(ub_ref, ua_ref)

    @pl.when(j == nj)
    def _():
        step(None, ua_ref if (nj - 1) % 2 == 0 else ub_ref)
        if norm_output:
            acc_ref[...] = _rms(acc_ref[...], NORM_EPS) * og_ref[...]
        for c, cl in enumerate(slabs):
            for p in range(n_ph):
                slab_ref[c, phase_rows(p), :] = acc_ref[:, blk(p), cl].reshape(nseq * q_len, LANES)
        for c, cl in enumerate(slabs):
            o_ref[:, cl] = slab_ref[c]


FFN_TILE_COLS = 512


def _conv_ffn(b, x, conv_state, g, w_up, conv_w, conv_b, w_down, out_g, norm_output, nseq, seq_len):
    m, d = x.shape
    t = m // b
    f = w_down.shape[0]
    tn = FFN_TILE_COLS
    nj = f // tn
    assert b % nseq == 0 and t % seq_len == 0 and f % tn == 0
    assert nseq == 1 or seq_len == t
    x_spec = pl.BlockSpec((nseq * seq_len, d), lambda i, s, j: (i * (t // seq_len) + s, 0))
    up_tile = lambda j: jnp.minimum(j, nj - 1)
    dn_tile = lambda j: jnp.maximum(j - 1, 0)
    st_v = pl.BlockSpec((nseq, 2, tn), lambda i, s, j: (i, 0, dn_tile(j)))
    st_g = pl.BlockSpec((nseq, 2, tn), lambda i, s, j: (i, 0, nj + dn_tile(j)))
    n_s = t // seq_len
    tail_spec = pl.BlockSpec((nseq, None, 2, tn), lambda i, s, j: (i, s, 0, dn_tile(j)))
    col_v = lambda n: pl.BlockSpec((n, tn), lambda i, s, j: (0, dn_tile(j)))
    col_g = lambda n: pl.BlockSpec((n, tn), lambda i, s, j: (0, nj + dn_tile(j)))
    y, ns_v, ns_g = pl.pallas_call(
        functools.partial(_ffn_kernel, nj=nj, nseq=nseq, seq_len=seq_len, norm_output=norm_output),
        grid=(b // nseq, n_s, nj + 1),
        in_specs=[x_spec, pl.BlockSpec((1, d), lambda i, s, j: (0, 0)), st_v, st_g,
                  pl.BlockSpec((d, tn), lambda i, s, j: (0, up_tile(j))),
                  pl.BlockSpec((d, tn), lambda i, s, j: (0, nj + up_tile(j))),
                  col_v(3), col_g(3), col_v(1), col_g(1),
                  pl.BlockSpec((tn, d), lambda i, s, j: (dn_tile(j), 0)),
                  pl.BlockSpec((1, d), lambda i, s, j: (0, 0))],
        out_specs=[x_spec, tail_spec, tail_spec],
        out_shape=[jax.ShapeDtypeStruct((m, d), F32),
                   jax.ShapeDtypeStruct((b, n_s, 2, f), F32),
                   jax.ShapeDtypeStruct((b, n_s, 2, f), F32)],
        scratch_shapes=[pltpu.VMEM((nseq, seq_len, d), BF16),
                        pltpu.VMEM((nseq, seq_len, d), F32),
                        pltpu.VMEM((d // LANES, nseq * seq_len, LANES), F32),
                        pltpu.VMEM((2, nseq * seq_len, tn), F32),
                        pltpu.VMEM((2, nseq * seq_len, tn), F32),
                        pltpu.VMEM((2 * nj, nseq, 2, tn), F32)],
        compiler_params=_params(("arbitrary", "arbitrary", "arbitrary")),
        name="conv_ffn",
    )(x, g.reshape(1, d), conv_state, conv_state, w_up, w_up, conv_w, conv_w,
      conv_b.reshape(1, -1), conv_b.reshape(1, -1), w_down, out_g.reshape(1, d))
    return y, jnp.concatenate([ns_v[:, -1], ns_g[:, -1]], axis=-1)


def _pad_cols(w, n):
    return jnp.pad(w, ((0, 0), (0, n - w.shape[1])))


def _pad_rows(w, n):
    return jnp.pad(w, ((0, n - w.shape[0]), (0, 0)))


def _rwkv_layer(b, x, shift_st, wkv_st, v_first, p):
    pre, new_shift = _rwkv_pre(b, x, shift_st, p["norm"], p["mix"], p["w1"], p["a1"], p["g1"], p["v1"])
    xr, xk, xv = pre[:3]
    r = _matmul(xr, p["wr"], name="rwkv_r")
    k = _matmul(xk, p["wk"], name="rwkv_k")
    v = _matmul(xv, p["wv"], name="rwkv_v")
    if p["v1"] is None:
        vres, v_first = None, v
    else:
        vres = (v_first, pre[6], p["v2"], p["v0"])
    y, s_bd = _wkv7(b, r, k, v, pre[3:6], (p["w2"], p["a2"], p["g2"]), (p["w0"], p["a0"]), vres,
                    p["k_k"], p["k_a"], p["r_k"], p["lnx_w"], p["lnx_b"],
                    _wkv_state_to_blockdiag(wkv_st))
    x_new = _matmul(y, p["wo"], extras=(x,), epilogue=lambda acc, res: res + acc, name="rwkv_o")
    return x_new, new_shift, _wkv_state_from_blockdiag(s_bd), v_first


def _gla_layer(b, x, gla_st, p):
    h = _rmsnorm(x, p["norm"], BF16)
    qkvg = _matmul(h, p["w_main"], name="gla_in")
    lr = _matmul(h, p["w_lr"], name="gla_lr")
    y, s_t = _gla(b, qkvg, lr, p["gk_w2"], p["gk_b"], p["head_norm"], jnp.swapaxes(gla_st, -1, -2))
    x_new = _matmul(y, p["wo"], extras=(x,), epilogue=lambda acc, res: res + acc, name="gla_o")
    return x_new, jnp.swapaxes(s_t, -1, -2)


FFN_TILE_ROWS = 512


def _ffn_tile(b, t):
    if t >= FFN_TILE_ROWS:
        return 1, FFN_TILE_ROWS
    return min(b, FFN_TILE_ROWS // t), t


def _run_trunk(x, shift_st, wkv_st, gla_st, conv_st, layers, norm_final, ffn_nseq, ffn_len):
    new_shift, new_wkv, new_gla, new_conv = [], [], [], []
    v_first = None
    b, t, d = x.shape
    x = x.reshape(b * t, d)
    for i, (mixer, ffn) in enumerate(layers):
        j = i // 2
        if i % 2 == 0:
            x, s_shift, s_wkv, v_first = _rwkv_layer(b, x, shift_st[j], wkv_st[j], v_first, mixer)
            new_shift.append(s_shift)
            new_wkv.append(s_wkv)
        else:
            x, s_gla = _gla_layer(b, x, gla_st[j], mixer)
            new_gla.append(s_gla)
        x, s_conv = _conv_ffn(b, x, conv_st[i], ffn["norm"], ffn["w_up"], ffn["conv_w"], ffn["conv_b"],
                              ffn["w_down"], norm_final, i == len(layers) - 1, ffn_nseq, ffn_len)
        new_conv.append(s_conv)
    return x.reshape(b, t, d), jnp.stack(new_shift), jnp.stack(new_wkv), jnp.stack(new_gla), jnp.stack(new_conv)


def kernel(x_prompt, x_sample, state_rwkv_shift, state_rwkv_wkv, state_gla, state_ffn_conv, norm_mix, norm_ffn, norm_final, rwkv_mix, rwkv_w0, rwkv_w1, rwkv_w2, rwkv_a0, rwkv_a1, rwkv_a2, rwkv_v0, rwkv_v1, rwkv_v2, rwkv_g1, rwkv_g2, rwkv_k_k, rwkv_k_a, rwkv_r_k, rwkv_wr, rwkv_wk, rwkv_wv, rwkv_wo, rwkv_lnx_w, rwkv_lnx_b, gla_w_in, gla_gk_w2, gla_gk_b, gla_head_norm, gla_wo, ffn_w_up, ffn_conv_w, ffn_conv_b, ffn_w_down):
    depth = norm_mix.shape[0]
    d = x_prompt.shape[-1]
    bf = lambda w: w.astype(BF16)
    dk_total = gla_gk_w2.shape[-1]
    n_main = gla_w_in.shape[-1] - GLA_GATE_RANK
    layers = []
    for i in range(depth):
        j = i // 2
        if i % 2 == 0:
            has_vres = j > 0
            mixer = dict(
                norm=norm_mix[i], mix=rwkv_mix[j],
                wr=bf(rwkv_wr[j]), wk=bf(rwkv_wk[j]), wv=bf(rwkv_wv[j]), wo=bf(rwkv_wo[j]),
                w0=rwkv_w0[j], w1=bf(_pad_cols(rwkv_w1[j], LANES)), w2=bf(_pad_rows(rwkv_w2[j], LANES)),
                a0=rwkv_a0[j], a1=bf(_pad_cols(rwkv_a1[j], LANES)), a2=bf(_pad_rows(rwkv_a2[j], LANES)),
                g1=bf(rwkv_g1[j]), g2=bf(rwkv_g2[j]),
                v0=rwkv_v0[j - 1] if has_vres else None,
                v1=bf(_pad_cols(rwkv_v1[j - 1], LANES)) if has_vres else None,
                v2=bf(_pad_rows(rwkv_v2[j - 1], LANES)) if has_vres else None,
                k_k=rwkv_k_k[j], k_a=rwkv_k_a[j], r_k=rwkv_r_k[j].reshape(d),
                lnx_w=rwkv_lnx_w[j], lnx_b=rwkv_lnx_b[j])
        else:
            mixer = dict(
                norm=norm_mix[i],
                w_main=bf(gla_w_in[j][:, :n_main]),
                w_lr=bf(_pad_cols(gla_w_in[j][:, n_main:], LANES)),
                gk_w2=bf(_pad_rows(gla_gk_w2[j], LANES)), gk_b=gla_gk_b[j],
                head_norm=gla_head_norm[j], wo=bf(gla_wo[j]))
        ffn = dict(norm=norm_ffn[i], w_up=bf(ffn_w_up[i]), conv_w=ffn_conv_w[i],
                   conv_b=ffn_conv_b[i],
                   w_down=bf(ffn_w_down[i]))
        layers.append((mixer, ffn))

    bp, tp, _ = x_prompt.shape
    bs, ts, _ = x_sample.shape
    zeros_like_state = lambda s: jnp.zeros((s.shape[0], bp) + s.shape[2:], s.dtype)
    out_p = _run_trunk(x_prompt, zeros_like_state(state_rwkv_shift), zeros_like_state(state_rwkv_wkv),
                       zeros_like_state(state_gla), zeros_like_state(state_ffn_conv), layers,
                       norm_final, *_ffn_tile(bp, tp))
    out_s = _run_trunk(x_sample, state_rwkv_shift, state_rwkv_wkv, state_gla, state_ffn_conv, layers,
                       norm_final, *_ffn_tile(bs, ts))
    return (out_p[0], out_s[0]) + tuple(out_p[1:]) + tuple(out_s[1:])
```

```python
import functools

import jax
import jax.numpy as jnp
from jax import lax
from jax.experimental import pallas as pl
from jax.experimental.pallas import tpu as pltpu

F32 = jnp.float32
BF16 = jnp.bfloat16

CHUNK = 64
RWKV_HEAD = 64
RWKV_GN_EPS = 64e-5
GLA_HEADS = 4
GLA_GATE_RANK = 16
GLA_GATE_NORMALIZER = 16.0
GLA_NORM_EPS = 1e-5
NORM_EPS = 1e-6

LANES = 128
MXU_DIM = 256
HEADS_PER_GROUP = MXU_DIM // RWKV_HEAD
VMEM_LIMIT = 56 * 1024 * 1024


def _params(sem):
    return pltpu.CompilerParams(dimension_semantics=sem, vmem_limit_bytes=VMEM_LIMIT)


def _dot(a, b):
    return jnp.dot(a, b, preferred_element_type=F32)


def _dot_nt(a, b):
    return lax.dot_general(a, b, (((1,), (1,)), ((), ())), preferred_element_type=F32)


def _dot_tn(a, b):
    return lax.dot_general(a, b, (((0,), (0,)), ((), ())), preferred_element_type=F32)


def _split_dot(fixed_bf16, x, terms):
    acc = None
    rem = x
    for _ in range(terms):
        piece = rem.astype(BF16)
        rem = rem - piece.astype(F32)
        part = _dot(fixed_bf16, piece)
        acc = part if acc is None else acc + part
    return acc


def _split_dot_right(x, fixed_bf16, terms):
    acc = None
    rem = x
    for _ in range(terms):
        piece = rem.astype(BF16)
        rem = rem - piece.astype(F32)
        part = _dot(piece, fixed_bf16)
        acc = part if acc is None else acc + part
    return acc


def _log_sigmoid(x):
    return jnp.minimum(x, 0.0) - jnp.log(1.0 + jnp.exp(-jnp.abs(x)))


def _sigmoid(x):
    return 1.0 / (1.0 + jnp.exp(-x))


def _silu(x):
    return x * _sigmoid(x)


def _rms(x, eps):
    return x * lax.rsqrt(jnp.mean(x * x, axis=-1, keepdims=True) + eps)


def _tri_incl(n):
    r = lax.broadcasted_iota(jnp.int32, (n, n), 0)
    c = lax.broadcasted_iota(jnp.int32, (n, n), 1)
    return c <= r


def _rmsnorm_kernel(x_ref, g_ref, o_ref):
    o_ref[...] = (_rms(x_ref[...], NORM_EPS) * g_ref[...]).astype(o_ref.dtype)


def _rmsnorm(x, g, out_dtype):
    m, d = x.shape
    tm = min(m, 1024)
    return pl.pallas_call(
        _rmsnorm_kernel,
        grid=(m // tm,),
        in_specs=[pl.BlockSpec((tm, d), lambda i: (i, 0)), pl.BlockSpec((1, d), lambda i: (0, 0))],
        out_specs=pl.BlockSpec((tm, d), lambda i: (i, 0)),
        out_shape=jax.ShapeDtypeStruct((m, d), out_dtype),
        compiler_params=_params(("arbitrary",)),
        name="rmsnorm",
    )(x, g.reshape(1, d))


def _mm_kernel(a_ref, b_ref, *rest, epilogue, n_extra):
    o_ref = rest[n_extra]
    acc = _dot(a_ref[...], b_ref[...])
    if epilogue is not None:
        acc = epilogue(acc, *[e[...] for e in rest[:n_extra]])
    o_ref[...] = acc.astype(o_ref.dtype)


def _matmul(a, b, extras=(), epilogue=None, out_dtype=F32, tm=1024, tn=1024, n_cols=None,
            name="matmul"):
    m, k = a.shape
    n = b.shape[1] if n_cols is None else n_cols
    tm, tn = min(tm, m), min(tn, n)
    assert m % tm == 0 and n % tn == 0, (m, n, tm, tn)
    in_specs = [pl.BlockSpec((tm, k), lambda i, j: (i, 0)), pl.BlockSpec((k, tn), lambda i, j: (0, j))]
    for e in extras:
        if e.shape[0] == 1:
            in_specs.append(pl.BlockSpec((1, tn), lambda i, j: (0, j)))
        else:
            in_specs.append(pl.BlockSpec((tm, tn), lambda i, j: (i, j)))
    return pl.pallas_call(
        functools.partial(_mm_kernel, epilogue=epilogue, n_extra=len(extras)),
        grid=(m // tm, n // tn),
        in_specs=in_specs,
        out_specs=pl.BlockSpec((tm, tn), lambda i, j: (i, j)),
        out_shape=jax.ShapeDtypeStruct((m, n), out_dtype),
        compiler_params=_params(("arbitrary", "arbitrary")),
        name=name,
    )(a, b, *extras)


def _rwkv_pre_kernel(x_ref, st_ref, g_ref, mix_ref, w1_ref, a1_ref, g1_ref, *rest, has_v1):
    if has_v1:
        v1_ref, rest = rest[0], rest[1:]
    xr_ref, xk_ref, xv_ref, lw_ref, la_ref, lg_ref = rest[:6]
    rest = rest[6:]
    if has_v1:
        lv_ref, rest = rest[0], rest[1:]
    hl_ref, carry_ref = rest
    tt = x_ref.shape[0]

    @pl.when(pl.program_id(1) == 0)
    def _():
        carry_ref[...] = st_ref[...]

    h = _rms(x_ref[...], NORM_EPS) * g_ref[...]
    row = lax.broadcasted_iota(jnp.int32, (tt, 1), 0)
    h_prev = jnp.where(row == 0, carry_ref[...], pltpu.roll(h, 1, axis=0))
    carry_ref[...] = h[tt - 1:tt, :]
    hl_ref[...] = h[tt - 8:, :]
    xx = h_prev - h
    mixed = lambda i: (h + xx * mix_ref[i:i + 1, :]).astype(BF16)
    xr_ref[...] = mixed(0)
    xk_ref[...] = mixed(2)
    xv = mixed(3)
    xv_ref[...] = xv
    lw_ref[...] = jnp.tanh(_dot(mixed(1), w1_ref[...])).astype(BF16)
    la_ref[...] = _dot(mixed(4), a1_ref[...]).astype(BF16)
    lg_ref[...] = _sigmoid(_dot(mixed(5), g1_ref[...])).astype(BF16)
    if has_v1:
        lv_ref[...] = _dot(xv, v1_ref[...]).astype(BF16)


def _rwkv_pre(b, x, shift_state, g, mix, w1, a1, g1, v1):
    m, d = x.shape
    t = m // b
    tt = min(t, 512)
    nt = t // tt
    has_v1 = v1 is not None
    lora = [w1, a1, g1] + ([v1] if has_v1 else [])
    flat_spec = lambda n: pl.BlockSpec((tt, n), lambda i, j: (i * nt + j, 0))
    full_spec = lambda w: pl.BlockSpec(w.shape, lambda i, j: (0, 0))
    wide = [jax.ShapeDtypeStruct((b * t, d), BF16)] * 3
    narrow = [jax.ShapeDtypeStruct((b * t, w.shape[1]), BF16) for w in lora]
    outs = pl.pallas_call(
        functools.partial(_rwkv_pre_kernel, has_v1=has_v1),
        grid=(b, nt),
        in_specs=[flat_spec(d),
                  pl.BlockSpec((None, 1, d), lambda i, j: (i, 0, 0)),
                  pl.BlockSpec((1, d), lambda i, j: (0, 0)),
                  pl.BlockSpec((6, d), lambda i, j: (0, 0))] + [full_spec(w) for w in lora],
        out_specs=[flat_spec(d)] * 3 + [flat_spec(w.shape[1]) for w in lora]
                  + [pl.BlockSpec((None, 8, d), lambda i, j: (i, 0, 0))],
        out_shape=wide + narrow + [jax.ShapeDtypeStruct((b, 8, d), F32)],
        scratch_shapes=[pltpu.VMEM((1, d), F32)],
        compiler_params=_params(("arbitrary", "arbitrary")),
        name="rwkv_pre",
    )(x, shift_state.reshape(b, 1, d), g.reshape(1, d), mix, *lora)
    return outs[:-1], outs[-1][:, 7, :]


def _wkv7_kernel(*refs, has_vres):
    refs = list(refs)
    r_ref, k_ref, v_ref, lw_ref, la_ref, lg_ref = refs[:6]
    w2_ref, a2_ref, g2_ref, w0_ref, a0_ref = refs[6:11]
    refs = refs[11:]
    if has_vres:
        vf_ref, lv_ref, v2_ref, v0_ref = refs[:4]
        refs = refs[4:]
    kk_ref, ka_ref, rk_ref, gnw_ref, gnb_ref, s0_ref, y_ref, s_ref = refs
    c_len, d = r_ref.shape
    n_groups = d // MXU_DIM

    @pl.when(pl.program_id(1) == 0)
    def _():
        s_ref[...] = s0_ref[...]

    ri = lax.broadcasted_iota(jnp.int32, (MXU_DIM, MXU_DIM), 0)
    ci = lax.broadcasted_iota(jnp.int32, (MXU_DIM, MXU_DIM), 1)
    head_bits = RWKV_HEAD.bit_length() - 1
    same_head = (ri >> head_bits) == (ci >> head_bits)
    bd = jnp.where(same_head, 1.0, 0.0)
    bd_b = bd.astype(BF16)
    tri_b = jnp.where(_tri_incl(c_len), 1.0, 0.0).astype(BF16)

    assert c_len == RWKV_HEAD and 2 * RWKV_HEAD == LANES
    t_c = lax.broadcasted_iota(jnp.int32, (MXU_DIM, LANES), 0) & (c_len - 1)
    lane_c = lax.broadcasted_iota(jnp.int32, (MXU_DIM, LANES), 1)
    s_c = lane_c & (RWKV_HEAD - 1)
    low_half = lane_c < RWKV_HEAD
    strict_c = s_c < t_c
    incl_c = s_c <= t_c
    eye_c = jnp.where(s_c == t_c, 1.0, 0.0)
    lane_t = lax.broadcasted_iota(jnp.int32, (c_len, LANES), 1)
    half_masks = (lane_t < RWKV_HEAD, lane_t >= RWKV_HEAD)

    def stack(x):
        return jnp.concatenate([x] * HEADS_PER_GROUP, axis=0)

    def stack_masked(x):
        return stack(x) * bd

    def head_sum(x):
        return _split_dot_right(x, bd_b, 1)

    def expand(xc, half):
        zeros = jnp.zeros((c_len, LANES), F32)
        blocks = []
        for hd in range(HEADS_PER_GROUP):
            src = xc[hd * c_len:(hd + 1) * c_len]
            if hd % 2 != half:
                src = pltpu.roll(src, RWKV_HEAD, axis=1)
            tile = jnp.where(half_masks[hd % 2], src, 0.0)
            blocks.append(jnp.concatenate(
                [tile if lt == hd // 2 else zeros for lt in range(MXU_DIM // LANES)], axis=1))
        return jnp.concatenate(blocks, axis=0)

    groups = range(n_groups)
    sls = [slice(grp * MXU_DIM, (grp + 1) * MXU_DIM) for grp in groups]
    r = [r_ref[:, sl] for sl in sls]
    k = [k_ref[:, sl] for sl in sls]
    v = [v_ref[:, sl] for sl in sls]
    lw_n, la_n, lg_n = lw_ref[...], la_ref[...], lg_ref[...]
    w = [_log_sigmoid(w0_ref[:, sl] + _dot(lw_n, w2_ref[:, sl])) - 0.5 for sl in sls]
    a = [_sigmoid(a0_ref[:, sl] + _dot(la_n, a2_ref[:, sl])) for sl in sls]
    g = [_dot(lg_n, g2_ref[:, sl]) for sl in sls]
    if has_vres:
        lv_n = lv_ref[...]
        mix_v = [_sigmoid(v0_ref[:, sl] + _dot(lv_n, v2_ref[:, sl])) for sl in sls]
        v = [v[i] + (vf_ref[:, sls[i]] - v[i]) * mix_v[i] for i in groups]
    lw = [-jnp.exp(x) for x in w]
    kk = [k[i] * kk_ref[:, sls[i]] for i in groups]
    kk_ss = [head_sum(x * x) for x in kk]
    cum = [_split_dot(tri_b, x, 2) for x in lw]
    kk = [kk[i] / jnp.maximum(jnp.sqrt(kk_ss[i]), 1e-12) for i in groups]
    kmod = [k[i] * (1.0 + (a[i] - 1.0) * ka_ref[:, sls[i]]) for i in groups]
    kka = [kk[i] * a[i] for i in groups]
    last = [x[c_len - 1:c_len, :] for x in cum]
    e_neg = [jnp.exp(-x) for x in cum]
    ar = [jnp.concatenate([stack_masked(-kk[i] * jnp.exp(cum[i] - lw[i])),
                           stack_masked(r[i] * jnp.exp(cum[i]))], axis=0).astype(BF16) for i in groups]
    bk = [jnp.concatenate([kka[i] * e_neg[i], kmod[i] * e_neg[i]], axis=0).astype(BF16) for i in groups]
    z = [_dot_nt(ar[i], bk[i]) for i in groups]
    za = [jnp.where(strict_c, x[:MXU_DIM], 0.0) for x in z]
    zr = [jnp.where(incl_c, x[MXU_DIM:], 0.0) for x in z]
    a_ak = [expand(x, 1).astype(BF16) for x in za]
    a_r = [jnp.concatenate([expand(x, 0), expand(x, 1)], axis=1).astype(BF16) for x in zr]

    rhs = [jnp.where(low_half, x, eye_c) for x in za]
    p_bd = [expand(x, 0).astype(BF16) for x in za]
    n_steps = (c_len - 1).bit_length()
    for step in range(n_steps):
        x2 = [_dot(p_bd[i], rhs[i].astype(BF16)) for i in groups]
        rhs = [x2[i] + jnp.where(low_half, 0.0, rhs[i]) for i in groups]
        if step + 1 < n_steps:
            p_bd = [expand(x, 0).astype(BF16) for x in x2]
    tinv = [expand(x, 1) for x in rhs]

    v_st = [stack_masked(x) for x in v]
    s = [s_ref[i] for i in groups]
    y = [_dot_nt(ar[i], s[i].astype(BF16)) for i in groups]
    av = [_dot(a_ak[i], v_st[i].astype(BF16)) for i in groups]
    u = [_dot(tinv[i].astype(BF16), (y[i][:MXU_DIM] + av[i]).astype(BF16)) for i in groups]
    uv = [jnp.concatenate([u[i], v_st[i]], axis=0).astype(BF16) for i in groups]
    o_st = [y[i][MXU_DIM:] + _dot(a_r[i], uv[i]) for i in groups]
    e_last = [jnp.exp(last[i] - cum[i]) for i in groups]
    bk_hat = [jnp.concatenate([stack_masked(kka[i] * e_last[i]), stack_masked(kmod[i] * e_last[i])],
                              axis=0).astype(BF16) for i in groups]
    for i in groups:
        s_ref[i] = s[i] * jnp.exp(last[i]) + _dot_tn(uv[i], bk_hat[i])

    o = [functools.reduce(lambda p, q: p + q,
                          [x[hd * c_len:(hd + 1) * c_len] for hd in range(HEADS_PER_GROUP)]) for x in o_st]
    inv_n = 1.0 / RWKV_HEAD
    mu = [head_sum(x) * inv_n for x in o]
    bonus_dot = [head_sum(r[i] * kmod[i] * rk_ref[:, sls[i]]) for i in groups]
    dev = [o[i] - mu[i] for i in groups]
    var = [head_sum(x * x) * inv_n for x in dev]
    for i in groups:
        sl = sls[i]
        o_n = dev[i] * lax.rsqrt(var[i] + RWKV_GN_EPS) * gnw_ref[:, sl] + gnb_ref[:, sl]
        y_ref[:, sl] = ((o_n + bonus_dot[i] * v[i]) * g[i]).astype(y_ref.dtype)


def _wkv7(b, r, k, v, lora, lora_w, lora_b, vres, k_k, k_a, r_k, lnx_w, lnx_b, s0_bd):
    m, d = r.shape
    t = m // b
    n_chunks = t // CHUNK
    n_groups = d // MXU_DIM
    rows = lambda n: pl.BlockSpec((CHUNK, n), lambda i, j: (i * n_chunks + j, 0))
    whole = lambda w: pl.BlockSpec(w.shape, lambda i, j: (0, 0))
    vec_spec = pl.BlockSpec((1, d), lambda i, j: (0, 0))
    st_spec = pl.BlockSpec((None, n_groups, MXU_DIM, MXU_DIM), lambda i, j: (i, 0, 0, 0))
    vec = lambda p: p.reshape(1, d)
    args = [r, k, v, *lora, *lora_w, *[vec(x) for x in lora_b]]
    specs = ([rows(d)] * 3 + [rows(x.shape[1]) for x in lora] + [whole(w) for w in lora_w]
             + [vec_spec] * len(lora_b))
    if vres is not None:
        v_first, lv, v2, v0 = vres
        args += [v_first, lv, v2, vec(v0)]
        specs += [rows(d), rows(lv.shape[1]), whole(v2), vec_spec]
    args += [vec(k_k), vec(k_a), vec(r_k), vec(lnx_w), vec(lnx_b), s0_bd]
    specs += [vec_spec] * 5 + [st_spec]
    return pl.pallas_call(
        functools.partial(_wkv7_kernel, has_vres=vres is not None),
        grid=(b, n_chunks),
        in_specs=specs,
        out_specs=[rows(d), st_spec],
        out_shape=[jax.ShapeDtypeStruct((m, d), BF16),
                   jax.ShapeDtypeStruct(s0_bd.shape, F32)],
        compiler_params=_params(("arbitrary", "arbitrary")),
        name="wkv7",
    )(*args)


def _wkv_state_to_blockdiag(s):
    b, h, n, _ = s.shape
    g = h // HEADS_PER_GROUP
    s = s.reshape(b, g, HEADS_PER_GROUP, n, n)
    row_blocks = [jnp.pad(s[:, :, i], ((0, 0), (0, 0), (0, 0), (i * n, MXU_DIM - (i + 1) * n)))
                  for i in range(HEADS_PER_GROUP)]
    return jnp.concatenate(row_blocks, axis=2)


def _wkv_state_from_blockdiag(s_bd):
    b, g = s_bd.shape[:2]
    n = RWKV_HEAD
    blocks = [s_bd[:, :, i * n:(i + 1) * n, i * n:(i + 1) * n] for i in range(HEADS_PER_GROUP)]
    return jnp.stack(blocks, axis=2).reshape(b, g * HEADS_PER_GROUP, n, n)


def _gla_kernel(q_ref, k_ref, v_ref, gate_ref, lr_ref, w2_ref, gb_ref, hn_ref, s0_ref, y_ref, sn_ref,
                s_ref):
    tc = q_ref.shape[0]
    n_heads, dv, dk = s_ref.shape
    heads = range(n_heads)
    ksl = [slice(h * dk, (h + 1) * dk) for h in heads]
    vsl = [slice(h * dv, (h + 1) * dv) for h in heads]

    @pl.when(pl.program_id(1) == 0)
    def _():
        for h in heads:
            s_ref[h] = s0_ref[h].T

    tri = _tri_incl(CHUNK)
    tri_b = jnp.where(tri, 1.0, 0.0).astype(BF16)
    scale = dk ** -0.5
    for c in range(tc // CHUNK):
        rows = slice(c * CHUNK, (c + 1) * CHUNK)
        lr = lr_ref[rows, :].astype(BF16)
        gk = [_log_sigmoid(_dot(lr, w2_ref[:, sl]) + gb_ref[:, sl]) * (1.0 / GLA_GATE_NORMALIZER)
              for sl in ksl]
        cum = [_split_dot(tri_b, x, 3) for x in gk]
        last = [x[CHUNK - 1:CHUNK, :] for x in cum]
        qe = [(q_ref[rows, ksl[h]] * scale * jnp.exp(cum[h])).astype(BF16) for h in heads]
        ke = [(k_ref[rows, ksl[h]] * jnp.exp(-cum[h])).astype(BF16) for h in heads]
        k2 = [(k_ref[rows, ksl[h]] * jnp.exp(last[h] - cum[h])).astype(BF16) for h in heads]
        vb = [v_ref[rows, sl].astype(BF16) for sl in vsl]
        scores = [jnp.where(tri, _dot_nt(qe[h], ke[h]), 0.0).astype(BF16) for h in heads]
        s_t = [s_ref[h] for h in heads]
        o = [_dot(scores[h], vb[h]) + _dot_nt(qe[h], s_t[h].astype(BF16)) for h in heads]
        for h in heads:
            s_ref[h] = s_t[h] * jnp.exp(last[h]) + _dot_tn(vb[h], k2[h])
        for h in heads:
            o_n = _rms(o[h], GLA_NORM_EPS) * hn_ref[...]
            y_ref[rows, vsl[h]] = (o_n * _silu(gate_ref[rows, vsl[h]])).astype(y_ref.dtype)

    @pl.when(pl.program_id(1) == pl.num_programs(1) - 1)
    def _():
        for h in heads:
            sn_ref[h] = s_ref[h].T


def _gla(b, qkvg, lr, gk_w2, gk_b, head_norm, s0):
    t = qkvg.shape[0] // b
    h, dk, dv = s0.shape[1:]
    tc = min(t, 2 * CHUNK)
    dk_all, dv_all = h * dk, h * dv
    assert dv_all == 2 * dk_all
    st_spec = pl.BlockSpec((None, h, dk, dv), lambda i, c: (i, 0, 0, 0))
    n_steps = t // tc
    rows = lambda n, col: pl.BlockSpec((tc, n), lambda i, c: (i * n_steps + c, col))
    return pl.pallas_call(
        _gla_kernel,
        grid=(b, t // tc),
        in_specs=[rows(dk_all, 0), rows(dk_all, 1), rows(dv_all, 1), rows(dv_all, 2), rows(LANES, 0),
                  pl.BlockSpec((LANES, dk_all), lambda i, c: (0, 0)),
                  pl.BlockSpec((1, dk_all), lambda i, c: (0, 0)),
                  pl.BlockSpec((1, dv), lambda i, c: (0, 0)),
                  st_spec],
        out_specs=[rows(dv_all, 0), st_spec],
        out_shape=[jax.ShapeDtypeStruct((b * t, dv_all), BF16),
                   jax.ShapeDtypeStruct(s0.shape, F32)],
        scratch_shapes=[pltpu.VMEM((h, dv, dk), F32)],
        compiler_params=_params(("arbitrary", "arbitrary")),
        name="gla",
    )(qkvg, qkvg, qkvg, qkvg, lr, gk_w2, gk_b.reshape(1, -1), head_norm.reshape(1, dv), s0)


def _ffn_kernel(x_ref, g_ref, sv_ref, sg_ref, wv_ref, wg_ref, cwv_ref, cwg_ref, cbv_ref, cbg_ref,
                wd_ref, og_ref, o_ref, nsv_ref, nsg_ref, h_ref, acc_ref, slab_ref, ua_ref, ub_ref, carry_ref,
                *, nj, nseq, seq_len, norm_output):
    rows, d = x_ref.shape
    tn = wv_ref.shape[1]
    j = pl.program_id(2)
    jt = jnp.maximum(j - 1, 0)

    n_ph = min(8, seq_len // 16)
    q_len = seq_len // n_ph
    phase_rows = lambda p: pl.ds(p, nseq * q_len, stride=n_ph)
    blk = lambda p: slice(p * q_len, (p + 1) * q_len)
    slabs = [slice(c * LANES, (c + 1) * LANES) for c in range(d // LANES)]

    @pl.when(j == 0)
    def _():
        for c, cl in enumerate(slabs):
            slab_ref[c] = x_ref[:, cl]
        for c, cl in enumerate(slabs):
            for p in range(n_ph):
                acc_ref[:, blk(p), cl] = slab_ref[c, phase_rows(p), :].reshape(nseq, q_len, LANES)
        h_ref[...] = (_rms(acc_ref[...], NORM_EPS) * g_ref[...]).astype(BF16)

    @pl.when((pl.program_id(1) == 0) & (j > 0))
    def _():
        carry_ref[2 * jt] = sv_ref[...]
        carry_ref[2 * jt + 1] = sg_ref[...]

    cols = [slice(c * MXU_DIM, (c + 1) * MXU_DIM) for c in range(tn // MXU_DIM)]
    first = lax.broadcasted_iota(jnp.int32, (1, q_len, 1), 1) == 0

    def conv(u, cs, cw_ref, cb_ref, slot, ns_ref):
        u = u.reshape(nseq, seq_len, MXU_DIM)
        w0, w1, w2, cb = cw_ref[0:1, cs], cw_ref[1:2, cs], cw_ref[2:3, cs], cb_ref[:, cs]
        p2, p1 = carry_ref[slot, :, 0:1, cs], carry_ref[slot, :, 1:2, cs]
        back1 = jnp.where(first, p1, pltpu.roll(u[:, blk(n_ph - 1), :], 1, axis=1))
        back2 = jnp.where(first, p2, pltpu.roll(u[:, blk(n_ph - 2), :], 1, axis=1))
        r1 = jnp.concatenate([back1, u[:, :(n_ph - 1) * q_len, :]], axis=1)
        r2 = jnp.concatenate([back2, back1, u[:, :(n_ph - 2) * q_len, :]], axis=1)
        last = seq_len - 1
        before_last = (n_ph - 1) * q_len - 1
        for row, src in ((0, before_last), (1, last)):
            carry_ref[slot, :, row:row + 1, cs] = u[:, src:src + 1, :]
            ns_ref[:, row:row + 1, cs] = u[:, src:src + 1, :]
        return cb + w0 * r2 + w1 * r1 + w2 * u

    def step(up_ref, dn_ref):
        h = None if up_ref is None else h_ref[...].reshape(rows, d)
        acc = None
        for cs in cols:
            if up_ref is not None:
                up_ref[0, :, cs] = _dot(h, wv_ref[:, cs])
            if dn_ref is not None:
                val = conv(dn_ref[0, :, cs], cs, cwv_ref, cbv_ref, 2 * jt, nsv_ref)
            if up_ref is not None:
                up_ref[1, :, cs] = _dot(h, wg_ref[:, cs])
            if dn_ref is not None:
                gate = conv(dn_ref[1, :, cs], cs, cwg_ref, cbg_ref, 2 * jt + 1, nsg_ref)
                act = (_silu(gate) * val).reshape(rows, MXU_DIM).astype(BF16)
                part = _dot(act, wd_ref[cs, :])
                acc = part if acc is None else acc + part
        if dn_ref is not None:
            acc_ref[...] += acc.reshape(nseq, seq_len, d)

    even = (j & 1) == 0
    inner = (j > 0) & (j < nj)

    @pl.when(j == 0)
    def _():
        step(ua_ref, None)

    @pl.when(inner & even)
    def _():
        step(ua_ref, ub_ref)

    @pl.when(inner & jnp.logical_not(even))
    def _():
        step(ub_ref, ua_ref)

    @pl.when(j == nj)
    def _():
        step(None, ua_ref if (nj - 1) % 2 == 0 else ub_ref)
        if norm_output:
            acc_ref[...] = _rms(acc_ref[...], NORM_EPS) * og_ref[...]
        for c, cl in enumerate(slabs):
            for p in range(n_ph):
                slab_ref[c, phase_rows(p), :] = acc_ref[:, blk(p), cl].reshape(nseq * q_len, LANES)
        for c, cl in enumerate(slabs):
            o_ref[:, cl] = slab_ref[c]


FFN_TILE_COLS = 512


def _conv_ffn(b, x, conv_state, g, w_up, conv_w, conv_b, w_down, out_g, norm_output, nseq, seq_len):
    m, d = x.shape
    t = m // b
    f = w_down.shape[0]
    tn = FFN_TILE_COLS
    nj = f // tn
    assert b % nseq == 0 and t % seq_len == 0 and f % tn == 0
    assert nseq == 1 or seq_len == t
    x_spec = pl.BlockSpec((nseq * seq_len, d), lambda i, s, j: (i * (t // seq_len) + s, 0))
    up_tile = lambda j: jnp.minimum(j, nj - 1)
    dn_tile = lambda j: jnp.maximum(j - 1, 0)
    st_v = pl.BlockSpec((nseq, 2, tn), lambda i, s, j: (i, 0, dn_tile(j)))
    st_g = pl.BlockSpec((nseq, 2, tn), lambda i, s, j: (i, 0, nj + dn_tile(j)))
    n_s = t // seq_len
    tail_spec = pl.BlockSpec((nseq, None, 2, tn), lambda i, s, j: (i, s, 0, dn_tile(j)))
    col_v = lambda n: pl.BlockSpec((n, tn), lambda i, s, j: (0, dn_tile(j)))
    col_g = lambda n: pl.BlockSpec((n, tn), lambda i, s, j: (0, nj + dn_tile(j)))
    y, ns_v, ns_g = pl.pallas_call(
        functools.partial(_ffn_kernel, nj=nj, nseq=nseq, seq_len=seq_len, norm_output=norm_output),
        grid=(b // nseq, n_s, nj + 1),
        in_specs=[x_spec, pl.BlockSpec((1, d), lambda i, s, j: (0, 0)), st_v, st_g,
                  pl.BlockSpec((d, tn), lambda i, s, j: (0, up_tile(j))),
                  pl.BlockSpec((d, tn), lambda i, s, j: (0, nj + up_tile(j))),
                  col_v(3), col_g(3), col_v(1), col_g(1),
                  pl.BlockSpec((tn, d), lambda i, s, j: (dn_tile(j), 0)),
                  pl.BlockSpec((1, d), lambda i, s, j: (0, 0))],
        out_specs=[x_spec, tail_spec, tail_spec],
        out_shape=[jax.ShapeDtypeStruct((m, d), F32),
                   jax.ShapeDtypeStruct((b, n_s, 2, f), F32),
                   jax.ShapeDtypeStruct((b, n_s, 2, f), F32)],
        scratch_shapes=[pltpu.VMEM((nseq, seq_len, d), BF16),
                        pltpu.VMEM((nseq, seq_len, d), F32),
                        pltpu.VMEM((d // LANES, nseq * seq_len, LANES), F32),
                        pltpu.VMEM((2, nseq * seq_len, tn), F32),
                        pltpu.VMEM((2, nseq * seq_len, tn), F32),
                        pltpu.VMEM((2 * nj, nseq, 2, tn), F32)],
        compiler_params=_params(("arbitrary", "arbitrary", "arbitrary")),
        name="conv_ffn",
    )(x, g.reshape(1, d), conv_state, conv_state, w_up, w_up, conv_w, conv_w,
      conv_b.reshape(1, -1), conv_b.reshape(1, -1), w_down, out_g.reshape(1, d))
    return y, jnp.concatenate([ns_v[:, -1], ns_g[:, -1]], axis=-1)


def _pad_cols(w, n):
    return jnp.pad(w, ((0, 0), (0, n - w.shape[1])))


def _pad_rows(w, n):
    return jnp.pad(w, ((0, n - w.shape[0]), (0, 0)))


def _rwkv_layer(b, x, shift_st, wkv_st, v_first, p):
    pre, new_shift = _rwkv_pre(b, x, shift_st, p["norm"], p["mix"], p["w1"], p["a1"], p["g1"], p["v1"])
    xr, xk, xv = pre[:3]
    r = _matmul(xr, p["wr"], name="rwkv_r")
    k = _matmul(xk, p["wk"], name="rwkv_k")
    v = _matmul(xv, p["wv"], name="rwkv_v")
    if p["v1"] is None:
        vres, v_first = None, v
    else:
        vres = (v_first, pre[6], p["v2"], p["v0"])
    y, s_bd = _wkv7(b, r, k, v, pre[3:6], (p["w2"], p["a2"], p["g2"]), (p["w0"], p["a0"]), vres,
                    p["k_k"], p["k_a"], p["r_k"], p["lnx_w"], p["lnx_b"],
                    _wkv_state_to_blockdiag(wkv_st))
    x_new = _matmul(y, p["wo"], extras=(x,), epilogue=lambda acc, res: res + acc, name="rwkv_o")
    return x_new, new_shift, _wkv_state_from_blockdiag(s_bd), v_first


def _gla_layer(b, x, gla_st, p):
    h = _rmsnorm(x, p["norm"], BF16)
    qkvg = _matmul(h, p["w_in"], n_cols=p["n_main"], name="gla_in")
    lr = _matmul(h, p["w_lr"], name="gla_lr")
    y, s_new = _gla(b, qkvg, lr, p["gk_w2"], p["gk_b"], p["head_norm"], gla_st)
    x_new = _matmul(y, p["wo"], extras=(x,), epilogue=lambda acc, res: res + acc, name="gla_o")
    return x_new, s_new


FFN_TILE_ROWS = 512


def _ffn_tile(b, t):
    if t >= FFN_TILE_ROWS:
        return 1, FFN_TILE_ROWS
    return min(b, FFN_TILE_ROWS // t), t


def _run_trunk(x, shift_st, wkv_st, gla_st, conv_st, layers, norm_final, ffn_nseq, ffn_len):
    new_shift, new_wkv, new_gla, new_conv = [], [], [], []
    v_first = None
    b, t, d = x.shape
    x = x.reshape(b * t, d)
    for i, (mixer, ffn) in enumerate(layers):
        j = i // 2
        if i % 2 == 0:
            x, s_shift, s_wkv, v_first = _rwkv_layer(b, x, shift_st[j], wkv_st[j], v_first, mixer)
            new_shift.append(s_shift)
            new_wkv.append(s_wkv)
        else:
            x, s_gla = _gla_layer(b, x, gla_st[j], mixer)
            new_gla.append(s_gla)
        x, s_conv = _conv_ffn(b, x, conv_st[i], ffn["norm"], ffn["w_up"], ffn["conv_w"], ffn["conv_b"],
                              ffn["w_down"], norm_final, i == len(layers) - 1, ffn_nseq, ffn_len)
        new_conv.append(s_conv)
    return x.reshape(b, t, d), jnp.stack(new_shift), jnp.stack(new_wkv), jnp.stack(new_gla), jnp.stack(new_conv)


def kernel(x_prompt, x_sample, state_rwkv_shift, state_rwkv_wkv, state_gla, state_ffn_conv, norm_mix, norm_ffn, norm_final, rwkv_mix, rwkv_w0, rwkv_w1, rwkv_w2, rwkv_a0, rwkv_a1, rwkv_a2, rwkv_v0, rwkv_v1, rwkv_v2, rwkv_g1, rwkv_g2, rwkv_k_k, rwkv_k_a, rwkv_r_k, rwkv_wr, rwkv_wk, rwkv_wv, rwkv_wo, rwkv_lnx_w, rwkv_lnx_b, gla_w_in, gla_gk_w2, gla_gk_b, gla_head_norm, gla_wo, ffn_w_up, ffn_conv_w, ffn_conv_b, ffn_w_down):
    depth = norm_mix.shape[0]
    d = x_prompt.shape[-1]
    bf = lambda w: w.astype(BF16)
    dk_total = gla_gk_w2.shape[-1]
    n_main = gla_w_in.shape[-1] - GLA_GATE_RANK
    layers = []
    for i in range(depth):
        j = i // 2
        if i % 2 == 0:
            has_vres = j > 0
            mixer = dict(
                norm=norm_mix[i], mix=rwkv_mix[j],
                wr=bf(rwkv_wr[j]), wk=bf(rwkv_wk[j]), wv=bf(rwkv_wv[j]), wo=bf(rwkv_wo[j]),
                w0=rwkv_w0[j], w1=bf(_pad_cols(rwkv_w1[j], LANES)), w2=bf(_pad_rows(rwkv_w2[j], LANES)),
                a0=rwkv_a0[j], a1=bf(_pad_cols(rwkv_a1[j], LANES)), a2=bf(_pad_rows(rwkv_a2[j], LANES)),
                g1=bf(rwkv_g1[j]), g2=bf(rwkv_g2[j]),
                v0=rwkv_v0[j - 1] if has_vres else None,
                v1=bf(_pad_cols(rwkv_v1[j - 1], LANES)) if has_vres else None,
                v2=bf(_pad_rows(rwkv_v2[j - 1], LANES)) if has_vres else None,
                k_k=rwkv_k_k[j], k_a=rwkv_k_a[j], r_k=rwkv_r_k[j].reshape(d),
                lnx_w=rwkv_lnx_w[j], lnx_b=rwkv_lnx_b[j])
        else:
            mixer = dict(
                norm=norm_mix[i],
                w_in=bf(gla_w_in[j]), n_main=n_main,
                w_lr=bf(_pad_cols(gla_w_in[j][:, n_main:], LANES)),
                gk_w2=bf(_pad_rows(gla_gk_w2[j], LANES)), gk_b=gla_gk_b[j],
                head_norm=gla_head_norm[j], wo=bf(gla_wo[j]))
        ffn = dict(norm=norm_ffn[i], w_up=bf(ffn_w_up[i]), conv_w=ffn_conv_w[i],
                   conv_b=ffn_conv_b[i],
                   w_down=bf(ffn_w_down[i]))
        layers.append((mixer, ffn))

    bp, tp, _ = x_prompt.shape
    bs, ts, _ = x_sample.shape
    zeros_like_state = lambda s: jnp.zeros((s.shape[0], bp) + s.shape[2:], s.dtype)
    out_p = _run_trunk(x_prompt, zeros_like_state(state_rwkv_shift), zeros_like_state(state_rwkv_wkv),
                       zeros_like_state(state_gla), zeros_like_state(state_ffn_conv), layers,
                       norm_final, *_ffn_tile(bp, tp))
    out_s = _run_trunk(x_sample, state_rwkv_shift, state_rwkv_wkv, state_gla, state_ffn_conv, layers,
                       norm_final, *_ffn_tile(bs, ts))
    return (out_p[0], out_s[0]) + tuple(out_p[1:]) + tuple(out_s[1:])
```

```python
import functools

import jax
import jax.numpy as jnp
from jax import lax
from jax.experimental import pallas as pl
from jax.experimental.pallas import tpu as pltpu

F32 = jnp.float32
BF16 = jnp.bfloat16

CHUNK = 64
RWKV_HEAD = 64
RWKV_GN_EPS = 64e-5
GLA_GATE_RANK = 16
GLA_GATE_NORMALIZER = 16.0
GLA_NORM_EPS = 1e-5
NORM_EPS = 1e-6

LANES = 128
MXU_DIM = 256
HEADS_PER_GROUP = MXU_DIM // RWKV_HEAD
VMEM_LIMIT = 56 * 1024 * 1024


def _params(sem):
    return pltpu.CompilerParams(dimension_semantics=sem, vmem_limit_bytes=VMEM_LIMIT)


def _dot(a, b):
    return jnp.dot(a, b, preferred_element_type=F32)


def _dot_nt(a, b):
    return lax.dot_general(a, b, (((1,), (1,)), ((), ())), preferred_element_type=F32)


def _dot_tn(a, b):
    return lax.dot_general(a, b, (((0,), (0,)), ((), ())), preferred_element_type=F32)


def _split_dot(fixed_bf16, x, terms):
    acc = None
    rem = x
    for _ in range(terms):
        piece = rem.astype(BF16)
        rem = rem - piece.astype(F32)
        part = _dot(fixed_bf16, piece)
        acc = part if acc is None else acc + part
    return acc


def _log_sigmoid(x):
    return jnp.minimum(x, 0.0) - jnp.log(1.0 + jnp.exp(-jnp.abs(x)))


def _sigmoid(x):
    return 1.0 / (1.0 + jnp.exp(-x))


def _silu(x):
    return x * _sigmoid(x)


def _rms(x, eps):
    return x * lax.rsqrt(jnp.mean(x * x, axis=-1, keepdims=True) + eps)


def _tri_incl(n):
    r = lax.broadcasted_iota(jnp.int32, (n, n), 0)
    c = lax.broadcasted_iota(jnp.int32, (n, n), 1)
    return c <= r


def _rmsnorm_kernel(x_ref, g_ref, o_ref):
    o_ref[...] = (_rms(x_ref[...], NORM_EPS) * g_ref[...]).astype(o_ref.dtype)


def _rmsnorm(x, g, out_dtype):
    m, d = x.shape
    tm = min(m, 1024)
    return pl.pallas_call(
        _rmsnorm_kernel,
        grid=(m // tm,),
        in_specs=[pl.BlockSpec((tm, d), lambda i: (i, 0)), pl.BlockSpec((1, d), lambda i: (0, 0))],
        out_specs=pl.BlockSpec((tm, d), lambda i: (i, 0)),
        out_shape=jax.ShapeDtypeStruct((m, d), out_dtype),
        compiler_params=_params(("arbitrary",)),
        name="rmsnorm",
    )(x, g.reshape(1, d))


def _mm_kernel(a_ref, b_ref, *rest, epilogue, n_extra):
    o_ref = rest[n_extra]
    acc = _dot(a_ref[...], b_ref[...])
    if epilogue is not None:
        acc = epilogue(acc, *[e[...] for e in rest[:n_extra]])
    o_ref[...] = acc.astype(o_ref.dtype)


def _matmul(a, b, extras=(), epilogue=None, out_dtype=F32, tm=1024, tn=1024, n_cols=None,
            name="matmul"):
    m, k = a.shape
    n = b.shape[1] if n_cols is None else n_cols
    tm, tn = min(tm, m), min(tn, n)
    assert m % tm == 0 and n % tn == 0, (m, n, tm, tn)
    in_specs = [pl.BlockSpec((tm, k), lambda i, j: (i, 0)), pl.BlockSpec((k, tn), lambda i, j: (0, j))]
    for e in extras:
        if e.shape[0] == 1:
            in_specs.append(pl.BlockSpec((1, tn), lambda i, j: (0, j)))
        else:
            in_specs.append(pl.BlockSpec((tm, tn), lambda i, j: (i, j)))
    return pl.pallas_call(
        functools.partial(_mm_kernel, epilogue=epilogue, n_extra=len(extras)),
        grid=(m // tm, n // tn),
        in_specs=in_specs,
        out_specs=pl.BlockSpec((tm, tn), lambda i, j: (i, j)),
        out_shape=jax.ShapeDtypeStruct((m, n), out_dtype),
        compiler_params=_params(("arbitrary", "arbitrary")),
        name=name,
    )(a, b, *extras)


def _rwkv_pre_kernel(x_ref, st_ref, g_ref, mix_ref, w1_ref, a1_ref, g1_ref, *rest, has_v1):
    if has_v1:
        v1_ref, rest = rest[0], rest[1:]
    xr_ref, xk_ref, xv_ref, lw_ref, la_ref, lg_ref = rest[:6]
    rest = rest[6:]
    if has_v1:
        lv_ref, rest = rest[0], rest[1:]
    hl_ref, carry_ref = rest
    tt = x_ref.shape[0]

    @pl.when(pl.program_id(1) == 0)
    def _():
        carry_ref[...] = st_ref[...]

    h = _rms(x_ref[...], NORM_EPS) * g_ref[...]
    row = lax.broadcasted_iota(jnp.int32, (tt, 1), 0)
    h_prev = jnp.where(row == 0, carry_ref[...], pltpu.roll(h, 1, axis=0))
    carry_ref[...] = h[tt - 1:tt, :]
    hl_ref[...] = h[tt - 8:, :]
    xx = h_prev - h
    mixed = lambda i: (h + xx * mix_ref[i:i + 1, :]).astype(BF16)
    xr_ref[...] = mixed(0)
    xk_ref[...] = mixed(2)
    xv = mixed(3)
    xv_ref[...] = xv
    lw_ref[...] = jnp.tanh(_dot(mixed(1), w1_ref[...])).astype(BF16)
    la_ref[...] = _dot(mixed(4), a1_ref[...]).astype(BF16)
    lg_ref[...] = _sigmoid(_dot(mixed(5), g1_ref[...])).astype(BF16)
    if has_v1:
        lv_ref[...] = _dot(xv, v1_ref[...]).astype(BF16)


def _rwkv_pre(b, x, shift_state, g, mix, w1, a1, g1, v1):
    m, d = x.shape
    t = m // b
    tt = min(t, 512)
    nt = t // tt
    has_v1 = v1 is not None
    lora = [w1, a1, g1] + ([v1] if has_v1 else [])
    flat_spec = lambda n: pl.BlockSpec((tt, n), lambda i, j: (i * nt + j, 0))
    full_spec = lambda w: pl.BlockSpec(w.shape, lambda i, j: (0, 0))
    wide = [jax.ShapeDtypeStruct((b * t, d), BF16)] * 3
    narrow = [jax.ShapeDtypeStruct((b * t, w.shape[1]), BF16) for w in lora]
    outs = pl.pallas_call(
        functools.partial(_rwkv_pre_kernel, has_v1=has_v1),
        grid=(b, nt),
        in_specs=[flat_spec(d),
                  pl.BlockSpec((None, 1, d), lambda i, j: (i, 0, 0)),
                  pl.BlockSpec((1, d), lambda i, j: (0, 0)),
                  pl.BlockSpec((6, d), lambda i, j: (0, 0))] + [full_spec(w) for w in lora],
        out_specs=[flat_spec(d)] * 3 + [flat_spec(w.shape[1]) for w in lora]
                  + [pl.BlockSpec((None, 8, d), lambda i, j: (i, 0, 0))],
        out_shape=wide + narrow + [jax.ShapeDtypeStruct((b, 8, d), F32)],
        scratch_shapes=[pltpu.VMEM((1, d), F32)],
        compiler_params=_params(("arbitrary", "arbitrary")),
        name="rwkv_pre",
    )(x, shift_state.reshape(b, 1, d), g.reshape(1, d), mix, *lora)
    return outs[:-1], outs[-1][:, 7, :]


def _wkv7_kernel(*refs, has_vres):
    refs = list(refs)
    r_ref, k_ref, v_ref, lw_ref, la_ref, lg_ref = refs[:6]
    w2_ref, a2_ref, g2_ref, w0_ref, a0_ref = refs[6:11]
    refs = refs[11:]
    if has_vres:
        vf_ref, lv_ref, v2_ref, v0_ref = refs[:4]
        refs = refs[4:]
    kk_ref, ka_ref, rk_ref, gnw_ref, gnb_ref, s0_ref, y_ref, s_ref = refs
    c_len, d = r_ref.shape
    n_groups = d // MXU_DIM

    @pl.when(pl.program_id(1) == 0)
    def _():
        s_ref[...] = s0_ref[...]

    ri = lax.broadcasted_iota(jnp.int32, (MXU_DIM, MXU_DIM), 0)
    ci = lax.broadcasted_iota(jnp.int32, (MXU_DIM, MXU_DIM), 1)
    head_bits = RWKV_HEAD.bit_length() - 1
    same_head = (ri >> head_bits) == (ci >> head_bits)
    bd = jnp.where(same_head, 1.0, 0.0)
    bd_b = bd.astype(BF16)
    tri_b = jnp.where(_tri_incl(c_len), 1.0, 0.0).astype(BF16)

    assert c_len == RWKV_HEAD and 2 * RWKV_HEAD == LANES
    t_c = lax.broadcasted_iota(jnp.int32, (MXU_DIM, LANES), 0) & (c_len - 1)
    lane_c = lax.broadcasted_iota(jnp.int32, (MXU_DIM, LANES), 1)
    s_c = lane_c & (RWKV_HEAD - 1)
    low_half = lane_c < RWKV_HEAD
    strict_c = s_c < t_c
    incl_c = s_c <= t_c
    eye_c = jnp.where(s_c == t_c, 1.0, 0.0)
    lane_t = lax.broadcasted_iota(jnp.int32, (c_len, LANES), 1)
    half_masks = (lane_t < RWKV_HEAD, lane_t >= RWKV_HEAD)

    def stack(x):
        return jnp.concatenate([x] * HEADS_PER_GROUP, axis=0)

    def stack_masked(x):
        return stack(x) * bd

    def head_sum(x):
        return _dot(x.astype(BF16), bd_b)

    def expand(xc, half):
        zeros = jnp.zeros((c_len, LANES), F32)
        blocks = []
        for hd in range(HEADS_PER_GROUP):
            src = xc[hd * c_len:(hd + 1) * c_len]
            if hd % 2 != half:
                src = pltpu.roll(src, RWKV_HEAD, axis=1)
            tile = jnp.where(half_masks[hd % 2], src, 0.0)
            blocks.append(jnp.concatenate(
                [tile if lt == hd // 2 else zeros for lt in range(MXU_DIM // LANES)], axis=1))
        return jnp.concatenate(blocks, axis=0)

    groups = range(n_groups)
    sls = [slice(grp * MXU_DIM, (grp + 1) * MXU_DIM) for grp in groups]
    r = [r_ref[:, sl] for sl in sls]
    k = [k_ref[:, sl] for sl in sls]
    v = [v_ref[:, sl] for sl in sls]
    lw_n, la_n, lg_n = lw_ref[...], la_ref[...], lg_ref[...]
    w = [_log_sigmoid(w0_ref[:, sl] + _dot(lw_n, w2_ref[:, sl])) - 0.5 for sl in sls]
    a = [_sigmoid(a0_ref[:, sl] + _dot(la_n, a2_ref[:, sl])) for sl in sls]
    g = [_dot(lg_n, g2_ref[:, sl]) for sl in sls]
    if has_vres:
        lv_n = lv_ref[...]
        mix_v = [_sigmoid(v0_ref[:, sl] + _dot(lv_n, v2_ref[:, sl])) for sl in sls]
        v = [v[i] + (vf_ref[:, sls[i]] - v[i]) * mix_v[i] for i in groups]
    lw = [-jnp.exp(x) for x in w]
    kk = [k[i] * kk_ref[:, sls[i]] for i in groups]
    kk_ss = [head_sum(x * x) for x in kk]
    cum = [_split_dot(tri_b, x, 2) for x in lw]
    kk = [kk[i] / jnp.maximum(jnp.sqrt(kk_ss[i]), 1e-12) for i in groups]
    kmod = [k[i] * (1.0 + (a[i] - 1.0) * ka_ref[:, sls[i]]) for i in groups]
    kka = [kk[i] * a[i] for i in groups]
    last = [x[c_len - 1:c_len, :] for x in cum]
    e_neg = [jnp.exp(-x) for x in cum]
    ar = [jnp.concatenate([stack_masked(-kk[i] * jnp.exp(cum[i] - lw[i])),
                           stack_masked(r[i] * jnp.exp(cum[i]))], axis=0).astype(BF16) for i in groups]
    bk = [jnp.concatenate([kka[i] * e_neg[i], kmod[i] * e_neg[i]], axis=0).astype(BF16) for i in groups]
    z = [_dot_nt(ar[i], bk[i]) for i in groups]
    za = [jnp.where(strict_c, x[:MXU_DIM], 0.0) for x in z]
    zr = [jnp.where(incl_c, x[MXU_DIM:], 0.0) for x in z]
    a_ak = [expand(x, 1).astype(BF16) for x in za]
    a_r = [jnp.concatenate([expand(x, 0), expand(x, 1)], axis=1).astype(BF16) for x in zr]

    rhs = [jnp.where(low_half, x, eye_c) for x in za]
    p_bd = [expand(x, 0).astype(BF16) for x in za]
    n_steps = (c_len - 1).bit_length()
    for step in range(n_steps):
        x2 = [_dot(p_bd[i], rhs[i].astype(BF16)) for i in groups]
        rhs = [x2[i] + jnp.where(low_half, 0.0, rhs[i]) for i in groups]
        if step + 1 < n_steps:
            p_bd = [expand(x, 0).astype(BF16) for x in x2]
    tinv = [expand(x, 1) for x in rhs]

    v_st = [stack_masked(x) for x in v]
    s = [s_ref[i] for i in groups]
    y = [_dot_nt(ar[i], s[i].astype(BF16)) for i in groups]
    av = [_dot(a_ak[i], v_st[i].astype(BF16)) for i in groups]
    u = [_dot(tinv[i].astype(BF16), (y[i][:MXU_DIM] + av[i]).astype(BF16)) for i in groups]
    uv = [jnp.concatenate([u[i], v_st[i]], axis=0).astype(BF16) for i in groups]
    o_st = [y[i][MXU_DIM:] + _dot(a_r[i], uv[i]) for i in groups]
    e_last = [jnp.exp(last[i] - cum[i]) for i in groups]
    bk_hat = [jnp.concatenate([stack_masked(kka[i] * e_last[i]), stack_masked(kmod[i] * e_last[i])],
                              axis=0).astype(BF16) for i in groups]
    for i in groups:
        s_ref[i] = s[i] * jnp.exp(last[i]) + _dot_tn(uv[i], bk_hat[i])

    o = [functools.reduce(lambda p, q: p + q,
                          [x[hd * c_len:(hd + 1) * c_len] for hd in range(HEADS_PER_GROUP)]) for x in o_st]
    inv_n = 1.0 / RWKV_HEAD
    mu = [head_sum(x) * inv_n for x in o]
    bonus_dot = [head_sum(r[i] * kmod[i] * rk_ref[:, sls[i]]) for i in groups]
    dev = [o[i] - mu[i] for i in groups]
    var = [head_sum(x * x) * inv_n for x in dev]
    for i in groups:
        sl = sls[i]
        o_n = dev[i] * lax.rsqrt(var[i] + RWKV_GN_EPS) * gnw_ref[:, sl] + gnb_ref[:, sl]
        y_ref[:, sl] = ((o_n + bonus_dot[i] * v[i]) * g[i]).astype(y_ref.dtype)


def _wkv7(b, r, k, v, lora, lora_w, lora_b, vres, k_k, k_a, r_k, lnx_w, lnx_b, s0_bd):
    m, d = r.shape
    t = m // b
    n_chunks = t // CHUNK
    n_groups = d // MXU_DIM
    rows = lambda n: pl.BlockSpec((CHUNK, n), lambda i, j: (i * n_chunks + j, 0))
    whole = lambda w: pl.BlockSpec(w.shape, lambda i, j: (0, 0))
    vec_spec = pl.BlockSpec((1, d), lambda i, j: (0, 0))
    st_spec = pl.BlockSpec((None, n_groups, MXU_DIM, MXU_DIM), lambda i, j: (i, 0, 0, 0))
    vec = lambda p: p.reshape(1, d)
    args = [r, k, v, *lora, *lora_w, *[vec(x) for x in lora_b]]
    specs = ([rows(d)] * 3 + [rows(x.shape[1]) for x in lora] + [whole(w) for w in lora_w]
             + [vec_spec] * len(lora_b))
    if vres is not None:
        v_first, lv, v2, v0 = vres
        args += [v_first, lv, v2, vec(v0)]
        specs += [rows(d), rows(lv.shape[1]), whole(v2), vec_spec]
    args += [vec(k_k), vec(k_a), vec(r_k), vec(lnx_w), vec(lnx_b), s0_bd]
    specs += [vec_spec] * 5 + [st_spec]
    return pl.pallas_call(
        functools.partial(_wkv7_kernel, has_vres=vres is not None),
        grid=(b, n_chunks),
        in_specs=specs,
        out_specs=[rows(d), st_spec],
        out_shape=[jax.ShapeDtypeStruct((m, d), BF16),
                   jax.ShapeDtypeStruct(s0_bd.shape, F32)],
        compiler_params=_params(("arbitrary", "arbitrary")),
        name="wkv7",
    )(*args)


def _wkv_state_to_blockdiag(s):
    b, h, n, _ = s.shape
    g = h // HEADS_PER_GROUP
    s = s.reshape(b, g, HEADS_PER_GROUP, n, n)
    row_blocks = [jnp.pad(s[:, :, i], ((0, 0), (0, 0), (0, 0), (i * n, MXU_DIM - (i + 1) * n)))
                  for i in range(HEADS_PER_GROUP)]
    return jnp.concatenate(row_blocks, axis=2)


def _wkv_state_from_blockdiag(s_bd):
    b, g = s_bd.shape[:2]
    n = RWKV_HEAD
    blocks = [s_bd[:, :, i * n:(i + 1) * n, i * n:(i + 1) * n] for i in range(HEADS_PER_GROUP)]
    return jnp.stack(blocks, axis=2).reshape(b, g * HEADS_PER_GROUP, n, n)


def _gla_kernel(q_ref, k_ref, v_ref, gate_ref, lr_ref, w2_ref, gb_ref, hn_ref, s0_ref, y_ref, sn_ref,
                s_ref):
    tc = q_ref.shape[0]
    n_heads, dv, dk = s_ref.shape
    heads = range(n_heads)
    ksl = [slice(h * dk, (h + 1) * dk) for h in heads]
    vsl = [slice(h * dv, (h + 1) * dv) for h in heads]

    @pl.when(pl.program_id(1) == 0)
    def _():
        for h in heads:
            s_ref[h] = s0_ref[h].T

    tri = _tri_incl(CHUNK)
    tri_b = jnp.where(tri, 1.0, 0.0).astype(BF16)
    scale = dk ** -0.5
    for c in range(tc // CHUNK):
        rows = slice(c * CHUNK, (c + 1) * CHUNK)
        lr = lr_ref[rows, :].astype(BF16)
        gk = [_log_sigmoid(_dot(lr, w2_ref[:, sl]) + gb_ref[:, sl]) * (1.0 / GLA_GATE_NORMALIZER)
              for sl in ksl]
        cum = [_split_dot(tri_b, x, 3) for x in gk]
        last = [x[CHUNK - 1:CHUNK, :] for x in cum]
        qe = [(q_ref[rows, ksl[h]] * scale * jnp.exp(cum[h])).astype(BF16) for h in heads]
        ke = [(k_ref[rows, ksl[h]] * jnp.exp(-cum[h])).astype(BF16) for h in heads]
        k2 = [(k_ref[rows, ksl[h]] * jnp.exp(last[h] - cum[h])).astype(BF16) for h in heads]
        vb = [v_ref[rows, sl].astype(BF16) for sl in vsl]
        scores = [jnp.where(tri, _dot_nt(qe[h], ke[h]), 0.0).astype(BF16) for h in heads]
        s_t = [s_ref[h] for h in heads]
        o = [_dot(scores[h], vb[h]) + _dot_nt(qe[h], s_t[h].astype(BF16)) for h in heads]
        for h in heads:
            s_ref[h] = s_t[h] * jnp.exp(last[h]) + _dot_tn(vb[h], k2[h])
        for h in heads:
            o_n = _rms(o[h], GLA_NORM_EPS) * hn_ref[...]
            y_ref[rows, vsl[h]] = (o_n * _silu(gate_ref[rows, vsl[h]])).astype(y_ref.dtype)

    @pl.when(pl.program_id(1) == pl.num_programs(1) - 1)
    def _():
        for h in heads:
            sn_ref[h] = s_ref[h].T


def _gla(b, qkvg, lr, gk_w2, gk_b, head_norm, s0):
    t = qkvg.shape[0] // b
    h, dk, dv = s0.shape[1:]
    tc = min(t, 4 * CHUNK)
    dk_all, dv_all = h * dk, h * dv
    assert dv_all == 2 * dk_all
    st_spec = pl.BlockSpec((None, h, dk, dv), lambda i, c: (i, 0, 0, 0))
    n_steps = t // tc
    rows = lambda n, col: pl.BlockSpec((tc, n), lambda i, c: (i * n_steps + c, col))
    return pl.pallas_call(
        _gla_kernel,
        grid=(b, t // tc),
        in_specs=[rows(dk_all, 0), rows(dk_all, 1), rows(dv_all, 1), rows(dv_all, 2), rows(LANES, 0),
                  pl.BlockSpec((LANES, dk_all), lambda i, c: (0, 0)),
                  pl.BlockSpec((1, dk_all), lambda i, c: (0, 0)),
                  pl.BlockSpec((1, dv), lambda i, c: (0, 0)),
                  st_spec],
        out_specs=[rows(dv_all, 0), st_spec],
        out_shape=[jax.ShapeDtypeStruct((b * t, dv_all), BF16),
                   jax.ShapeDtypeStruct(s0.shape, F32)],
        scratch_shapes=[pltpu.VMEM((h, dv, dk), F32)],
        compiler_params=_params(("arbitrary", "arbitrary")),
        name="gla",
    )(qkvg, qkvg, qkvg, qkvg, lr, gk_w2, gk_b.reshape(1, -1), head_norm.reshape(1, dv), s0)


def _ffn_kernel(x_ref, g_ref, sv_ref, sg_ref, wv_ref, wg_ref, cwv_ref, cwg_ref, cbv_ref, cbg_ref,
                wd_ref, og_ref, o_ref, nsv_ref, nsg_ref, h_ref, acc_ref, slab_ref, ua_ref, ub_ref, carry_ref,
                *, nj, nseq, seq_len, norm_output):
    rows, d = x_ref.shape
    tn = wv_ref.shape[1]
    j = pl.program_id(2)
    row_tile = pl.program_id(0) * pl.num_programs(1) + pl.program_id(1)
    jt = _hidden_tile(jnp.maximum(j - 1, 0), row_tile, nj)

    n_ph = min(8, seq_len // 16)
    q_len = seq_len // n_ph
    phase_rows = lambda p: pl.ds(p, nseq * q_len, stride=n_ph)
    blk = lambda p: slice(p * q_len, (p + 1) * q_len)
    slabs = [slice(c * LANES, (c + 1) * LANES) for c in range(d // LANES)]

    @pl.when(j == 0)
    def _():
        for c, cl in enumerate(slabs):
            slab_ref[c] = x_ref[:, cl]
        for c, cl in enumerate(slabs):
            for p in range(n_ph):
                acc_ref[:, blk(p), cl] = slab_ref[c, phase_rows(p), :].reshape(nseq, q_len, LANES)
        h_ref[...] = (_rms(acc_ref[...], NORM_EPS) * g_ref[...]).astype(BF16)

    @pl.when((pl.program_id(1) == 0) & (j > 0))
    def _():
        carry_ref[2 * jt] = sv_ref[...]
        carry_ref[2 * jt + 1] = sg_ref[...]

    cols = [slice(c * MXU_DIM, (c + 1) * MXU_DIM) for c in range(tn // MXU_DIM)]
    first = lax.broadcasted_iota(jnp.int32, (1, q_len, 1), 1) == 0

    def conv(u, cs, cw_ref, cb_ref, slot, ns_ref):
        u = u.reshape(nseq, seq_len, MXU_DIM)
        w0, w1, w2, cb = cw_ref[0:1, cs], cw_ref[1:2, cs], cw_ref[2:3, cs], cb_ref[:, cs]
        p2, p1 = carry_ref[slot, :, 0:1, cs], carry_ref[slot, :, 1:2, cs]
        back1 = jnp.where(first, p1, pltpu.roll(u[:, blk(n_ph - 1), :], 1, axis=1))
        back2 = jnp.where(first, p2, pltpu.roll(u[:, blk(n_ph - 2), :], 1, axis=1))
        r1 = jnp.concatenate([back1, u[:, :(n_ph - 1) * q_len, :]], axis=1)
        r2 = jnp.concatenate([back2, back1, u[:, :(n_ph - 2) * q_len, :]], axis=1)
        last = seq_len - 1
        before_last = (n_ph - 1) * q_len - 1
        for row, src in ((0, before_last), (1, last)):
            carry_ref[slot, :, row:row + 1, cs] = u[:, src:src + 1, :]
            ns_ref[:, row:row + 1, cs] = u[:, src:src + 1, :]
        return cb + w0 * r2 + w1 * r1 + w2 * u

    def step(up_ref, dn_ref):
        h = None if up_ref is None else h_ref[...].reshape(rows, d)
        acc = None
        for cs in cols:
            if up_ref is not None:
                up_ref[0, :, cs] = _dot(h, wv_ref[:, cs])
            if dn_ref is not None:
                val = conv(dn_ref[0, :, cs], cs, cwv_ref, cbv_ref, 2 * jt, nsv_ref)
            if up_ref is not None:
                up_ref[1, :, cs] = _dot(h, wg_ref[:, cs])
            if dn_ref is not None:
                gate = conv(dn_ref[1, :, cs], cs, cwg_ref, cbg_ref, 2 * jt + 1, nsg_ref)
                act = (_silu(gate) * val).reshape(rows, MXU_DIM).astype(BF16)
                part = _dot(act, wd_ref[cs, :])
                acc = part if acc is None else acc + part
        if dn_ref is not None:
            acc_ref[...] += acc.reshape(nseq, seq_len, d)

    even = (j & 1) == 0
    inner = (j > 0) & (j < nj)

    @pl.when(j == 0)
    def _():
        step(ua_ref, None)

    @pl.when(inner & even)
    def _():
        step(ua_ref, ub_ref)

    @pl.when(inner & jnp.logical_not(even))
    def _():
        step(ub_ref, ua_ref)

    @pl.when(j == nj)
    def _():
        step(None, ua_ref if (nj - 1) % 2 == 0 else ub_ref)
        if norm_output:
            acc_ref[...] = _rms(acc_ref[...], NORM_EPS) * og_ref[...]
        for c, cl in enumerate(slabs):
            for p in range(n_ph):
                slab_ref[c, phase_rows(p), :] = acc_ref[:, blk(p), cl].reshape(nseq * q_len, LANES)
        for c, cl in enumerate(slabs):
            o_ref[:, cl] = slab_ref[c]


FFN_TILE_COLS = 512


def _hidden_tile(step_tile, row_tile, nj):
    return step_tile + (row_tile % 2) * (nj - 1 - 2 * step_tile)


def _conv_ffn(b, x, conv_state, g, w_up, conv_w, conv_b, w_down, out_g, norm_output, nseq, seq_len):
    m, d = x.shape
    t = m // b
    f = w_down.shape[0]
    tn = FFN_TILE_COLS
    nj = f // tn
    assert b % nseq == 0 and t % seq_len == 0 and f % tn == 0
    assert nseq == 1 or seq_len == t
    x_spec = pl.BlockSpec((nseq * seq_len, d), lambda i, s, j: (i * (t // seq_len) + s, 0))
    n_s = t // seq_len
    up_tile = lambda i, s, j: _hidden_tile(jnp.minimum(j, nj - 1), i * n_s + s, nj)
    dn_tile = lambda i, s, j: _hidden_tile(jnp.maximum(j - 1, 0), i * n_s + s, nj)
    st_v = pl.BlockSpec((nseq, 2, tn), lambda i, s, j: (i, 0, dn_tile(i, s, j)))
    st_g = pl.BlockSpec((nseq, 2, tn), lambda i, s, j: (i, 0, nj + dn_tile(i, s, j)))
    tail_spec = pl.BlockSpec((nseq, None, 2, tn), lambda i, s, j: (i, s, 0, dn_tile(i, s, j)))
    col_v = lambda n: pl.BlockSpec((n, tn), lambda i, s, j: (0, dn_tile(i, s, j)))
    col_g = lambda n: pl.BlockSpec((n, tn), lambda i, s, j: (0, nj + dn_tile(i, s, j)))
    y, ns_v, ns_g = pl.pallas_call(
        functools.partial(_ffn_kernel, nj=nj, nseq=nseq, seq_len=seq_len, norm_output=norm_output),
        grid=(b // nseq, n_s, nj + 1),
        in_specs=[x_spec, pl.BlockSpec((1, d), lambda i, s, j: (0, 0)), st_v, st_g,
                  pl.BlockSpec((d, tn), lambda i, s, j: (0, up_tile(i, s, j))),
                  pl.BlockSpec((d, tn), lambda i, s, j: (0, nj + up_tile(i, s, j))),
                  col_v(3), col_g(3), col_v(1), col_g(1),
                  pl.BlockSpec((tn, d), lambda i, s, j: (dn_tile(i, s, j), 0)),
                  pl.BlockSpec((1, d), lambda i, s, j: (0, 0))],
        out_specs=[x_spec, tail_spec, tail_spec],
        out_shape=[jax.ShapeDtypeStruct((m, d), F32),
                   jax.ShapeDtypeStruct((b, n_s, 2, f), F32),
                   jax.ShapeDtypeStruct((b, n_s, 2, f), F32)],
        scratch_shapes=[pltpu.VMEM((nseq, seq_len, d), BF16),
                        pltpu.VMEM((nseq, seq_len, d), F32),
                        pltpu.VMEM((d // LANES, nseq * seq_len, LANES), F32),
                        pltpu.VMEM((2, nseq * seq_len, tn), F32),
                        pltpu.VMEM((2, nseq * seq_len, tn), F32),
                        pltpu.VMEM((2 * nj, nseq, 2, tn), F32)],
        compiler_params=_params(("arbitrary", "arbitrary", "arbitrary")),
        name="conv_ffn",
    )(x, g.reshape(1, d), conv_state, conv_state, w_up, w_up, conv_w, conv_w,
      conv_b.reshape(1, -1), conv_b.reshape(1, -1), w_down, out_g.reshape(1, d))
    return y, jnp.concatenate([ns_v[:, -1], ns_g[:, -1]], axis=-1)


def _pad_cols(w, n):
    return jnp.pad(w, ((0, 0), (0, n - w.shape[1])))


def _pad_rows(w, n):
    return jnp.pad(w, ((0, n - w.shape[0]), (0, 0)))


def _rwkv_layer(b, x, shift_st, wkv_st, v_first, p):
    pre, new_shift = _rwkv_pre(b, x, shift_st, p["norm"], p["mix"], p["w1"], p["a1"], p["g1"], p["v1"])
    xr, xk, xv = pre[:3]
    r = _matmul(xr, p["wr"], name="rwkv_r")
    k = _matmul(xk, p["wk"], name="rwkv_k")
    v = _matmul(xv, p["wv"], name="rwkv_v")
    if p["v1"] is None:
        vres, v_first = None, v
    else:
        vres = (v_first, pre[6], p["v2"], p["v0"])
    y, s_bd = _wkv7(b, r, k, v, pre[3:6], (p["w2"], p["a2"], p["g2"]), (p["w0"], p["a0"]), vres,
                    p["k_k"], p["k_a"], p["r_k"], p["lnx_w"], p["lnx_b"],
                    _wkv_state_to_blockdiag(wkv_st))
    x_new = _matmul(y, p["wo"], extras=(x,), epilogue=lambda acc, res: res + acc, name="rwkv_o")
    return x_new, new_shift, _wkv_state_from_blockdiag(s_bd), v_first


def _gla_layer(b, x, gla_st, p):
    h = _rmsnorm(x, p["norm"], BF16)
    qkvg = _matmul(h, p["w_in"], n_cols=p["n_main"], name="gla_in")
    lr = _matmul(h, p["w_lr"], name="gla_lr")
    y, s_new = _gla(b, qkvg, lr, p["gk_w2"], p["gk_b"], p["head_norm"], gla_st)
    x_new = _matmul(y, p["wo"], extras=(x,), epilogue=lambda acc, res: res + acc, name="gla_o")
    return x_new, s_new


FFN_TILE_ROWS = 512


def _ffn_tile(b, t):
    if t >= FFN_TILE_ROWS:
        return 1, FFN_TILE_ROWS
    return min(b, FFN_TILE_ROWS // t), t


def _run_trunk(x, shift_st, wkv_st, gla_st, conv_st, layers, norm_final, ffn_nseq, ffn_len):
    new_shift, new_wkv, new_gla, new_conv = [], [], [], []
    v_first = None
    b, t, d = x.shape
    x = x.reshape(b * t, d)
    for i, (mixer, ffn) in enumerate(layers):
        j = i // 2
        if i % 2 == 0:
            x, s_shift, s_wkv, v_first = _rwkv_layer(b, x, shift_st[j], wkv_st[j], v_first, mixer)
            new_shift.append(s_shift)
            new_wkv.append(s_wkv)
        else:
            x, s_gla = _gla_layer(b, x, gla_st[j], mixer)
            new_gla.append(s_gla)
        x, s_conv = _conv_ffn(b, x, conv_st[i], ffn["norm"], ffn["w_up"], ffn["conv_w"], ffn["conv_b"],
                              ffn["w_down"], norm_final, i == len(layers) - 1, ffn_nseq, ffn_len)
        new_conv.append(s_conv)
    return x.reshape(b, t, d), jnp.stack(new_shift), jnp.stack(new_wkv), jnp.stack(new_gla), jnp.stack(new_conv)


def kernel(x_prompt, x_sample, state_rwkv_shift, state_rwkv_wkv, state_gla, state_ffn_conv, norm_mix, norm_ffn, norm_final, rwkv_mix, rwkv_w0, rwkv_w1, rwkv_w2, rwkv_a0, rwkv_a1, rwkv_a2, rwkv_v0, rwkv_v1, rwkv_v2, rwkv_g1, rwkv_g2, rwkv_k_k, rwkv_k_a, rwkv_r_k, rwkv_wr, rwkv_wk, rwkv_wv, rwkv_wo, rwkv_lnx_w, rwkv_lnx_b, gla_w_in, gla_gk_w2, gla_gk_b, gla_head_norm, gla_wo, ffn_w_up, ffn_conv_w, ffn_conv_b, ffn_w_down):
    depth = norm_mix.shape[0]
    d = x_prompt.shape[-1]
    bf = lambda w: w.astype(BF16)
    n_main = gla_w_in.shape[-1] - GLA_GATE_RANK
    layers = []
    for i in range(depth):
        j = i // 2
        if i % 2 == 0:
            has_vres = j > 0
            mixer = dict(
                norm=norm_mix[i], mix=rwkv_mix[j],
                wr=bf(rwkv_wr[j]), wk=bf(rwkv_wk[j]), wv=bf(rwkv_wv[j]), wo=bf(rwkv_wo[j]),
                w0=rwkv_w0[j], w1=bf(_pad_cols(rwkv_w1[j], LANES)), w2=bf(_pad_rows(rwkv_w2[j], LANES)),
                a0=rwkv_a0[j], a1=bf(_pad_cols(rwkv_a1[j], LANES)), a2=bf(_pad_rows(rwkv_a2[j], LANES)),
                g1=bf(rwkv_g1[j]), g2=bf(rwkv_g2[j]),
                v0=rwkv_v0[j - 1] if has_vres else None,
                v1=bf(_pad_cols(rwkv_v1[j - 1], LANES)) if has_vres else None,
                v2=bf(_pad_rows(rwkv_v2[j - 1], LANES)) if has_vres else None,
                k_k=rwkv_k_k[j], k_a=rwkv_k_a[j], r_k=rwkv_r_k[j].reshape(d),
                lnx_w=rwkv_lnx_w[j], lnx_b=rwkv_lnx_b[j])
        else:
            mixer = dict(
                norm=norm_mix[i],
                w_in=bf(gla_w_in[j]), n_main=n_main,
                w_lr=bf(_pad_cols(gla_w_in[j][:, n_main:], LANES)),
                gk_w2=bf(_pad_rows(gla_gk_w2[j], LANES)), gk_b=gla_gk_b[j],
                head_norm=gla_head_norm[j], wo=bf(gla_wo[j]))
        ffn = dict(norm=norm_ffn[i], w_up=bf(ffn_w_up[i]), conv_w=ffn_conv_w[i],
                   conv_b=ffn_conv_b[i],
                   w_down=bf(ffn_w_down[i]))
        layers.append((mixer, ffn))

    bp, tp, _ = x_prompt.shape
    bs, ts, _ = x_sample.shape
    zeros_like_state = lambda s: jnp.zeros((s.shape[0], bp) + s.shape[2:], s.dtype)
    out_p = _run_trunk(x_prompt, zeros_like_state(state_rwkv_shift), zeros_like_state(state_rwkv_wkv),
                       zeros_like_state(state_gla), zeros_like_state(state_ffn_conv), layers,
                       norm_final, *_ffn_tile(bp, tp))
    out_s = _run_trunk(x_sample, state_rwkv_shift, state_rwkv_wkv, state_gla, state_ffn_conv, layers,
                       norm_final, *_ffn_tile(bs, ts))
    return (out_p[0], out_s[0]) + tuple(out_p[1:]) + tuple(out_s[1:])
```

```python
import functools

import jax
import jax.numpy as jnp
from jax import lax
from jax.experimental import pallas as pl
from jax.experimental.pallas import tpu as pltpu

F32 = jnp.float32
BF16 = jnp.bfloat16

CHUNK = 64
RWKV_HEAD = 64
RWKV_GN_EPS = 64e-5
GLA_GATE_RANK = 16
GLA_GATE_NORMALIZER = 16.0
GLA_NORM_EPS = 1e-5
NORM_EPS = 1e-6

LANES = 128
MXU_DIM = 256
HEADS_PER_GROUP = MXU_DIM // RWKV_HEAD
VMEM_LIMIT = 56 * 1024 * 1024


def _params(sem):
    return pltpu.CompilerParams(dimension_semantics=sem, vmem_limit_bytes=VMEM_LIMIT)


def _dot(a, b):
    return jnp.dot(a, b, preferred_element_type=F32)


def _dot_nt(a, b):
    return lax.dot_general(a, b, (((1,), (1,)), ((), ())), preferred_element_type=F32)


def _dot_tn(a, b):
    return lax.dot_general(a, b, (((0,), (0,)), ((), ())), preferred_element_type=F32)


def _split_dot(fixed_bf16, x, terms):
    acc = None
    rem = x
    for _ in range(terms):
        piece = rem.astype(BF16)
        rem = rem - piece.astype(F32)
        part = _dot(fixed_bf16, piece)
        acc = part if acc is None else acc + part
    return acc


def _log_sigmoid(x):
    return jnp.minimum(x, 0.0) - jnp.log(1.0 + jnp.exp(-jnp.abs(x)))


def _sigmoid(x):
    return 1.0 / (1.0 + jnp.exp(-x))


def _silu(x):
    return x * _sigmoid(x)


def _rms(x, eps):
    return x * lax.rsqrt(jnp.mean(x * x, axis=-1, keepdims=True) + eps)


def _tri_incl(n):
    r = lax.broadcasted_iota(jnp.int32, (n, n), 0)
    c = lax.broadcasted_iota(jnp.int32, (n, n), 1)
    return c <= r


def _rmsnorm_kernel(x_ref, g_ref, o_ref):
    o_ref[...] = (_rms(x_ref[...], NORM_EPS) * g_ref[...]).astype(o_ref.dtype)


def _rmsnorm(x, g, out_dtype):
    m, d = x.shape
    tm = min(m, 1024)
    return pl.pallas_call(
        _rmsnorm_kernel,
        grid=(m // tm,),
        in_specs=[pl.BlockSpec((tm, d), lambda i: (i, 0)), pl.BlockSpec((1, d), lambda i: (0, 0))],
        out_specs=pl.BlockSpec((tm, d), lambda i: (i, 0)),
        out_shape=jax.ShapeDtypeStruct((m, d), out_dtype),
        compiler_params=_params(("arbitrary",)),
        name="rmsnorm",
    )(x, g.reshape(1, d))


def _mm_kernel(a_ref, b_ref, *rest, epilogue, n_extra):
    o_ref = rest[n_extra]
    acc = _dot(a_ref[...], b_ref[...])
    if epilogue is not None:
        acc = epilogue(acc, *[e[...] for e in rest[:n_extra]])
    o_ref[...] = acc.astype(o_ref.dtype)


def _matmul(a, b, extras=(), epilogue=None, out_dtype=F32, tm=1024, tn=1024, n_cols=None,
            name="matmul"):
    m, k = a.shape
    n = b.shape[1] if n_cols is None else n_cols
    tm, tn = min(tm, m), min(tn, n)
    assert m % tm == 0 and n % tn == 0, (m, n, tm, tn)
    in_specs = [pl.BlockSpec((tm, k), lambda i, j: (i, 0)), pl.BlockSpec((k, tn), lambda i, j: (0, j))]
    for e in extras:
        if e.shape[0] == 1:
            in_specs.append(pl.BlockSpec((1, tn), lambda i, j: (0, j)))
        else:
            in_specs.append(pl.BlockSpec((tm, tn), lambda i, j: (i, j)))
    return pl.pallas_call(
        functools.partial(_mm_kernel, epilogue=epilogue, n_extra=len(extras)),
        grid=(m // tm, n // tn),
        in_specs=in_specs,
        out_specs=pl.BlockSpec((tm, tn), lambda i, j: (i, j)),
        out_shape=jax.ShapeDtypeStruct((m, n), out_dtype),
        compiler_params=_params(("arbitrary", "arbitrary")),
        name=name,
    )(a, b, *extras)


def _rwkv_pre_kernel(x_ref, st_ref, g_ref, mix_ref, w1_ref, a1_ref, g1_ref, *rest, has_v1):
    if has_v1:
        v1_ref, rest = rest[0], rest[1:]
    xr_ref, xk_ref, xv_ref, lw_ref, la_ref, lg_ref = rest[:6]
    rest = rest[6:]
    if has_v1:
        lv_ref, rest = rest[0], rest[1:]
    hl_ref, carry_ref = rest
    tt = x_ref.shape[0]

    @pl.when(pl.program_id(1) == 0)
    def _():
        carry_ref[...] = st_ref[...]

    h = _rms(x_ref[...], NORM_EPS) * g_ref[...]
    row = lax.broadcasted_iota(jnp.int32, (tt, 1), 0)
    h_prev = jnp.where(row == 0, carry_ref[...], pltpu.roll(h, 1, axis=0))
    carry_ref[...] = h[tt - 1:tt, :]
    hl_ref[...] = h[tt - 8:, :]
    xx = h_prev - h
    mixed = lambda i: (h + xx * mix_ref[i:i + 1, :]).astype(BF16)
    xr_ref[...] = mixed(0)
    xk_ref[...] = mixed(2)
    xv = mixed(3)
    xv_ref[...] = xv
    lw_ref[...] = jnp.tanh(_dot(mixed(1), w1_ref[...])).astype(BF16)
    la_ref[...] = _dot(mixed(4), a1_ref[...]).astype(BF16)
    lg_ref[...] = _sigmoid(_dot(mixed(5), g1_ref[...])).astype(BF16)
    if has_v1:
        lv_ref[...] = _dot(xv, v1_ref[...]).astype(BF16)


def _rwkv_pre(b, x, shift_state, g, mix, w1, a1, g1, v1):
    m, d = x.shape
    t = m // b
    tt = min(t, 512)
    nt = t // tt
    has_v1 = v1 is not None
    lora = [w1, a1, g1] + ([v1] if has_v1 else [])
    flat_spec = lambda n: pl.BlockSpec((tt, n), lambda i, j: (i * nt + j, 0))
    full_spec = lambda w: pl.BlockSpec(w.shape, lambda i, j: (0, 0))
    wide = [jax.ShapeDtypeStruct((b * t, d), BF16)] * 3
    narrow = [jax.ShapeDtypeStruct((b * t, w.shape[1]), BF16) for w in lora]
    outs = pl.pallas_call(
        functools.partial(_rwkv_pre_kernel, has_v1=has_v1),
        grid=(b, nt),
        in_specs=[flat_spec(d),
                  pl.BlockSpec((None, 1, d), lambda i, j: (i, 0, 0)),
                  pl.BlockSpec((1, d), lambda i, j: (0, 0)),
                  pl.BlockSpec((6, d), lambda i, j: (0, 0))] + [full_spec(w) for w in lora],
        out_specs=[flat_spec(d)] * 3 + [flat_spec(w.shape[1]) for w in lora]
                  + [pl.BlockSpec((None, 8, d), lambda i, j: (i, 0, 0))],
        out_shape=wide + narrow + [jax.ShapeDtypeStruct((b, 8, d), F32)],
        scratch_shapes=[pltpu.VMEM((1, d), F32)],
        compiler_params=_params(("arbitrary", "arbitrary")),
        name="rwkv_pre",
    )(x, shift_state.reshape(b, 1, d), g.reshape(1, d), mix, *lora)
    return outs[:-1], outs[-1][:, 7, :]


def _wkv7_kernel(*refs, has_vres):
    refs = list(refs)
    r_ref, k_ref, v_ref, lw_ref, la_ref, lg_ref = refs[:6]
    w2_ref, a2_ref, g2_ref, w0_ref, a0_ref = refs[6:11]
    refs = refs[11:]
    if has_vres:
        vf_ref, lv_ref, v2_ref, v0_ref = refs[:4]
        refs = refs[4:]
    kk_ref, ka_ref, rk_ref, gnw_ref, gnb_ref, s0_ref, y_ref, s_ref = refs
    c_len = CHUNK
    n_sub, d = r_ref.shape[0] // c_len, r_ref.shape[1]
    n_groups = d // MXU_DIM

    @pl.when(pl.program_id(1) == 0)
    def _():
        s_ref[...] = s0_ref[...]

    ri = lax.broadcasted_iota(jnp.int32, (MXU_DIM, MXU_DIM), 0)
    ci = lax.broadcasted_iota(jnp.int32, (MXU_DIM, MXU_DIM), 1)
    head_bits = RWKV_HEAD.bit_length() - 1
    same_head = (ri >> head_bits) == (ci >> head_bits)
    bd = jnp.where(same_head, 1.0, 0.0)
    bd_b = bd.astype(BF16)
    tri_b = jnp.where(_tri_incl(c_len), 1.0, 0.0).astype(BF16)

    assert c_len == RWKV_HEAD and 2 * RWKV_HEAD == LANES
    t_c = lax.broadcasted_iota(jnp.int32, (MXU_DIM, LANES), 0) & (c_len - 1)
    lane_c = lax.broadcasted_iota(jnp.int32, (MXU_DIM, LANES), 1)
    s_c = lane_c & (RWKV_HEAD - 1)
    low_half = lane_c < RWKV_HEAD
    strict_c = s_c < t_c
    incl_c = s_c <= t_c
    eye_c = jnp.where(s_c == t_c, 1.0, 0.0)
    lane_t = lax.broadcasted_iota(jnp.int32, (c_len, LANES), 1)
    half_masks = (lane_t < RWKV_HEAD, lane_t >= RWKV_HEAD)

    def stack(x):
        return jnp.concatenate([x] * HEADS_PER_GROUP, axis=0)

    def stack_masked(x):
        return stack(x) * bd

    def head_sum(x):
        return _dot(x.astype(BF16), bd_b)

    def expand(xc, half):
        zeros = jnp.zeros((c_len, LANES), F32)
        blocks = []
        for hd in range(HEADS_PER_GROUP):
            src = xc[hd * c_len:(hd + 1) * c_len]
            if hd % 2 != half:
                src = pltpu.roll(src, RWKV_HEAD, axis=1)
            tile = jnp.where(half_masks[hd % 2], src, 0.0)
            blocks.append(jnp.concatenate(
                [tile if lt == hd // 2 else zeros for lt in range(MXU_DIM // LANES)], axis=1))
        return jnp.concatenate(blocks, axis=0)

    groups = range(n_groups)
    sls = [slice(grp * MXU_DIM, (grp + 1) * MXU_DIM) for grp in groups]
    lw_n, la_n, lg_n = lw_ref[...], la_ref[...], lg_ref[...]
    w_all = [_log_sigmoid(w0_ref[:, sl] + _dot(lw_n, w2_ref[:, sl])) - 0.5 for sl in sls]
    a_all = [_sigmoid(a0_ref[:, sl] + _dot(la_n, a2_ref[:, sl])) for sl in sls]
    g_all = [_dot(lg_n, g2_ref[:, sl]) for sl in sls]
    v_all = [v_ref[:, sl] for sl in sls]
    if has_vres:
        lv_n = lv_ref[...]
        mix_v = [_sigmoid(v0_ref[:, sl] + _dot(lv_n, v2_ref[:, sl])) for sl in sls]
        v_all = [v_all[i] + (vf_ref[:, sls[i]] - v_all[i]) * mix_v[i] for i in groups]
    chains = [(c, grp) for c in range(n_sub) for grp in groups]
    rws = [slice(c * c_len, (c + 1) * c_len) for c, _ in chains]
    sls = [sls[grp] for _, grp in chains]
    groups = range(len(chains))
    r = [r_ref[rws[i], sls[i]] for i in groups]
    k = [k_ref[rws[i], sls[i]] for i in groups]
    v = [v_all[chains[i][1]][rws[i]] for i in groups]
    a = [a_all[chains[i][1]][rws[i]] for i in groups]
    g = [g_all[chains[i][1]][rws[i]] for i in groups]
    lw = [-jnp.exp(w_all[chains[i][1]][rws[i]]) for i in groups]
    kk = [k[i] * kk_ref[:, sls[i]] for i in groups]
    kk_ss = [head_sum(x * x) for x in kk]
    cum = [_split_dot(tri_b, x, 2) for x in lw]
    kk = [kk[i] / jnp.maximum(jnp.sqrt(kk_ss[i]), 1e-12) for i in groups]
    kmod = [k[i] * (1.0 + (a[i] - 1.0) * ka_ref[:, sls[i]]) for i in groups]
    kka = [kk[i] * a[i] for i in groups]
    last = [x[c_len - 1:c_len, :] for x in cum]
    e_neg = [jnp.exp(-x) for x in cum]
    ar = [jnp.concatenate([stack_masked(-kk[i] * jnp.exp(cum[i] - lw[i])),
                           stack_masked(r[i] * jnp.exp(cum[i]))], axis=0).astype(BF16) for i in groups]
    bk = [jnp.concatenate([kka[i] * e_neg[i], kmod[i] * e_neg[i]], axis=0).astype(BF16) for i in groups]
    z = [_dot_nt(ar[i], bk[i]) for i in groups]
    za = [jnp.where(strict_c, x[:MXU_DIM], 0.0) for x in z]
    zr = [jnp.where(incl_c, x[MXU_DIM:], 0.0) for x in z]
    a_ak = [expand(x, 1).astype(BF16) for x in za]
    a_r = [jnp.concatenate([expand(x, 0), expand(x, 1)], axis=1).astype(BF16) for x in zr]

    rhs = [jnp.where(low_half, x, eye_c) for x in za]
    p_bd = [expand(x, 0).astype(BF16) for x in za]
    n_steps = (c_len - 1).bit_length()
    for step in range(n_steps):
        x2 = [_dot(p_bd[i], rhs[i].astype(BF16)) for i in groups]
        rhs = [x2[i] + jnp.where(low_half, 0.0, rhs[i]) for i in groups]
        if step + 1 < n_steps:
            p_bd = [expand(x, 0).astype(BF16) for x in x2]
    tinv = [expand(x, 1) for x in rhs]

    v_st = [stack_masked(x) for x in v]
    av = [_dot(a_ak[i], v_st[i].astype(BF16)) for i in groups]
    e_last = [jnp.exp(last[i] - cum[i]) for i in groups]
    bk_hat = [jnp.concatenate([stack_masked(kka[i] * e_last[i]), stack_masked(kmod[i] * e_last[i])],
                              axis=0).astype(BF16) for i in groups]
    tinv_b = [x.astype(BF16) for x in tinv]

    o_st = [None] * len(chains)
    for c in range(n_sub):
        cur = [i for i in groups if chains[i][0] == c]
        s = {i: s_ref[chains[i][1]] for i in cur}
        y = {i: _dot_nt(ar[i], s[i].astype(BF16)) for i in cur}
        u = {i: _dot(tinv_b[i], (y[i][:MXU_DIM] + av[i]).astype(BF16)) for i in cur}
        uv = {i: jnp.concatenate([u[i], v_st[i]], axis=0).astype(BF16) for i in cur}
        for i in cur:
            o_st[i] = y[i][MXU_DIM:] + _dot(a_r[i], uv[i])
        for i in cur:
            s_ref[chains[i][1]] = s[i] * jnp.exp(last[i]) + _dot_tn(uv[i], bk_hat[i])

    o = [functools.reduce(lambda p, q: p + q,
                          [x[hd * c_len:(hd + 1) * c_len] for hd in range(HEADS_PER_GROUP)]) for x in o_st]
    inv_n = 1.0 / RWKV_HEAD
    mu = [head_sum(x) * inv_n for x in o]
    bonus_dot = [head_sum(r[i] * kmod[i] * rk_ref[:, sls[i]]) for i in groups]
    dev = [o[i] - mu[i] for i in groups]
    var = [head_sum(x * x) * inv_n for x in dev]
    for i in groups:
        sl = sls[i]
        o_n = dev[i] * lax.rsqrt(var[i] + RWKV_GN_EPS) * gnw_ref[:, sl] + gnb_ref[:, sl]
        y_ref[rws[i], sl] = ((o_n + bonus_dot[i] * v[i]) * g[i]).astype(y_ref.dtype)


WKV_CHUNKS_PER_STEP = 2


def _wkv7(b, r, k, v, lora, lora_w, lora_b, vres, k_k, k_a, r_k, lnx_w, lnx_b, s0_bd):
    m, d = r.shape
    t = m // b
    blk_rows = min(t, WKV_CHUNKS_PER_STEP * CHUNK)
    n_steps = t // blk_rows
    n_groups = d // MXU_DIM
    rows = lambda n: pl.BlockSpec((blk_rows, n), lambda i, j: (i * n_steps + j, 0))
    whole = lambda w: pl.BlockSpec(w.shape, lambda i, j: (0, 0))
    vec_spec = pl.BlockSpec((1, d), lambda i, j: (0, 0))
    st_spec = pl.BlockSpec((None, n_groups, MXU_DIM, MXU_DIM), lambda i, j: (i, 0, 0, 0))
    vec = lambda p: p.reshape(1, d)
    args = [r, k, v, *lora, *lora_w, *[vec(x) for x in lora_b]]
    specs = ([rows(d)] * 3 + [rows(x.shape[1]) for x in lora] + [whole(w) for w in lora_w]
             + [vec_spec] * len(lora_b))
    if vres is not None:
        v_first, lv, v2, v0 = vres
        args += [v_first, lv, v2, vec(v0)]
        specs += [rows(d), rows(lv.shape[1]), whole(v2), vec_spec]
    args += [vec(k_k), vec(k_a), vec(r_k), vec(lnx_w), vec(lnx_b), s0_bd]
    specs += [vec_spec] * 5 + [st_spec]
    return pl.pallas_call(
        functools.partial(_wkv7_kernel, has_vres=vres is not None),
        grid=(b, n_steps),
        in_specs=specs,
        out_specs=[rows(d), st_spec],
        out_shape=[jax.ShapeDtypeStruct((m, d), BF16),
                   jax.ShapeDtypeStruct(s0_bd.shape, F32)],
        compiler_params=_params(("arbitrary", "arbitrary")),
        name="wkv7",
    )(*args)


def _wkv_state_to_blockdiag(s):
    b, h, n, _ = s.shape
    g = h // HEADS_PER_GROUP
    s = s.reshape(b, g, HEADS_PER_GROUP, n, n)
    row_blocks = [jnp.pad(s[:, :, i], ((0, 0), (0, 0), (0, 0), (i * n, MXU_DIM - (i + 1) * n)))
                  for i in range(HEADS_PER_GROUP)]
    return jnp.concatenate(row_blocks, axis=2)


def _wkv_state_from_blockdiag(s_bd):
    b, g = s_bd.shape[:2]
    n = RWKV_HEAD
    blocks = [s_bd[:, :, i * n:(i + 1) * n, i * n:(i + 1) * n] for i in range(HEADS_PER_GROUP)]
    return jnp.stack(blocks, axis=2).reshape(b, g * HEADS_PER_GROUP, n, n)


def _gla_kernel(q_ref, k_ref, v_ref, gate_ref, lr_ref, w2_ref, gb_ref, hn_ref, s0_ref, y_ref, sn_ref,
                s_ref):
    tc = q_ref.shape[0]
    n_heads, dv, dk = s_ref.shape
    heads = range(n_heads)
    ksl = [slice(h * dk, (h + 1) * dk) for h in heads]
    vsl = [slice(h * dv, (h + 1) * dv) for h in heads]

    @pl.when(pl.program_id(1) == 0)
    def _():
        for h in heads:
            s_ref[h] = s0_ref[h].T

    tri = _tri_incl(CHUNK)
    tri_b = jnp.where(tri, 1.0, 0.0).astype(BF16)
    scale = dk ** -0.5
    for c in range(tc // CHUNK):
        rows = slice(c * CHUNK, (c + 1) * CHUNK)
        lr = lr_ref[rows, :].astype(BF16)
        gk = [_log_sigmoid(_dot(lr, w2_ref[:, sl]) + gb_ref[:, sl]) * (1.0 / GLA_GATE_NORMALIZER)
              for sl in ksl]
        cum = [_split_dot(tri_b, x, 3) for x in gk]
        last = [x[CHUNK - 1:CHUNK, :] for x in cum]
        qe = [(q_ref[rows, ksl[h]] * scale * jnp.exp(cum[h])).astype(BF16) for h in heads]
        ke = [(k_ref[rows, ksl[h]] * jnp.exp(-cum[h])).astype(BF16) for h in heads]
        k2 = [(k_ref[rows, ksl[h]] * jnp.exp(last[h] - cum[h])).astype(BF16) for h in heads]
        vb = [v_ref[rows, sl].astype(BF16) for sl in vsl]
        scores = [jnp.where(tri, _dot_nt(qe[h], ke[h]), 0.0).astype(BF16) for h in heads]
        s_t = [s_ref[h] for h in heads]
        o = [_dot(scores[h], vb[h]) + _dot_nt(qe[h], s_t[h].astype(BF16)) for h in heads]
        for h in heads:
            s_ref[h] = s_t[h] * jnp.exp(last[h]) + _dot_tn(vb[h], k2[h])
        for h in heads:
            o_n = _rms(o[h], GLA_NORM_EPS) * hn_ref[...]
            y_ref[rows, vsl[h]] = (o_n * _silu(gate_ref[rows, vsl[h]])).astype(y_ref.dtype)

    @pl.when(pl.program_id(1) == pl.num_programs(1) - 1)
    def _():
        for h in heads:
            sn_ref[h] = s_ref[h].T


def _gla(b, qkvg, lr, gk_w2, gk_b, head_norm, s0):
    t = qkvg.shape[0] // b
    h, dk, dv = s0.shape[1:]
    tc = min(t, 4 * CHUNK)
    dk_all, dv_all = h * dk, h * dv
    assert dv_all == 2 * dk_all
    st_spec = pl.BlockSpec((None, h, dk, dv), lambda i, c: (i, 0, 0, 0))
    n_steps = t // tc
    rows = lambda n, col: pl.BlockSpec((tc, n), lambda i, c: (i * n_steps + c, col))
    return pl.pallas_call(
        _gla_kernel,
        grid=(b, t // tc),
        in_specs=[rows(dk_all, 0), rows(dk_all, 1), rows(dv_all, 1), rows(dv_all, 2), rows(LANES, 0),
                  pl.BlockSpec((LANES, dk_all), lambda i, c: (0, 0)),
                  pl.BlockSpec((1, dk_all), lambda i, c: (0, 0)),
                  pl.BlockSpec((1, dv), lambda i, c: (0, 0)),
                  st_spec],
        out_specs=[rows(dv_all, 0), st_spec],
        out_shape=[jax.ShapeDtypeStruct((b * t, dv_all), BF16),
                   jax.ShapeDtypeStruct(s0.shape, F32)],
        scratch_shapes=[pltpu.VMEM((h, dv, dk), F32)],
        compiler_params=_params(("arbitrary", "arbitrary")),
        name="gla",
    )(qkvg, qkvg, qkvg, qkvg, lr, gk_w2, gk_b.reshape(1, -1), head_norm.reshape(1, dv), s0)


def _ffn_kernel(x_ref, g_ref, sv_ref, sg_ref, wv_ref, wg_ref, cwv_ref, cwg_ref, cbv_ref, cbg_ref,
                wd_ref, og_ref, o_ref, nsv_ref, nsg_ref, h_ref, acc_ref, slab_ref, ua_ref, ub_ref, carry_ref,
                *, nj, nseq, seq_len, norm_output):
    rows, d = x_ref.shape
    tn = wv_ref.shape[1]
    j = pl.program_id(2)
    row_tile = pl.program_id(0) * pl.num_programs(1) + pl.program_id(1)
    jt = _hidden_tile(jnp.maximum(j - 1, 0), row_tile, nj)

    n_ph = min(8, seq_len // 16)
    q_len = seq_len // n_ph
    phase_rows = lambda p: pl.ds(p, nseq * q_len, stride=n_ph)
    blk = lambda p: slice(p * q_len, (p + 1) * q_len)
    slabs = [slice(c * LANES, (c + 1) * LANES) for c in range(d // LANES)]

    @pl.when(j == 0)
    def _():
        for c, cl in enumerate(slabs):
            slab_ref[c] = x_ref[:, cl]
        for c, cl in enumerate(slabs):
            for p in range(n_ph):
                acc_ref[:, blk(p), cl] = slab_ref[c, phase_rows(p), :].reshape(nseq, q_len, LANES)
        h_ref[...] = (_rms(acc_ref[...], NORM_EPS) * g_ref[...]).astype(BF16)

    @pl.when((pl.program_id(1) == 0) & (j > 0))
    def _():
        carry_ref[2 * jt] = sv_ref[...]
        carry_ref[2 * jt + 1] = sg_ref[...]

    cols = [slice(c * MXU_DIM, (c + 1) * MXU_DIM) for c in range(tn // MXU_DIM)]
    first = lax.broadcasted_iota(jnp.int32, (1, q_len, 1), 1) == 0

    def conv(u, cs, cw_ref, cb_ref, slot, ns_ref):
        u = u.reshape(nseq, seq_len, MXU_DIM)
        w0, w1, w2, cb = cw_ref[0:1, cs], cw_ref[1:2, cs], cw_ref[2:3, cs], cb_ref[:, cs]
        p2, p1 = carry_ref[slot, :, 0:1, cs], carry_ref[slot, :, 1:2, cs]
        back1 = jnp.where(first, p1, pltpu.roll(u[:, blk(n_ph - 1), :], 1, axis=1))
        back2 = jnp.where(first, p2, pltpu.roll(u[:, blk(n_ph - 2), :], 1, axis=1))
        r1 = jnp.concatenate([back1, u[:, :(n_ph - 1) * q_len, :]], axis=1)
        r2 = jnp.concatenate([back2, back1, u[:, :(n_ph - 2) * q_len, :]], axis=1)
        last = seq_len - 1
        before_last = (n_ph - 1) * q_len - 1
        for row, src in ((0, before_last), (1, last)):
            carry_ref[slot, :, row:row + 1, cs] = u[:, src:src + 1, :]
            ns_ref[:, row:row + 1, cs] = u[:, src:src + 1, :]
        return cb + w0 * r2 + w1 * r1 + w2 * u

    def step(up_ref, dn_ref):
        h = None if up_ref is None else h_ref[...].reshape(rows, d)
        acc = None
        for cs in cols:
            if up_ref is not None:
                up_ref[0, :, cs] = _dot(h, wv_ref[:, cs])
            if dn_ref is not None:
                val = conv(dn_ref[0, :, cs], cs, cwv_ref, cbv_ref, 2 * jt, nsv_ref)
            if up_ref is not None:
                up_ref[1, :, cs] = _dot(h, wg_ref[:, cs])
            if dn_ref is not None:
                gate = conv(dn_ref[1, :, cs], cs, cwg_ref, cbg_ref, 2 * jt + 1, nsg_ref)
                act = (_silu(gate) * val).reshape(rows, MXU_DIM).astype(BF16)
                part = _dot(act, wd_ref[cs, :])
                acc = part if acc is None else acc + part
        if dn_ref is not None:
            acc_ref[...] += acc.reshape(nseq, seq_len, d)

    even = (j & 1) == 0
    inner = (j > 0) & (j < nj)

    @pl.when(j == 0)
    def _():
        step(ua_ref, None)

    @pl.when(inner & even)
    def _():
        step(ua_ref, ub_ref)

    @pl.when(inner & jnp.logical_not(even))
    def _():
        step(ub_ref, ua_ref)

    @pl.when(j == nj)
    def _():
        step(None, ua_ref if (nj - 1) % 2 == 0 else ub_ref)
        if norm_output:
            acc_ref[...] = _rms(acc_ref[...], NORM_EPS) * og_ref[...]
        for c, cl in enumerate(slabs):
            for p in range(n_ph):
                slab_ref[c, phase_rows(p), :] = acc_ref[:, blk(p), cl].reshape(nseq * q_len, LANES)
        for c, cl in enumerate(slabs):
            o_ref[:, cl] = slab_ref[c]


FFN_TILE_COLS = 512


def _hidden_tile(step_tile, row_tile, nj):
    return step_tile + (row_tile % 2) * (nj - 1 - 2 * step_tile)


def _conv_ffn(b, x, conv_state, g, w_up, conv_w, conv_b, w_down, out_g, norm_output, nseq, seq_len):
    m, d = x.shape
    t = m // b
    f = w_down.shape[0]
    tn = FFN_TILE_COLS
    nj = f // tn
    assert b % nseq == 0 and t % seq_len == 0 and f % tn == 0
    assert nseq == 1 or seq_len == t
    x_spec = pl.BlockSpec((nseq * seq_len, d), lambda i, s, j: (i * (t // seq_len) + s, 0))
    n_s = t // seq_len
    up_tile = lambda i, s, j: _hidden_tile(jnp.minimum(j, nj - 1), i * n_s + s, nj)
    dn_tile = lambda i, s, j: _hidden_tile(jnp.maximum(j - 1, 0), i * n_s + s, nj)
    st_v = pl.BlockSpec((nseq, 2, tn), lambda i, s, j: (i, 0, dn_tile(i, s, j)))
    st_g = pl.BlockSpec((nseq, 2, tn), lambda i, s, j: (i, 0, nj + dn_tile(i, s, j)))
    tail_spec = pl.BlockSpec((nseq, None, 2, tn), lambda i, s, j: (i, s, 0, dn_tile(i, s, j)))
    col_v = lambda n: pl.BlockSpec((n, tn), lambda i, s, j: (0, dn_tile(i, s, j)))
    col_g = lambda n: pl.BlockSpec((n, tn), lambda i, s, j: (0, nj + dn_tile(i, s, j)))
    y, ns_v, ns_g = pl.pallas_call(
        functools.partial(_ffn_kernel, nj=nj, nseq=nseq, seq_len=seq_len, norm_output=norm_output),
        grid=(b // nseq, n_s, nj + 1),
        in_specs=[x_spec, pl.BlockSpec((1, d), lambda i, s, j: (0, 0)), st_v, st_g,
                  pl.BlockSpec((d, tn), lambda i, s, j: (0, up_tile(i, s, j))),
                  pl.BlockSpec((d, tn), lambda i, s, j: (0, nj + up_tile(i, s, j))),
                  col_v(3), col_g(3), col_v(1), col_g(1),
                  pl.BlockSpec((tn, d), lambda i, s, j: (dn_tile(i, s, j), 0)),
                  pl.BlockSpec((1, d), lambda i, s, j: (0, 0))],
        out_specs=[x_spec, tail_spec, tail_spec],
        out_shape=[jax.ShapeDtypeStruct((m, d), F32),
                   jax.ShapeDtypeStruct((b, n_s, 2, f), F32),
                   jax.ShapeDtypeStruct((b, n_s, 2, f), F32)],
        scratch_shapes=[pltpu.VMEM((nseq, seq_len, d), BF16),
                        pltpu.VMEM((nseq, seq_len, d), F32),
                        pltpu.VMEM((d // LANES, nseq * seq_len, LANES), F32),
                        pltpu.VMEM((2, nseq * seq_len, tn), F32),
                        pltpu.VMEM((2, nseq * seq_len, tn), F32),
                        pltpu.VMEM((2 * nj, nseq, 2, tn), F32)],
        compiler_params=_params(("arbitrary", "arbitrary", "arbitrary")),
        name="conv_ffn",
    )(x, g.reshape(1, d), conv_state, conv_state, w_up, w_up, conv_w, conv_w,
      conv_b.reshape(1, -1), conv_b.reshape(1, -1), w_down, out_g.reshape(1, d))
    return y, jnp.concatenate([ns_v[:, -1], ns_g[:, -1]], axis=-1)


def _pad_cols(w, n):
    return jnp.pad(w, ((0, 0), (0, n - w.shape[1])))


def _pad_rows(w, n):
    return jnp.pad(w, ((0, n - w.shape[0]), (0, 0)))


def _rwkv_layer(b, x, shift_st, wkv_st, v_first, p):
    pre, new_shift = _rwkv_pre(b, x, shift_st, p["norm"], p["mix"], p["w1"], p["a1"], p["g1"], p["v1"])
    xr, xk, xv = pre[:3]
    r = _matmul(xr, p["wr"], name="rwkv_r")
    k = _matmul(xk, p["wk"], name="rwkv_k")
    v = _matmul(xv, p["wv"], name="rwkv_v")
    if p["v1"] is None:
        vres, v_first = None, v
    else:
        vres = (v_first, pre[6], p["v2"], p["v0"])
    y, s_bd = _wkv7(b, r, k, v, pre[3:6], (p["w2"], p["a2"], p["g2"]), (p["w0"], p["a0"]), vres,
                    p["k_k"], p["k_a"], p["r_k"], p["lnx_w"], p["lnx_b"],
                    _wkv_state_to_blockdiag(wkv_st))
    x_new = _matmul(y, p["wo"], extras=(x,), epilogue=lambda acc, res: res + acc, name="rwkv_o")
    return x_new, new_shift, _wkv_state_from_blockdiag(s_bd), v_first


def _gla_layer(b, x, gla_st, p):
    h = _rmsnorm(x, p["norm"], BF16)
    qkvg = _matmul(h, p["w_in"], n_cols=p["n_main"], name="gla_in")
    lr = _matmul(h, p["w_lr"], name="gla_lr")
    y, s_new = _gla(b, qkvg, lr, p["gk_w2"], p["gk_b"], p["head_norm"], gla_st)
    x_new = _matmul(y, p["wo"], extras=(x,), epilogue=lambda acc, res: res + acc, name="gla_o")
    return x_new, s_new


FFN_TILE_ROWS = 512


def _ffn_tile(b, t):
    if t >= FFN_TILE_ROWS:
        return 1, FFN_TILE_ROWS
    return min(b, FFN_TILE_ROWS // t), t


def _run_trunk(x, shift_st, wkv_st, gla_st, conv_st, layers, norm_final, ffn_nseq, ffn_len):
    new_shift, new_wkv, new_gla, new_conv = [], [], [], []
    v_first = None
    b, t, d = x.shape
    x = x.reshape(b * t, d)
    for i, (mixer, ffn) in enumerate(layers):
        j = i // 2
        if i % 2 == 0:
            x, s_shift, s_wkv, v_first = _rwkv_layer(b, x, shift_st[j], wkv_st[j], v_first, mixer)
            new_shift.append(s_shift)
            new_wkv.append(s_wkv)
        else:
            x, s_gla = _gla_layer(b, x, gla_st[j], mixer)
            new_gla.append(s_gla)
        x, s_conv = _conv_ffn(b, x, conv_st[i], ffn["norm"], ffn["w_up"], ffn["conv_w"], ffn["conv_b"],
                              ffn["w_down"], norm_final, i == len(layers) - 1, ffn_nseq, ffn_len)
        new_conv.append(s_conv)
    return x.reshape(b, t, d), jnp.stack(new_shift), jnp.stack(new_wkv), jnp.stack(new_gla), jnp.stack(new_conv)


def kernel(x_prompt, x_sample, state_rwkv_shift, state_rwkv_wkv, state_gla, state_ffn_conv, norm_mix, norm_ffn, norm_final, rwkv_mix, rwkv_w0, rwkv_w1, rwkv_w2, rwkv_a0, rwkv_a1, rwkv_a2, rwkv_v0, rwkv_v1, rwkv_v2, rwkv_g1, rwkv_g2, rwkv_k_k, rwkv_k_a, rwkv_r_k, rwkv_wr, rwkv_wk, rwkv_wv, rwkv_wo, rwkv_lnx_w, rwkv_lnx_b, gla_w_in, gla_gk_w2, gla_gk_b, gla_head_norm, gla_wo, ffn_w_up, ffn_conv_w, ffn_conv_b, ffn_w_down):
    depth = norm_mix.shape[0]
    d = x_prompt.shape[-1]
    bf = lambda w: w.astype(BF16)
    n_main = gla_w_in.shape[-1] - GLA_GATE_RANK
    layers = []
    for i in range(depth):
        j = i // 2
        if i % 2 == 0:
            has_vres = j > 0
            mixer = dict(
                norm=norm_mix[i], mix=rwkv_mix[j],
                wr=bf(rwkv_wr[j]), wk=bf(rwkv_wk[j]), wv=bf(rwkv_wv[j]), wo=bf(rwkv_wo[j]),
                w0=rwkv_w0[j], w1=bf(_pad_cols(rwkv_w1[j], LANES)), w2=bf(_pad_rows(rwkv_w2[j], LANES)),
                a0=rwkv_a0[j], a1=bf(_pad_cols(rwkv_a1[j], LANES)), a2=bf(_pad_rows(rwkv_a2[j], LANES)),
                g1=bf(rwkv_g1[j]), g2=bf(rwkv_g2[j]),
                v0=rwkv_v0[j - 1] if has_vres else None,
                v1=bf(_pad_cols(rwkv_v1[j - 1], LANES)) if has_vres else None,
                v2=bf(_pad_rows(rwkv_v2[j - 1], LANES)) if has_vres else None,
                k_k=rwkv_k_k[j], k_a=rwkv_k_a[j], r_k=rwkv_r_k[j].reshape(d),
                lnx_w=rwkv_lnx_w[j], lnx_b=rwkv_lnx_b[j])
        else:
            mixer = dict(
                norm=norm_mix[i],
                w_in=bf(gla_w_in[j]), n_main=n_main,
                w_lr=bf(_pad_cols(gla_w_in[j][:, n_main:], LANES)),
                gk_w2=bf(_pad_rows(gla_gk_w2[j], LANES)), gk_b=gla_gk_b[j],
                head_norm=gla_head_norm[j], wo=bf(gla_wo[j]))
        ffn = dict(norm=norm_ffn[i], w_up=bf(ffn_w_up[i]), conv_w=ffn_conv_w[i],
                   conv_b=ffn_conv_b[i],
                   w_down=bf(ffn_w_down[i]))
        layers.append((mixer, ffn))

    bp, tp, _ = x_prompt.shape
    bs, ts, _ = x_sample.shape
    zeros_like_state = lambda s: jnp.zeros((s.shape[0], bp) + s.shape[2:], s.dtype)
    out_p = _run_trunk(x_prompt, zeros_like_state(state_rwkv_shift), zeros_like_state(state_rwkv_wkv),
                       zeros_like_state(state_gla), zeros_like_state(state_ffn_conv), layers,
                       norm_final, *_ffn_tile(bp, tp))
    out_s = _run_trunk(x_sample, state_rwkv_shift, state_rwkv_wkv, state_gla, state_ffn_conv, layers,
                       norm_final, *_ffn_tile(bs, ts))
    return (out_p[0], out_s[0]) + tuple(out_p[1:]) + tuple(out_s[1:])
```

```python
import functools

import jax
import jax.numpy as jnp
from jax import lax
from jax.experimental import pallas as pl
from jax.experimental.pallas import tpu as pltpu

F32 = jnp.float32
BF16 = jnp.bfloat16

CHUNK = 64
RWKV_HEAD = 64
RWKV_GN_EPS = 64e-5
GLA_GATE_RANK = 16
GLA_GATE_NORMALIZER = 16.0
GLA_NORM_EPS = 1e-5
NORM_EPS = 1e-6

LANES = 128
MXU_DIM = 256
HEADS_PER_GROUP = MXU_DIM // RWKV_HEAD
VMEM_LIMIT = 56 * 1024 * 1024


def _params(sem):
    return pltpu.CompilerParams(dimension_semantics=sem, vmem_limit_bytes=VMEM_LIMIT)


def _dot(a, b):
    return jnp.dot(a, b, preferred_element_type=F32)


def _dot_nt(a, b):
    return lax.dot_general(a, b, (((1,), (1,)), ((), ())), preferred_element_type=F32)


def _dot_tn(a, b):
    return lax.dot_general(a, b, (((0,), (0,)), ((), ())), preferred_element_type=F32)


def _split_dot(fixed_bf16, x, terms):
    acc = None
    rem = x
    for _ in range(terms):
        piece = rem.astype(BF16)
        rem = rem - piece.astype(F32)
        part = _dot(fixed_bf16, piece)
        acc = part if acc is None else acc + part
    return acc


def _log_sigmoid(x):
    return jnp.minimum(x, 0.0) - jnp.log(1.0 + jnp.exp(-jnp.abs(x)))


def _sigmoid(x):
    return 1.0 / (1.0 + jnp.exp(-x))


def _silu(x):
    return x * _sigmoid(x)


def _rms(x, eps):
    return x * lax.rsqrt(jnp.mean(x * x, axis=-1, keepdims=True) + eps)


def _tri_incl(n):
    r = lax.broadcasted_iota(jnp.int32, (n, n), 0)
    c = lax.broadcasted_iota(jnp.int32, (n, n), 1)
    return c <= r


def _rmsnorm_kernel(x_ref, g_ref, o_ref):
    o_ref[...] = (_rms(x_ref[...], NORM_EPS) * g_ref[...]).astype(o_ref.dtype)


def _rmsnorm(x, g, out_dtype):
    m, d = x.shape
    tm = min(m, 1024)
    return pl.pallas_call(
        _rmsnorm_kernel,
        grid=(m // tm,),
        in_specs=[pl.BlockSpec((tm, d), lambda i: (i, 0)), pl.BlockSpec((1, d), lambda i: (0, 0))],
        out_specs=pl.BlockSpec((tm, d), lambda i: (i, 0)),
        out_shape=jax.ShapeDtypeStruct((m, d), out_dtype),
        compiler_params=_params(("arbitrary",)),
        name="rmsnorm",
    )(x, g.reshape(1, d))


def _mm_kernel(a_ref, b_ref, *rest, epilogue, n_extra):
    o_ref = rest[n_extra]
    acc = _dot(a_ref[...], b_ref[...])
    if epilogue is not None:
        acc = epilogue(acc, *[e[...] for e in rest[:n_extra]])
    o_ref[...] = acc.astype(o_ref.dtype)


def _matmul(a, b, extras=(), epilogue=None, out_dtype=F32, tm=1024, tn=1024, n_cols=None,
            name="matmul"):
    m, k = a.shape
    n = b.shape[1] if n_cols is None else n_cols
    tm, tn = min(tm, m), min(tn, n)
    assert m % tm == 0 and n % tn == 0, (m, n, tm, tn)
    in_specs = [pl.BlockSpec((tm, k), lambda i, j: (i, 0)), pl.BlockSpec((k, tn), lambda i, j: (0, j))]
    for e in extras:
        if e.shape[0] == 1:
            in_specs.append(pl.BlockSpec((1, tn), lambda i, j: (0, j)))
        else:
            in_specs.append(pl.BlockSpec((tm, tn), lambda i, j: (i, j)))
    return pl.pallas_call(
        functools.partial(_mm_kernel, epilogue=epilogue, n_extra=len(extras)),
        grid=(m // tm, n // tn),
        in_specs=in_specs,
        out_specs=pl.BlockSpec((tm, tn), lambda i, j: (i, j)),
        out_shape=jax.ShapeDtypeStruct((m, n), out_dtype),
        compiler_params=_params(("arbitrary", "arbitrary")),
        name=name,
    )(a, b, *extras)


def _rwkv_pre_kernel(x_ref, st_ref, g_ref, mix_ref, w1_ref, a1_ref, g1_ref, *rest, has_v1):
    if has_v1:
        v1_ref, rest = rest[0], rest[1:]
    xr_ref, xk_ref, xv_ref, lw_ref, la_ref, lg_ref = rest[:6]
    rest = rest[6:]
    if has_v1:
        lv_ref, rest = rest[0], rest[1:]
    hl_ref, carry_ref = rest
    tt = x_ref.shape[0]

    @pl.when(pl.program_id(1) == 0)
    def _():
        carry_ref[...] = st_ref[...]

    h = _rms(x_ref[...], NORM_EPS) * g_ref[...]
    row = lax.broadcasted_iota(jnp.int32, (tt, 1), 0)
    h_prev = jnp.where(row == 0, carry_ref[...], pltpu.roll(h, 1, axis=0))
    carry_ref[...] = h[tt - 1:tt, :]
    hl_ref[...] = h[tt - 8:, :]
    xx = h_prev - h
    mixed = lambda i: (h + xx * mix_ref[i:i + 1, :]).astype(BF16)
    xr_ref[...] = mixed(0)
    xk_ref[...] = mixed(2)
    xv = mixed(3)
    xv_ref[...] = xv
    lw_ref[...] = jnp.tanh(_dot(mixed(1), w1_ref[...])).astype(BF16)
    la_ref[...] = _dot(mixed(4), a1_ref[...]).astype(BF16)
    lg_ref[...] = _sigmoid(_dot(mixed(5), g1_ref[...])).astype(BF16)
    if has_v1:
        lv_ref[...] = _dot(xv, v1_ref[...]).astype(BF16)


def _rwkv_pre(b, x, shift_state, g, mix, w1, a1, g1, v1):
    m, d = x.shape
    t = m // b
    tt = min(t, 512)
    nt = t // tt
    has_v1 = v1 is not None
    lora = [w1, a1, g1] + ([v1] if has_v1 else [])
    flat_spec = lambda n: pl.BlockSpec((tt, n), lambda i, j: (i * nt + j, 0))
    full_spec = lambda w: pl.BlockSpec(w.shape, lambda i, j: (0, 0))
    wide = [jax.ShapeDtypeStruct((b * t, d), BF16)] * 3
    narrow = [jax.ShapeDtypeStruct((b * t, w.shape[1]), BF16) for w in lora]
    outs = pl.pallas_call(
        functools.partial(_rwkv_pre_kernel, has_v1=has_v1),
        grid=(b, nt),
        in_specs=[flat_spec(d),
                  pl.BlockSpec((None, 1, d), lambda i, j: (i, 0, 0)),
                  pl.BlockSpec((1, d), lambda i, j: (0, 0)),
                  pl.BlockSpec((6, d), lambda i, j: (0, 0))] + [full_spec(w) for w in lora],
        out_specs=[flat_spec(d)] * 3 + [flat_spec(w.shape[1]) for w in lora]
                  + [pl.BlockSpec((None, 8, d), lambda i, j: (i, 0, 0))],
        out_shape=wide + narrow + [jax.ShapeDtypeStruct((b, 8, d), F32)],
        scratch_shapes=[pltpu.VMEM((1, d), F32)],
        compiler_params=_params(("arbitrary", "arbitrary")),
        name="rwkv_pre",
    )(x, shift_state.reshape(b, 1, d), g.reshape(1, d), mix, *lora)
    return outs[:-1], outs[-1][:, 7, :]


def _wkv7_kernel(*refs, has_vres):
    refs = list(refs)
    r_ref, k_ref, v_ref, lw_ref, la_ref, lg_ref = refs[:6]
    w2_ref, a2_ref, g2_ref, w0_ref, a0_ref = refs[6:11]
    refs = refs[11:]
    if has_vres:
        vf_ref, lv_ref, v2_ref, v0_ref = refs[:4]
        refs = refs[4:]
    kk_ref, ka_ref, rk_ref, gnw_ref, gnb_ref, s0_ref, y_ref, s_ref = refs
    c_len = CHUNK
    n_sub, d = r_ref.shape[0] // c_len, r_ref.shape[1]
    n_groups = d // MXU_DIM

    @pl.when(pl.program_id(1) == 0)
    def _():
        s_ref[...] = s0_ref[...]

    ri = lax.broadcasted_iota(jnp.int32, (MXU_DIM, MXU_DIM), 0)
    ci = lax.broadcasted_iota(jnp.int32, (MXU_DIM, MXU_DIM), 1)
    head_bits = RWKV_HEAD.bit_length() - 1
    same_head = (ri >> head_bits) == (ci >> head_bits)
    bd = jnp.where(same_head, 1.0, 0.0)
    bd_b = bd.astype(BF16)
    tri_b = jnp.where(_tri_incl(c_len), 1.0, 0.0).astype(BF16)

    assert c_len == RWKV_HEAD and 2 * RWKV_HEAD == LANES
    t_c = lax.broadcasted_iota(jnp.int32, (MXU_DIM, LANES), 0) & (c_len - 1)
    lane_c = lax.broadcasted_iota(jnp.int32, (MXU_DIM, LANES), 1)
    s_c = lane_c & (RWKV_HEAD - 1)
    low_half = lane_c < RWKV_HEAD
    strict_c = s_c < t_c
    incl_c = s_c <= t_c
    eye_c = jnp.where(s_c == t_c, 1.0, 0.0)
    lane_t = lax.broadcasted_iota(jnp.int32, (c_len, LANES), 1)
    half_masks = (jnp.where(lane_t < RWKV_HEAD, 1.0, 0.0).astype(BF16),
                  jnp.where(lane_t >= RWKV_HEAD, 1.0, 0.0).astype(BF16))

    def stack(x):
        return jnp.concatenate([x] * HEADS_PER_GROUP, axis=0)

    def stack_masked(x):
        return stack(x.astype(BF16)) * bd_b

    def head_sum(x):
        return _dot(x.astype(BF16), bd_b)

    def expand(xc, half):
        zeros = jnp.zeros((c_len, LANES), BF16)
        blocks = []
        for hd in range(HEADS_PER_GROUP):
            src = xc[hd * c_len:(hd + 1) * c_len]
            if hd % 2 != half:
                src = pltpu.roll(src, RWKV_HEAD, axis=1)
            tile = src.astype(BF16) * half_masks[hd % 2]
            blocks.append(jnp.concatenate(
                [tile if lt == hd // 2 else zeros for lt in range(MXU_DIM // LANES)], axis=1))
        return jnp.concatenate(blocks, axis=0)

    groups = range(n_groups)
    sls = [slice(grp * MXU_DIM, (grp + 1) * MXU_DIM) for grp in groups]
    lw_n, la_n, lg_n = lw_ref[...], la_ref[...], lg_ref[...]
    w_all = [_log_sigmoid(w0_ref[:, sl] + _dot(lw_n, w2_ref[:, sl])) - 0.5 for sl in sls]
    a_all = [_sigmoid(a0_ref[:, sl] + _dot(la_n, a2_ref[:, sl])) for sl in sls]
    g_all = [_dot(lg_n, g2_ref[:, sl]) for sl in sls]
    v_all = [v_ref[:, sl] for sl in sls]
    if has_vres:
        lv_n = lv_ref[...]
        mix_v = [_sigmoid(v0_ref[:, sl] + _dot(lv_n, v2_ref[:, sl])) for sl in sls]
        v_all = [v_all[i] + (vf_ref[:, sls[i]] - v_all[i]) * mix_v[i] for i in groups]
    chains = [(c, grp) for c in range(n_sub) for grp in groups]
    rws = [slice(c * c_len, (c + 1) * c_len) for c, _ in chains]
    sls = [sls[grp] for _, grp in chains]
    groups = range(len(chains))
    r = [r_ref[rws[i], sls[i]] for i in groups]
    k = [k_ref[rws[i], sls[i]] for i in groups]
    v = [v_all[chains[i][1]][rws[i]] for i in groups]
    a = [a_all[chains[i][1]][rws[i]] for i in groups]
    g = [g_all[chains[i][1]][rws[i]] for i in groups]
    lw = [-jnp.exp(w_all[chains[i][1]][rws[i]]) for i in groups]
    kk = [k[i] * kk_ref[:, sls[i]] for i in groups]
    kk_ss = [head_sum(x * x) for x in kk]
    cum = [_split_dot(tri_b, x, 2) for x in lw]
    kk = [kk[i] / jnp.maximum(jnp.sqrt(kk_ss[i]), 1e-12) for i in groups]
    kmod = [k[i] * (1.0 + (a[i] - 1.0) * ka_ref[:, sls[i]]) for i in groups]
    kka = [kk[i] * a[i] for i in groups]
    last = [x[c_len - 1:c_len, :] for x in cum]
    e_neg = [jnp.exp(-x) for x in cum]
    ar = [jnp.concatenate([stack_masked(-kk[i] * jnp.exp(cum[i] - lw[i])),
                           stack_masked(r[i] * jnp.exp(cum[i]))], axis=0) for i in groups]
    bk = [jnp.concatenate([kka[i] * e_neg[i], kmod[i] * e_neg[i]], axis=0).astype(BF16) for i in groups]
    z = [_dot_nt(ar[i], bk[i]) for i in groups]
    za = [jnp.where(strict_c, x[:MXU_DIM], 0.0) for x in z]
    zr = [jnp.where(incl_c, x[MXU_DIM:], 0.0) for x in z]
    a_ak = [expand(x, 1) for x in za]
    a_r = [jnp.concatenate([expand(x, 0), expand(x, 1)], axis=1) for x in zr]

    rhs = [jnp.where(low_half, x, eye_c) for x in za]
    p_bd = [expand(x, 0) for x in za]
    n_steps = (c_len - 1).bit_length()
    for step in range(n_steps):
        x2 = [_dot(p_bd[i], rhs[i].astype(BF16)) for i in groups]
        rhs = [x2[i] + jnp.where(low_half, 0.0, rhs[i]) for i in groups]
        if step + 1 < n_steps:
            p_bd = [expand(x, 0) for x in x2]
    tinv_b = [expand(x, 1) for x in rhs]

    v_st = [stack_masked(x) for x in v]
    av = [_dot(a_ak[i], v_st[i]) for i in groups]
    e_last = [jnp.exp(last[i] - cum[i]) for i in groups]
    bk_hat = [jnp.concatenate([stack_masked(kka[i] * e_last[i]), stack_masked(kmod[i] * e_last[i])],
                              axis=0) for i in groups]

    o_st = [None] * len(chains)
    for c in range(n_sub):
        cur = [i for i in groups if chains[i][0] == c]
        s = {i: s_ref[chains[i][1]] for i in cur}
        y = {i: _dot_nt(ar[i], s[i].astype(BF16)) for i in cur}
        u = {i: _dot(tinv_b[i], (y[i][:MXU_DIM] + av[i]).astype(BF16)) for i in cur}
        uv = {i: jnp.concatenate([u[i].astype(BF16), v_st[i]], axis=0) for i in cur}
        for i in cur:
            o_st[i] = y[i][MXU_DIM:] + _dot(a_r[i], uv[i])
        for i in cur:
            s_ref[chains[i][1]] = s[i] * jnp.exp(last[i]) + _dot_tn(uv[i], bk_hat[i])

    o = [functools.reduce(lambda p, q: p + q,
                          [x[hd * c_len:(hd + 1) * c_len] for hd in range(HEADS_PER_GROUP)]) for x in o_st]
    inv_n = 1.0 / RWKV_HEAD
    mu = [head_sum(x) * inv_n for x in o]
    bonus_dot = [head_sum(r[i] * kmod[i] * rk_ref[:, sls[i]]) for i in groups]
    dev = [o[i] - mu[i] for i in groups]
    var = [head_sum(x * x) * inv_n for x in dev]
    for i in groups:
        sl = sls[i]
        o_n = dev[i] * lax.rsqrt(var[i] + RWKV_GN_EPS) * gnw_ref[:, sl] + gnb_ref[:, sl]
        y_ref[rws[i], sl] = ((o_n + bonus_dot[i] * v[i]) * g[i]).astype(y_ref.dtype)


WKV_CHUNKS_PER_STEP = 2


def _wkv7(b, r, k, v, lora, lora_w, lora_b, vres, k_k, k_a, r_k, lnx_w, lnx_b, s0_bd):
    m, d = r.shape
    t = m // b
    blk_rows = min(t, WKV_CHUNKS_PER_STEP * CHUNK)
    n_steps = t // blk_rows
    n_groups = d // MXU_DIM
    rows = lambda n: pl.BlockSpec((blk_rows, n), lambda i, j: (i * n_steps + j, 0))
    whole = lambda w: pl.BlockSpec(w.shape, lambda i, j: (0, 0))
    vec_spec = pl.BlockSpec((1, d), lambda i, j: (0, 0))
    st_spec = pl.BlockSpec((None, n_groups, MXU_DIM, MXU_DIM), lambda i, j: (i, 0, 0, 0))
    vec = lambda p: p.reshape(1, d)
    args = [r, k, v, *lora, *lora_w, *[vec(x) for x in lora_b]]
    specs = ([rows(d)] * 3 + [rows(x.shape[1]) for x in lora] + [whole(w) for w in lora_w]
             + [vec_spec] * len(lora_b))
    if vres is not None:
        v_first, lv, v2, v0 = vres
        args += [v_first, lv, v2, vec(v0)]
        specs += [rows(d), rows(lv.shape[1]), whole(v2), vec_spec]
    args += [vec(k_k), vec(k_a), vec(r_k), vec(lnx_w), vec(lnx_b), s0_bd]
    specs += [vec_spec] * 5 + [st_spec]
    return pl.pallas_call(
        functools.partial(_wkv7_kernel, has_vres=vres is not None),
        grid=(b, n_steps),
        in_specs=specs,
        out_specs=[rows(d), st_spec],
        out_shape=[jax.ShapeDtypeStruct((m, d), BF16),
                   jax.ShapeDtypeStruct(s0_bd.shape, F32)],
        compiler_params=_params(("arbitrary", "arbitrary")),
        name="wkv7",
    )(*args)


def _wkv_state_to_blockdiag(s):
    b, h, n, _ = s.shape
    g = h // HEADS_PER_GROUP
    s = s.reshape(b, g, HEADS_PER_GROUP, n, n)
    row_blocks = [jnp.pad(s[:, :, i], ((0, 0), (0, 0), (0, 0), (i * n, MXU_DIM - (i + 1) * n)))
                  for i in range(HEADS_PER_GROUP)]
    return jnp.concatenate(row_blocks, axis=2)


def _wkv_state_from_blockdiag(s_bd):
    b, g = s_bd.shape[:2]
    n = RWKV_HEAD
    blocks = [s_bd[:, :, i * n:(i + 1) * n, i * n:(i + 1) * n] for i in range(HEADS_PER_GROUP)]
    return jnp.stack(blocks, axis=2).reshape(b, g * HEADS_PER_GROUP, n, n)


def _gla_kernel(q_ref, k_ref, v_ref, gate_ref, lr_ref, w2_ref, gb_ref, hn_ref, s0_ref, y_ref, sn_ref,
                s_ref):
    tc = q_ref.shape[0]
    n_heads, dv, dk = s_ref.shape
    heads = range(n_heads)
    ksl = [slice(h * dk, (h + 1) * dk) for h in heads]
    vsl = [slice(h * dv, (h + 1) * dv) for h in heads]

    @pl.when(pl.program_id(1) == 0)
    def _():
        for h in heads:
            s_ref[h] = s0_ref[h].T

    tri = _tri_incl(CHUNK)
    tri_b = jnp.where(tri, 1.0, 0.0).astype(BF16)
    scale = dk ** -0.5
    for c in range(tc // CHUNK):
        rows = slice(c * CHUNK, (c + 1) * CHUNK)
        lr = lr_ref[rows, :].astype(BF16)
        gk = [_log_sigmoid(_dot(lr, w2_ref[:, sl]) + gb_ref[:, sl]) * (1.0 / GLA_GATE_NORMALIZER)
              for sl in ksl]
        cum = [_split_dot(tri_b, x, 3) for x in gk]
        last = [x[CHUNK - 1:CHUNK, :] for x in cum]
        qe = [(q_ref[rows, ksl[h]] * scale * jnp.exp(cum[h])).astype(BF16) for h in heads]
        ke = [(k_ref[rows, ksl[h]] * jnp.exp(-cum[h])).astype(BF16) for h in heads]
        k2 = [(k_ref[rows, ksl[h]] * jnp.exp(last[h] - cum[h])).astype(BF16) for h in heads]
        vb = [v_ref[rows, sl].astype(BF16) for sl in vsl]
        scores = [jnp.where(tri, _dot_nt(qe[h], ke[h]), 0.0).astype(BF16) for h in heads]
        s_t = [s_ref[h] for h in heads]
        o = [_dot(scores[h], vb[h]) + _dot_nt(qe[h], s_t[h].astype(BF16)) for h in heads]
        for h in heads:
            s_ref[h] = s_t[h] * jnp.exp(last[h]) + _dot_tn(vb[h], k2[h])
        for h in heads:
            o_n = _rms(o[h], GLA_NORM_EPS) * hn_ref[...]
            y_ref[rows, vsl[h]] = (o_n * _silu(gate_ref[rows, vsl[h]])).astype(y_ref.dtype)

    @pl.when(pl.program_id(1) == pl.num_programs(1) - 1)
    def _():
        for h in heads:
            sn_ref[h] = s_ref[h].T


def _gla(b, qkvg, lr, gk_w2, gk_b, head_norm, s0):
    t = qkvg.shape[0] // b
    h, dk, dv = s0.shape[1:]
    tc = min(t, 4 * CHUNK)
    dk_all, dv_all = h * dk, h * dv
    assert dv_all == 2 * dk_all
    st_spec = pl.BlockSpec((None, h, dk, dv), lambda i, c: (i, 0, 0, 0))
    n_steps = t // tc
    rows = lambda n, col: pl.BlockSpec((tc, n), lambda i, c: (i * n_steps + c, col))
    return pl.pallas_call(
        _gla_kernel,
        grid=(b, t // tc),
        in_specs=[rows(dk_all, 0), rows(dk_all, 1), rows(dv_all, 1), rows(dv_all, 2), rows(LANES, 0),
                  pl.BlockSpec((LANES, dk_all), lambda i, c: (0, 0)),
                  pl.BlockSpec((1, dk_all), lambda i, c: (0, 0)),
                  pl.BlockSpec((1, dv), lambda i, c: (0, 0)),
                  st_spec],
        out_specs=[rows(dv_all, 0), st_spec],
        out_shape=[jax.ShapeDtypeStruct((b * t, dv_all), BF16),
                   jax.ShapeDtypeStruct(s0.shape, F32)],
        scratch_shapes=[pltpu.VMEM((h, dv, dk), F32)],
        compiler_params=_params(("arbitrary", "arbitrary")),
        name="gla",
    )(qkvg, qkvg, qkvg, qkvg, lr, gk_w2, gk_b.reshape(1, -1), head_norm.reshape(1, dv), s0)


def _ffn_kernel(x_ref, g_ref, sv_ref, sg_ref, wv_ref, wg_ref, cwv_ref, cwg_ref, cbv_ref, cbg_ref,
                wd_ref, og_ref, o_ref, nsv_ref, nsg_ref, h_ref, acc_ref, slab_ref, ua_ref, ub_ref, carry_ref,
                *, nj, nseq, seq_len, norm_output):
    rows, d = x_ref.shape
    tn = wv_ref.shape[1]
    j = pl.program_id(2)
    row_tile = pl.program_id(0) * pl.num_programs(1) + pl.program_id(1)
    jt = _hidden_tile(jnp.maximum(j - 1, 0), row_tile, nj)

    n_ph = min(8, seq_len // 16)
    q_len = seq_len // n_ph
    phase_rows = lambda p: pl.ds(p, nseq * q_len, stride=n_ph)
    blk = lambda p: slice(p * q_len, (p + 1) * q_len)
    slabs = [slice(c * LANES, (c + 1) * LANES) for c in range(d // LANES)]

    @pl.when(j == 0)
    def _():
        for c, cl in enumerate(slabs):
            slab_ref[c] = x_ref[:, cl]
        for c, cl in enumerate(slabs):
            for p in range(n_ph):
                acc_ref[:, blk(p), cl] = slab_ref[c, phase_rows(p), :].reshape(nseq, q_len, LANES)
        h_ref[...] = (_rms(acc_ref[...], NORM_EPS) * g_ref[...]).astype(BF16)

    @pl.when((pl.program_id(1) == 0) & (j > 0))
    def _():
        carry_ref[2 * jt] = sv_ref[...]
        carry_ref[2 * jt + 1] = sg_ref[...]

    cols = [slice(c * MXU_DIM, (c + 1) * MXU_DIM) for c in range(tn // MXU_DIM)]
    first = lax.broadcasted_iota(jnp.int32, (1, q_len, 1), 1) == 0

    def conv(u, cs, cw_ref, cb_ref, slot, ns_ref):
        u = u.reshape(nseq, seq_len, MXU_DIM)
        w0, w1, w2, cb = cw_ref[0:1, cs], cw_ref[1:2, cs], cw_ref[2:3, cs], cb_ref[:, cs]
        p2, p1 = carry_ref[slot, :, 0:1, cs], carry_ref[slot, :, 1:2, cs]
        back1 = jnp.where(first, p1, pltpu.roll(u[:, blk(n_ph - 1), :], 1, axis=1))
        back2 = jnp.where(first, p2, pltpu.roll(u[:, blk(n_ph - 2), :], 1, axis=1))
        r1 = jnp.concatenate([back1, u[:, :(n_ph - 1) * q_len, :]], axis=1)
        r2 = jnp.concatenate([back2, back1, u[:, :(n_ph - 2) * q_len, :]], axis=1)
        last = seq_len - 1
        before_last = (n_ph - 1) * q_len - 1
        for row, src in ((0, before_last), (1, last)):
            carry_ref[slot, :, row:row + 1, cs] = u[:, src:src + 1, :]
            ns_ref[:, row:row + 1, cs] = u[:, src:src + 1, :]
        return cb + w0 * r2 + w1 * r1 + w2 * u

    def step(up_ref, dn_ref):
        h = None if up_ref is None else h_ref[...].reshape(rows, d)
        acc = None
        for cs in cols:
            if up_ref is not None:
                up_ref[0, :, cs] = _dot(h, wv_ref[:, cs])
            if dn_ref is not None:
                val = conv(dn_ref[0, :, cs], cs, cwv_ref, cbv_ref, 2 * jt, nsv_ref)
            if up_ref is not None:
                up_ref[1, :, cs] = _dot(h, wg_ref[:, cs])
            if dn_ref is not None:
                gate = conv(dn_ref[1, :, cs], cs, cwg_ref, cbg_ref, 2 * jt + 1, nsg_ref)
                act = (_silu(gate) * val).reshape(rows, MXU_DIM).astype(BF16)
                part = _dot(act, wd_ref[cs, :])
                acc = part if acc is None else acc + part
        if dn_ref is not None:
            acc_ref[...] += acc.reshape(nseq, seq_len, d)

    even = (j & 1) == 0
    inner = (j > 0) & (j < nj)

    @pl.when(j == 0)
    def _():
        step(ua_ref, None)

    @pl.when(inner & even)
    def _():
        step(ua_ref, ub_ref)

    @pl.when(inner & jnp.logical_not(even))
    def _():
        step(ub_ref, ua_ref)

    @pl.when(j == nj)
    def _():
        step(None, ua_ref if (nj - 1) % 2 == 0 else ub_ref)
        if norm_output:
            acc_ref[...] = _rms(acc_ref[...], NORM_EPS) * og_ref[...]
        for c, cl in enumerate(slabs):
            for p in range(n_ph):
                slab_ref[c, phase_rows(p), :] = acc_ref[:, blk(p), cl].reshape(nseq * q_len, LANES)
        for c, cl in enumerate(slabs):
            o_ref[:, cl] = slab_ref[c]


FFN_TILE_COLS = 512


def _hidden_tile(step_tile, row_tile, nj):
    return step_tile + (row_tile % 2) * (nj - 1 - 2 * step_tile)


def _conv_ffn(b, x, conv_state, g, w_up, conv_w, conv_b, w_down, out_g, norm_output, nseq, seq_len):
    m, d = x.shape
    t = m // b
    f = w_down.shape[0]
    tn = FFN_TILE_COLS
    nj = f // tn
    assert b % nseq == 0 and t % seq_len == 0 and f % tn == 0
    assert nseq == 1 or seq_len == t
    x_spec = pl.BlockSpec((nseq * seq_len, d), lambda i, s, j: (i * (t // seq_len) + s, 0))
    n_s = t // seq_len
    up_tile = lambda i, s, j: _hidden_tile(jnp.minimum(j, nj - 1), i * n_s + s, nj)
    dn_tile = lambda i, s, j: _hidden_tile(jnp.maximum(j - 1, 0), i * n_s + s, nj)
    st_v = pl.BlockSpec((nseq, 2, tn), lambda i, s, j: (i, 0, dn_tile(i, s, j)))
    st_g = pl.BlockSpec((nseq, 2, tn), lambda i, s, j: (i, 0, nj + dn_tile(i, s, j)))
    tail_spec = pl.BlockSpec((nseq, None, 2, tn), lambda i, s, j: (i, s, 0, dn_tile(i, s, j)))
    col_v = lambda n: pl.BlockSpec((n, tn), lambda i, s, j: (0, dn_tile(i, s, j)))
    col_g = lambda n: pl.BlockSpec((n, tn), lambda i, s, j: (0, nj + dn_tile(i, s, j)))
    y, ns_v, ns_g = pl.pallas_call(
        functools.partial(_ffn_kernel, nj=nj, nseq=nseq, seq_len=seq_len, norm_output=norm_output),
        grid=(b // nseq, n_s, nj + 1),
        in_specs=[x_spec, pl.BlockSpec((1, d), lambda i, s, j: (0, 0)), st_v, st_g,
                  pl.BlockSpec((d, tn), lambda i, s, j: (0, up_tile(i, s, j))),
                  pl.BlockSpec((d, tn), lambda i, s, j: (0, nj + up_tile(i, s, j))),
                  col_v(3), col_g(3), col_v(1), col_g(1),
                  pl.BlockSpec((tn, d), lambda i, s, j: (dn_tile(i, s, j), 0)),
                  pl.BlockSpec((1, d), lambda i, s, j: (0, 0))],
        out_specs=[x_spec, tail_spec, tail_spec],
        out_shape=[jax.ShapeDtypeStruct((m, d), F32),
                   jax.ShapeDtypeStruct((b, n_s, 2, f), F32),
                   jax.ShapeDtypeStruct((b, n_s, 2, f), F32)],
        scratch_shapes=[pltpu.VMEM((nseq, seq_len, d), BF16),
                        pltpu.VMEM((nseq, seq_len, d), F32),
                        pltpu.VMEM((d // LANES, nseq * seq_len, LANES), F32),
                        pltpu.VMEM((2, nseq * seq_len, tn), F32),
                        pltpu.VMEM((2, nseq * seq_len, tn), F32),
                        pltpu.VMEM((2 * nj, nseq, 2, tn), F32)],
        compiler_params=_params(("arbitrary", "arbitrary", "arbitrary")),
        name="conv_ffn",
    )(x, g.reshape(1, d), conv_state, conv_state, w_up, w_up, conv_w, conv_w,
      conv_b.reshape(1, -1), conv_b.reshape(1, -1), w_down, out_g.reshape(1, d))
    return y, jnp.concatenate([ns_v[:, -1], ns_g[:, -1]], axis=-1)


def _pad_cols(w, n):
    return jnp.pad(w, ((0, 0), (0, n - w.shape[1])))


def _pad_rows(w, n):
    return jnp.pad(w, ((0, n - w.shape[0]), (0, 0)))


def _rwkv_layer(b, x, shift_st, wkv_st, v_first, p):
    pre, new_shift = _rwkv_pre(b, x, shift_st, p["norm"], p["mix"], p["w1"], p["a1"], p["g1"], p["v1"])
    xr, xk, xv = pre[:3]
    r = _matmul(xr, p["wr"], name="rwkv_r")
    k = _matmul(xk, p["wk"], name="rwkv_k")
    v = _matmul(xv, p["wv"], name="rwkv_v")
    if p["v1"] is None:
        vres, v_first = None, v
    else:
        vres = (v_first, pre[6], p["v2"], p["v0"])
    y, s_bd = _wkv7(b, r, k, v, pre[3:6], (p["w2"], p["a2"], p["g2"]), (p["w0"], p["a0"]), vres,
                    p["k_k"], p["k_a"], p["r_k"], p["lnx_w"], p["lnx_b"],
                    _wkv_state_to_blockdiag(wkv_st))
    x_new = _matmul(y, p["wo"], extras=(x,), epilogue=lambda acc, res: res + acc, name="rwkv_o")
    return x_new, new_shift, _wkv_state_from_blockdiag(s_bd), v_first


def _gla_layer(b, x, gla_st, p):
    h = _rmsnorm(x, p["norm"], BF16)
    qkvg = _matmul(h, p["w_in"], n_cols=p["n_main"], name="gla_in")
    lr = _matmul(h, p["w_lr"], name="gla_lr")
    y, s_new = _gla(b, qkvg, lr, p["gk_w2"], p["gk_b"], p["head_norm"], gla_st)
    x_new = _matmul(y, p["wo"], extras=(x,), epilogue=lambda acc, res: res + acc, name="gla_o")
    return x_new, s_new


FFN_TILE_ROWS = 512


def _ffn_tile(b, t):
    if t >= FFN_TILE_ROWS:
        return 1, FFN_TILE_ROWS
    return min(b, FFN_TILE_ROWS // t), t


def _run_trunk(x, shift_st, wkv_st, gla_st, conv_st, layers, norm_final, ffn_nseq, ffn_len):
    new_shift, new_wkv, new_gla, new_conv = [], [], [], []
    v_first = None
    b, t, d = x.shape
    x = x.reshape(b * t, d)
    for i, (mixer, ffn) in enumerate(layers):
        j = i // 2
        if i % 2 == 0:
            x, s_shift, s_wkv, v_first = _rwkv_layer(b, x, shift_st[j], wkv_st[j], v_first, mixer)
            new_shift.append(s_shift)
            new_wkv.append(s_wkv)
        else:
            x, s_gla = _gla_layer(b, x, gla_st[j], mixer)
            new_gla.append(s_gla)
        x, s_conv = _conv_ffn(b, x, conv_st[i], ffn["norm"], ffn["w_up"], ffn["conv_w"], ffn["conv_b"],
                              ffn["w_down"], norm_final, i == len(layers) - 1, ffn_nseq, ffn_len)
        new_conv.append(s_conv)
    return x.reshape(b, t, d), jnp.stack(new_shift), jnp.stack(new_wkv), jnp.stack(new_gla), jnp.stack(new_conv)


def kernel(x_prompt, x_sample, state_rwkv_shift, state_rwkv_wkv, state_gla, state_ffn_conv, norm_mix, norm_ffn, norm_final, rwkv_mix, rwkv_w0, rwkv_w1, rwkv_w2, rwkv_a0, rwkv_a1, rwkv_a2, rwkv_v0, rwkv_v1, rwkv_v2, rwkv_g1, rwkv_g2, rwkv_k_k, rwkv_k_a, rwkv_r_k, rwkv_wr, rwkv_wk, rwkv_wv, rwkv_wo, rwkv_lnx_w, rwkv_lnx_b, gla_w_in, gla_gk_w2, gla_gk_b, gla_head_norm, gla_wo, ffn_w_up, ffn_conv_w, ffn_conv_b, ffn_w_down):
    depth = norm_mix.shape[0]
    d = x_prompt.shape[-1]
    bf = lambda w: w.astype(BF16)
    n_main = gla_w_in.shape[-1] - GLA_GATE_RANK
    layers = []
    for i in range(depth):
        j = i // 2
        if i % 2 == 0:
            has_vres = j > 0
            mixer = dict(
                norm=norm_mix[i], mix=rwkv_mix[j],
                wr=bf(rwkv_wr[j]), wk=bf(rwkv_wk[j]), wv=bf(rwkv_wv[j]), wo=bf(rwkv_wo[j]),
                w0=rwkv_w0[j], w1=bf(_pad_cols(rwkv_w1[j], LANES)), w2=bf(_pad_rows(rwkv_w2[j], LANES)),
                a0=rwkv_a0[j], a1=bf(_pad_cols(rwkv_a1[j], LANES)), a2=bf(_pad_rows(rwkv_a2[j], LANES)),
                g1=bf(rwkv_g1[j]), g2=bf(rwkv_g2[j]),
                v0=rwkv_v0[j - 1] if has_vres else None,
                v1=bf(_pad_cols(rwkv_v1[j - 1], LANES)) if has_vres else None,
                v2=bf(_pad_rows(rwkv_v2[j - 1], LANES)) if has_vres else None,
                k_k=rwkv_k_k[j], k_a=rwkv_k_a[j], r_k=rwkv_r_k[j].reshape(d),
                lnx_w=rwkv_lnx_w[j], lnx_b=rwkv_lnx_b[j])
        else:
            mixer = dict(
                norm=norm_mix[i],
                w_in=bf(gla_w_in[j]), n_main=n_main,
                w_lr=bf(_pad_cols(gla_w_in[j][:, n_main:], LANES)),
                gk_w2=bf(_pad_rows(gla_gk_w2[j], LANES)), gk_b=gla_gk_b[j],
                head_norm=gla_head_norm[j], wo=bf(gla_wo[j]))
        ffn = dict(norm=norm_ffn[i], w_up=bf(ffn_w_up[i]), conv_w=ffn_conv_w[i],
                   conv_b=ffn_conv_b[i],
                   w_down=bf(ffn_w_down[i]))
        layers.append((mixer, ffn))

    bp, tp, _ = x_prompt.shape
    bs, ts, _ = x_sample.shape
    zeros_like_state = lambda s: jnp.zeros((s.shape[0], bp) + s.shape[2:], s.dtype)
    out_p = _run_trunk(x_prompt, zeros_like_state(state_rwkv_shift), zeros_like_state(state_rwkv_wkv),
                       zeros_like_state(state_gla), zeros_like_state(state_ffn_conv), layers,
                       norm_final, *_ffn_tile(bp, tp))
    out_s = _run_trunk(x_sample, state_rwkv_shift, state_rwkv_wkv, state_gla, state_ffn_conv, layers,
                       norm_final, *_ffn_tile(bs, ts))
    return (out_p[0], out_s[0]) + tuple(out_p[1:]) + tuple(out_s[1:])
```

```python
import functools

import jax
import jax.numpy as jnp
from jax import lax
from jax.experimental import pallas as pl
from jax.experimental.pallas import tpu as pltpu

F32 = jnp.float32
BF16 = jnp.bfloat16

CHUNK = 64
RWKV_HEAD = 64
RWKV_GN_EPS = 64e-5
GLA_GATE_RANK = 16
GLA_GATE_NORMALIZER = 16.0
GLA_NORM_EPS = 1e-5
NORM_EPS = 1e-6

LANES = 128
MXU_DIM = 256
HEADS_PER_GROUP = MXU_DIM // RWKV_HEAD
VMEM_LIMIT = 56 * 1024 * 1024


def _params(sem):
    return pltpu.CompilerParams(dimension_semantics=sem, vmem_limit_bytes=VMEM_LIMIT)


def _dot(a, b):
    return jnp.dot(a, b, preferred_element_type=F32)


def _dot_nt(a, b):
    return lax.dot_general(a, b, (((1,), (1,)), ((), ())), preferred_element_type=F32)


def _dot_tn(a, b):
    return lax.dot_general(a, b, (((0,), (0,)), ((), ())), preferred_element_type=F32)


def _split_dot(fixed_bf16, x, terms):
    acc = None
    rem = x
    for _ in range(terms):
        piece = rem.astype(BF16)
        rem = rem - piece.astype(F32)
        part = _dot(fixed_bf16, piece)
        acc = part if acc is None else acc + part
    return acc


def _log_sigmoid(x):
    return jnp.minimum(x, 0.0) - jnp.log(1.0 + jnp.exp(-jnp.abs(x)))


def _sigmoid(x):
    return 1.0 / (1.0 + jnp.exp(-x))


def _silu(x):
    return x * _sigmoid(x)


def _rms(x, eps):
    return x * lax.rsqrt(jnp.mean(x * x, axis=-1, keepdims=True) + eps)


def _tri_incl(n):
    r = lax.broadcasted_iota(jnp.int32, (n, n), 0)
    c = lax.broadcasted_iota(jnp.int32, (n, n), 1)
    return c <= r


def _mm_kernel(a_ref, b_ref, *rest, epilogue, n_extra):
    o_ref = rest[n_extra]
    acc = _dot(a_ref[...], b_ref[...])
    if epilogue is not None:
        acc = epilogue(acc, *[e[...] for e in rest[:n_extra]])
    o_ref[...] = acc.astype(o_ref.dtype)


def _matmul(a, b, extras=(), epilogue=None, out_dtype=F32, tm=1024, tn=1024, name="matmul"):
    m, k = a.shape
    n = b.shape[1]
    tm, tn = min(tm, m), min(tn, n)
    assert m % tm == 0 and n % tn == 0, (m, n, tm, tn)
    in_specs = [pl.BlockSpec((tm, k), lambda i, j: (i, 0)), pl.BlockSpec((k, tn), lambda i, j: (0, j))]
    for e in extras:
        if e.shape[0] == 1:
            in_specs.append(pl.BlockSpec((1, tn), lambda i, j: (0, j)))
        else:
            in_specs.append(pl.BlockSpec((tm, tn), lambda i, j: (i, j)))
    return pl.pallas_call(
        functools.partial(_mm_kernel, epilogue=epilogue, n_extra=len(extras)),
        grid=(m // tm, n // tn),
        in_specs=in_specs,
        out_specs=pl.BlockSpec((tm, tn), lambda i, j: (i, j)),
        out_shape=jax.ShapeDtypeStruct((m, n), out_dtype),
        compiler_params=_params(("arbitrary", "arbitrary")),
        name=name,
    )(a, b, *extras)


def _rwkv_pre_kernel(x_ref, st_ref, g_ref, mix_ref, w1_ref, a1_ref, g1_ref, *rest, has_v1):
    if has_v1:
        v1_ref, rest = rest[0], rest[1:]
    xr_ref, xk_ref, xv_ref, lw_ref, la_ref, lg_ref = rest[:6]
    rest = rest[6:]
    if has_v1:
        lv_ref, rest = rest[0], rest[1:]
    hl_ref, carry_ref = rest
    tt = x_ref.shape[0]

    @pl.when(pl.program_id(1) == 0)
    def _():
        carry_ref[...] = st_ref[...]

    h = _rms(x_ref[...], NORM_EPS) * g_ref[...]
    row = lax.broadcasted_iota(jnp.int32, (tt, 1), 0)
    h_prev = jnp.where(row == 0, carry_ref[...], pltpu.roll(h, 1, axis=0))
    carry_ref[...] = h[tt - 1:tt, :]
    hl_ref[...] = h[tt - 8:, :]
    xx = h_prev - h
    mixed = lambda i: (h + xx * mix_ref[i:i + 1, :]).astype(BF16)
    xr_ref[...] = mixed(0)
    xk_ref[...] = mixed(2)
    xv = mixed(3)
    xv_ref[...] = xv
    lw_ref[...] = jnp.tanh(_dot(mixed(1), w1_ref[...])).astype(BF16)
    la_ref[...] = _dot(mixed(4), a1_ref[...]).astype(BF16)
    lg_ref[...] = _sigmoid(_dot(mixed(5), g1_ref[...])).astype(BF16)
    if has_v1:
        lv_ref[...] = _dot(xv, v1_ref[...]).astype(BF16)


def _rwkv_pre(b, x, shift_state, g, mix, w1, a1, g1, v1):
    m, d = x.shape
    t = m // b
    tt = min(t, 512)
    nt = t // tt
    has_v1 = v1 is not None
    lora = [w1, a1, g1] + ([v1] if has_v1 else [])
    flat_spec = lambda n: pl.BlockSpec((tt, n), lambda i, j: (i * nt + j, 0))
    full_spec = lambda w: pl.BlockSpec(w.shape, lambda i, j: (0, 0))
    wide = [jax.ShapeDtypeStruct((b * t, d), BF16)] * 3
    narrow = [jax.ShapeDtypeStruct((b * t, w.shape[1]), BF16) for w in lora]
    outs = pl.pallas_call(
        functools.partial(_rwkv_pre_kernel, has_v1=has_v1),
        grid=(b, nt),
        in_specs=[flat_spec(d),
                  pl.BlockSpec((None, 1, d), lambda i, j: (i, 0, 0)),
                  pl.BlockSpec((1, d), lambda i, j: (0, 0)),
                  pl.BlockSpec((6, d), lambda i, j: (0, 0))] + [full_spec(w) for w in lora],
        out_specs=[flat_spec(d)] * 3 + [flat_spec(w.shape[1]) for w in lora]
                  + [pl.BlockSpec((None, 8, d), lambda i, j: (i, 0, 0))],
        out_shape=wide + narrow + [jax.ShapeDtypeStruct((b, 8, d), F32)],
        scratch_shapes=[pltpu.VMEM((1, d), F32)],
        compiler_params=_params(("arbitrary", "arbitrary")),
        name="rwkv_pre",
    )(x, shift_state.reshape(b, 1, d), g.reshape(1, d), mix, *lora)
    return outs[:-1], outs[-1][:, 7, :]


def _wkv7_kernel(*refs, has_vres):
    refs = list(refs)
    r_ref, k_ref, v_ref, lw_ref, la_ref, lg_ref = refs[:6]
    w2_ref, a2_ref, g2_ref, w0_ref, a0_ref = refs[6:11]
    refs = refs[11:]
    if has_vres:
        vf_ref, lv_ref, v2_ref, v0_ref = refs[:4]
        refs = refs[4:]
    kk_ref, ka_ref, rk_ref, gnw_ref, gnb_ref, s0_ref, y_ref, s_ref = refs
    c_len = CHUNK
    n_sub, d = r_ref.shape[0] // c_len, r_ref.shape[1]
    n_groups = d // MXU_DIM

    @pl.when(pl.program_id(1) == 0)
    def _():
        s_ref[...] = s0_ref[...]

    ri = lax.broadcasted_iota(jnp.int32, (MXU_DIM, MXU_DIM), 0)
    ci = lax.broadcasted_iota(jnp.int32, (MXU_DIM, MXU_DIM), 1)
    head_bits = RWKV_HEAD.bit_length() - 1
    same_head = (ri >> head_bits) == (ci >> head_bits)
    bd = jnp.where(same_head, 1.0, 0.0)
    bd_b = bd.astype(BF16)
    tri_b = jnp.where(_tri_incl(c_len), 1.0, 0.0).astype(BF16)

    assert c_len == RWKV_HEAD and 2 * RWKV_HEAD == LANES
    t_c = lax.broadcasted_iota(jnp.int32, (MXU_DIM, LANES), 0) & (c_len - 1)
    lane_c = lax.broadcasted_iota(jnp.int32, (MXU_DIM, LANES), 1)
    s_c = lane_c & (RWKV_HEAD - 1)
    low_half = lane_c < RWKV_HEAD
    strict_c = s_c < t_c
    incl_c = s_c <= t_c
    eye_c = jnp.where(s_c == t_c, 1.0, 0.0)
    lane_t = lax.broadcasted_iota(jnp.int32, (c_len, LANES), 1)
    half_masks = (jnp.where(lane_t < RWKV_HEAD, 1.0, 0.0).astype(BF16),
                  jnp.where(lane_t >= RWKV_HEAD, 1.0, 0.0).astype(BF16))

    def stack(x):
        return jnp.concatenate([x] * HEADS_PER_GROUP, axis=0)

    def stack_masked(x):
        return stack(x.astype(BF16)) * bd_b

    def head_sum(x):
        return _dot(x.astype(BF16), bd_b)

    def expand(xc, half):
        zeros = jnp.zeros((c_len, LANES), BF16)
        blocks = []
        for hd in range(HEADS_PER_GROUP):
            src = xc[hd * c_len:(hd + 1) * c_len]
            if hd % 2 != half:
                src = pltpu.roll(src, RWKV_HEAD, axis=1)
            tile = src.astype(BF16) * half_masks[hd % 2]
            blocks.append(jnp.concatenate(
                [tile if lt == hd // 2 else zeros for lt in range(MXU_DIM // LANES)], axis=1))
        return jnp.concatenate(blocks, axis=0)

    groups = range(n_groups)
    sls = [slice(grp * MXU_DIM, (grp + 1) * MXU_DIM) for grp in groups]
    lw_n, la_n, lg_n = lw_ref[...], la_ref[...], lg_ref[...]
    w_all = [_log_sigmoid(w0_ref[:, sl] + _dot(lw_n, w2_ref[:, sl])) - 0.5 for sl in sls]
    a_all = [_sigmoid(a0_ref[:, sl] + _dot(la_n, a2_ref[:, sl])) for sl in sls]
    g_all = [_dot(lg_n, g2_ref[:, sl]) for sl in sls]
    v_all = [v_ref[:, sl] for sl in sls]
    if has_vres:
        lv_n = lv_ref[...]
        mix_v = [_sigmoid(v0_ref[:, sl] + _dot(lv_n, v2_ref[:, sl])) for sl in sls]
        v_all = [v_all[i] + (vf_ref[:, sls[i]] - v_all[i]) * mix_v[i] for i in groups]
    chains = [(c, grp) for c in range(n_sub) for grp in groups]
    rws = [slice(c * c_len, (c + 1) * c_len) for c, _ in chains]
    sls = [sls[grp] for _, grp in chains]
    groups = range(len(chains))
    r = [r_ref[rws[i], sls[i]] for i in groups]
    k = [k_ref[rws[i], sls[i]] for i in groups]
    v = [v_all[chains[i][1]][rws[i]] for i in groups]
    a = [a_all[chains[i][1]][rws[i]] for i in groups]
    g = [g_all[chains[i][1]][rws[i]] for i in groups]
    lw = [-jnp.exp(w_all[chains[i][1]][rws[i]]) for i in groups]
    kk = [k[i] * kk_ref[:, sls[i]] for i in groups]
    kk_ss = [head_sum(x * x) for x in kk]
    cum = [_split_dot(tri_b, x, 2) for x in lw]
    kk = [kk[i] / jnp.maximum(jnp.sqrt(kk_ss[i]), 1e-12) for i in groups]
    kmod = [k[i] * (1.0 + (a[i] - 1.0) * ka_ref[:, sls[i]]) for i in groups]
    kka = [kk[i] * a[i] for i in groups]
    last = [x[c_len - 1:c_len, :] for x in cum]
    e_neg = [jnp.exp(-x) for x in cum]
    ar = [jnp.concatenate([stack_masked(-kk[i] * jnp.exp(cum[i] - lw[i])),
                           stack_masked(r[i] * jnp.exp(cum[i]))], axis=0) for i in groups]
    bk = [jnp.concatenate([kka[i] * e_neg[i], kmod[i] * e_neg[i]], axis=0).astype(BF16) for i in groups]
    z = [_dot_nt(ar[i], bk[i]) for i in groups]
    za = [jnp.where(strict_c, x[:MXU_DIM], 0.0) for x in z]
    zr = [jnp.where(incl_c, x[MXU_DIM:], 0.0) for x in z]
    a_ak = [expand(x, 1) for x in za]
    a_r = [jnp.concatenate([expand(x, 0), expand(x, 1)], axis=1) for x in zr]

    rhs = [jnp.where(low_half, x, eye_c) for x in za]
    p_bd = [expand(x, 0) for x in za]
    n_steps = (c_len - 1).bit_length()
    for step in range(n_steps):
        x2 = [_dot(p_bd[i], rhs[i].astype(BF16)) for i in groups]
        rhs = [x2[i] + jnp.where(low_half, 0.0, rhs[i]) for i in groups]
        if step + 1 < n_steps:
            p_bd = [expand(x, 0) for x in x2]
    tinv_b = [expand(x, 1) for x in rhs]

    v_st = [stack_masked(x) for x in v]
    av = [_dot(a_ak[i], v_st[i]) for i in groups]
    e_last = [jnp.exp(last[i] - cum[i]) for i in groups]
    bk_hat = [jnp.concatenate([stack_masked(kka[i] * e_last[i]), stack_masked(kmod[i] * e_last[i])],
                              axis=0) for i in groups]

    o_st = [None] * len(chains)
    for c in range(n_sub):
        cur = [i for i in groups if chains[i][0] == c]
        s = {i: s_ref[chains[i][1]] for i in cur}
        y = {i: _dot_nt(ar[i], s[i].astype(BF16)) for i in cur}
        u = {i: _dot(tinv_b[i], (y[i][:MXU_DIM] + av[i]).astype(BF16)) for i in cur}
        uv = {i: jnp.concatenate([u[i].astype(BF16), v_st[i]], axis=0) for i in cur}
        for i in cur:
            o_st[i] = y[i][MXU_DIM:] + _dot(a_r[i], uv[i])
        for i in cur:
            s_ref[chains[i][1]] = s[i] * jnp.exp(last[i]) + _dot_tn(uv[i], bk_hat[i])

    o = [functools.reduce(lambda p, q: p + q,
                          [x[hd * c_len:(hd + 1) * c_len] for hd in range(HEADS_PER_GROUP)]) for x in o_st]
    inv_n = 1.0 / RWKV_HEAD
    mu = [head_sum(x) * inv_n for x in o]
    bonus_dot = [head_sum(r[i] * kmod[i] * rk_ref[:, sls[i]]) for i in groups]
    dev = [o[i] - mu[i] for i in groups]
    var = [head_sum(x * x) * inv_n for x in dev]
    for i in groups:
        sl = sls[i]
        o_n = dev[i] * lax.rsqrt(var[i] + RWKV_GN_EPS) * gnw_ref[:, sl] + gnb_ref[:, sl]
        y_ref[rws[i], sl] = ((o_n + bonus_dot[i] * v[i]) * g[i]).astype(y_ref.dtype)


WKV_CHUNKS_PER_STEP = 2


def _wkv7(b, r, k, v, lora, lora_w, lora_b, vres, k_k, k_a, r_k, lnx_w, lnx_b, s0_bd):
    m, d = r.shape
    t = m // b
    blk_rows = min(t, WKV_CHUNKS_PER_STEP * CHUNK)
    n_steps = t // blk_rows
    n_groups = d // MXU_DIM
    rows = lambda n: pl.BlockSpec((blk_rows, n), lambda i, j: (i * n_steps + j, 0))
    whole = lambda w: pl.BlockSpec(w.shape, lambda i, j: (0, 0))
    vec_spec = pl.BlockSpec((1, d), lambda i, j: (0, 0))
    st_spec = pl.BlockSpec((None, n_groups, MXU_DIM, MXU_DIM), lambda i, j: (i, 0, 0, 0))
    vec = lambda p: p.reshape(1, d)
    args = [r, k, v, *lora, *lora_w, *[vec(x) for x in lora_b]]
    specs = ([rows(d)] * 3 + [rows(x.shape[1]) for x in lora] + [whole(w) for w in lora_w]
             + [vec_spec] * len(lora_b))
    if vres is not None:
        v_first, lv, v2, v0 = vres
        args += [v_first, lv, v2, vec(v0)]
        specs += [rows(d), rows(lv.shape[1]), whole(v2), vec_spec]
    args += [vec(k_k), vec(k_a), vec(r_k), vec(lnx_w), vec(lnx_b), s0_bd]
    specs += [vec_spec] * 5 + [st_spec]
    return pl.pallas_call(
        functools.partial(_wkv7_kernel, has_vres=vres is not None),
        grid=(b, n_steps),
        in_specs=specs,
        out_specs=[rows(d), st_spec],
        out_shape=[jax.ShapeDtypeStruct((m, d), BF16),
                   jax.ShapeDtypeStruct(s0_bd.shape, F32)],
        compiler_params=_params(("arbitrary", "arbitrary")),
        name="wkv7",
    )(*args)


def _wkv_state_to_blockdiag(s):
    b, h, n, _ = s.shape
    g = h // HEADS_PER_GROUP
    s = s.reshape(b, g, HEADS_PER_GROUP, n, n)
    row_blocks = [jnp.pad(s[:, :, i], ((0, 0), (0, 0), (0, 0), (i * n, MXU_DIM - (i + 1) * n)))
                  for i in range(HEADS_PER_GROUP)]
    return jnp.concatenate(row_blocks, axis=2)


def _wkv_state_from_blockdiag(s_bd):
    b, g = s_bd.shape[:2]
    n = RWKV_HEAD
    blocks = [s_bd[:, :, i * n:(i + 1) * n, i * n:(i + 1) * n] for i in range(HEADS_PER_GROUP)]
    return jnp.stack(blocks, axis=2).reshape(b, g * HEADS_PER_GROUP, n, n)


def _gla_kernel(q_ref, k_ref, v_ref, gate_ref, lr_ref, w2_ref, gb_ref, hn_ref, s0_ref, y_ref, sn_ref,
                s_ref):
    tc = q_ref.shape[0]
    n_heads, dv, dk = s_ref.shape
    heads = range(n_heads)
    ksl = [slice(h * dk, (h + 1) * dk) for h in heads]
    vsl = [slice(h * dv, (h + 1) * dv) for h in heads]

    @pl.when(pl.program_id(1) == 0)
    def _():
        for h in heads:
            s_ref[h] = s0_ref[h].T

    tri = _tri_incl(CHUNK)
    tri_b = jnp.where(tri, 1.0, 0.0).astype(BF16)
    scale = dk ** -0.5
    for c in range(tc // CHUNK):
        rows = slice(c * CHUNK, (c + 1) * CHUNK)
        lr = lr_ref[rows, :].astype(BF16)
        gk = [_log_sigmoid(_dot(lr, w2_ref[:, sl]) + gb_ref[:, sl]) * (1.0 / GLA_GATE_NORMALIZER)
              for sl in ksl]
        cum = [_split_dot(tri_b, x, 3) for x in gk]
        last = [x[CHUNK - 1:CHUNK, :] for x in cum]
        qe = [(q_ref[rows, ksl[h]] * scale * jnp.exp(cum[h])).astype(BF16) for h in heads]
        ke = [(k_ref[rows, ksl[h]] * jnp.exp(-cum[h])).astype(BF16) for h in heads]
        k2 = [(k_ref[rows, ksl[h]] * jnp.exp(last[h] - cum[h])).astype(BF16) for h in heads]
        vb = [v_ref[rows, sl].astype(BF16) for sl in vsl]
        scores = [jnp.where(tri, _dot_nt(qe[h], ke[h]), 0.0).astype(BF16) for h in heads]
        s_t = [s_ref[h] for h in heads]
        o = [_dot(scores[h], vb[h]) + _dot_nt(qe[h], s_t[h].astype(BF16)) for h in heads]
        for h in heads:
            s_ref[h] = s_t[h] * jnp.exp(last[h]) + _dot_tn(vb[h], k2[h])
        for h in heads:
            o_n = _rms(o[h], GLA_NORM_EPS) * hn_ref[...]
            y_ref[rows, vsl[h]] = (o_n * _silu(gate_ref[rows, vsl[h]])).astype(y_ref.dtype)

    @pl.when(pl.program_id(1) == pl.num_programs(1) - 1)
    def _():
        for h in heads:
            sn_ref[h] = s_ref[h].T


def _gla(b, qkvg, lr, gk_w2, gk_b, head_norm, s0):
    t = qkvg.shape[0] // b
    h, dk, dv = s0.shape[1:]
    tc = min(t, 4 * CHUNK)
    dk_all, dv_all = h * dk, h * dv
    assert dv_all == 2 * dk_all
    st_spec = pl.BlockSpec((None, h, dk, dv), lambda i, c: (i, 0, 0, 0))
    n_steps = t // tc
    rows = lambda n, col: pl.BlockSpec((tc, n), lambda i, c: (i * n_steps + c, col))
    return pl.pallas_call(
        _gla_kernel,
        grid=(b, t // tc),
        in_specs=[rows(dk_all, 0), rows(dk_all, 1), rows(dv_all, 1), rows(dv_all, 2), rows(LANES, 0),
                  pl.BlockSpec((LANES, dk_all), lambda i, c: (0, 0)),
                  pl.BlockSpec((1, dk_all), lambda i, c: (0, 0)),
                  pl.BlockSpec((1, dv), lambda i, c: (0, 0)),
                  st_spec],
        out_specs=[rows(dv_all, 0), st_spec],
        out_shape=[jax.ShapeDtypeStruct((b * t, dv_all), BF16),
                   jax.ShapeDtypeStruct(s0.shape, F32)],
        scratch_shapes=[pltpu.VMEM((h, dv, dk), F32)],
        compiler_params=_params(("arbitrary", "arbitrary")),
        name="gla",
    )(qkvg, qkvg, qkvg, qkvg, lr, gk_w2, gk_b.reshape(1, -1), head_norm.reshape(1, dv), s0)


def _ffn_kernel(x_ref, g_ref, sv_ref, sg_ref, wv_ref, wg_ref, cwv_ref, cwg_ref, cbv_ref, cbg_ref,
                wd_ref, og_ref, o_ref, nsv_ref, nsg_ref, h_ref, acc_ref, slab_ref, ua_ref, ub_ref, carry_ref,
                *, nj, nseq, seq_len, norm_output):
    rows, d = x_ref.shape
    tn = wv_ref.shape[1]
    j = pl.program_id(2)
    row_tile = pl.program_id(0) * pl.num_programs(1) + pl.program_id(1)
    jt = _hidden_tile(jnp.maximum(j - 1, 0), row_tile, nj)

    n_ph = min(8, seq_len // 16)
    q_len = seq_len // n_ph
    phase_rows = lambda p: pl.ds(p, nseq * q_len, stride=n_ph)
    blk = lambda p: slice(p * q_len, (p + 1) * q_len)
    slabs = [slice(c * LANES, (c + 1) * LANES) for c in range(d // LANES)]

    @pl.when(j == 0)
    def _():
        for c, cl in enumerate(slabs):
            slab_ref[c] = x_ref[:, cl]
        for c, cl in enumerate(slabs):
            for p in range(n_ph):
                acc_ref[:, blk(p), cl] = slab_ref[c, phase_rows(p), :].reshape(nseq, q_len, LANES)
        h_ref[...] = (_rms(acc_ref[...], NORM_EPS) * g_ref[...]).astype(BF16)

    @pl.when((pl.program_id(1) == 0) & (j > 0))
    def _():
        carry_ref[2 * jt] = sv_ref[...]
        carry_ref[2 * jt + 1] = sg_ref[...]

    cols = [slice(c * MXU_DIM, (c + 1) * MXU_DIM) for c in range(tn // MXU_DIM)]
    first = lax.broadcasted_iota(jnp.int32, (1, q_len, 1), 1) == 0

    def conv(u, cs, cw_ref, cb_ref, slot, ns_ref):
        u = u.reshape(nseq, seq_len, MXU_DIM)
        w0, w1, w2, cb = cw_ref[0:1, cs], cw_ref[1:2, cs], cw_ref[2:3, cs], cb_ref[:, cs]
        p2, p1 = carry_ref[slot, :, 0:1, cs], carry_ref[slot, :, 1:2, cs]
        back1 = jnp.where(first, p1, pltpu.roll(u[:, blk(n_ph - 1), :], 1, axis=1))
        back2 = jnp.where(first, p2, pltpu.roll(u[:, blk(n_ph - 2), :], 1, axis=1))
        r1 = jnp.concatenate([back1, u[:, :(n_ph - 1) * q_len, :]], axis=1)
        r2 = jnp.concatenate([back2, back1, u[:, :(n_ph - 2) * q_len, :]], axis=1)
        last = seq_len - 1
        before_last = (n_ph - 1) * q_len - 1
        for row, src in ((0, before_last), (1, last)):
            carry_ref[slot, :, row:row + 1, cs] = u[:, src:src + 1, :]
            ns_ref[:, row:row + 1, cs] = u[:, src:src + 1, :]
        return cb + w0 * r2 + w1 * r1 + w2 * u

    def step(up_ref, dn_ref):
        h = None if up_ref is None else h_ref[...].reshape(rows, d)
        acc = None
        for cs in cols:
            if up_ref is not None:
                up_ref[0, :, cs] = _dot(h, wv_ref[:, cs])
            if dn_ref is not None:
                val = conv(dn_ref[0, :, cs], cs, cwv_ref, cbv_ref, 2 * jt, nsv_ref)
            if up_ref is not None:
                up_ref[1, :, cs] = _dot(h, wg_ref[:, cs])
            if dn_ref is not None:
                gate = conv(dn_ref[1, :, cs], cs, cwg_ref, cbg_ref, 2 * jt + 1, nsg_ref)
                act = (_silu(gate) * val).reshape(rows, MXU_DIM).astype(BF16)
                part = _dot(act, wd_ref[cs, :])
                acc = part if acc is None else acc + part
        if dn_ref is not None:
            acc_ref[...] += acc.reshape(nseq, seq_len, d)

    even = (j & 1) == 0
    inner = (j > 0) & (j < nj)

    @pl.when(j == 0)
    def _():
        step(ua_ref, None)

    @pl.when(inner & even)
    def _():
        step(ua_ref, ub_ref)

    @pl.when(inner & jnp.logical_not(even))
    def _():
        step(ub_ref, ua_ref)

    @pl.when(j == nj)
    def _():
        step(None, ua_ref if (nj - 1) % 2 == 0 else ub_ref)
        if norm_output:
            acc_ref[...] = _rms(acc_ref[...], NORM_EPS) * og_ref[...]
        for c, cl in enumerate(slabs):
            for p in range(n_ph):
                slab_ref[c, phase_rows(p), :] = acc_ref[:, blk(p), cl].reshape(nseq * q_len, LANES)
        for c, cl in enumerate(slabs):
            o_ref[:, cl] = slab_ref[c]


FFN_TILE_COLS = 512


def _hidden_tile(step_tile, row_tile, nj):
    return step_tile + (row_tile % 2) * (nj - 1 - 2 * step_tile)


def _conv_ffn(b, x, conv_state, g, w_up, conv_w, conv_b, w_down, out_g, norm_output, nseq, seq_len):
    m, d = x.shape
    t = m // b
    f = w_down.shape[0]
    tn = FFN_TILE_COLS
    nj = f // tn
    assert b % nseq == 0 and t % seq_len == 0 and f % tn == 0
    assert nseq == 1 or seq_len == t
    x_spec = pl.BlockSpec((nseq * seq_len, d), lambda i, s, j: (i * (t // seq_len) + s, 0))
    n_s = t // seq_len
    up_tile = lambda i, s, j: _hidden_tile(jnp.minimum(j, nj - 1), i * n_s + s, nj)
    dn_tile = lambda i, s, j: _hidden_tile(jnp.maximum(j - 1, 0), i * n_s + s, nj)
    st_v = pl.BlockSpec((nseq, 2, tn), lambda i, s, j: (i, 0, dn_tile(i, s, j)))
    st_g = pl.BlockSpec((nseq, 2, tn), lambda i, s, j: (i, 0, nj + dn_tile(i, s, j)))
    tail_spec = pl.BlockSpec((nseq, None, 2, tn), lambda i, s, j: (i, s, 0, dn_tile(i, s, j)))
    col_v = lambda n: pl.BlockSpec((n, tn), lambda i, s, j: (0, dn_tile(i, s, j)))
    col_g = lambda n: pl.BlockSpec((n, tn), lambda i, s, j: (0, nj + dn_tile(i, s, j)))
    y, ns_v, ns_g = pl.pallas_call(
        functools.partial(_ffn_kernel, nj=nj, nseq=nseq, seq_len=seq_len, norm_output=norm_output),
        grid=(b // nseq, n_s, nj + 1),
        in_specs=[x_spec, pl.BlockSpec((1, d), lambda i, s, j: (0, 0)), st_v, st_g,
                  pl.BlockSpec((d, tn), lambda i, s, j: (0, up_tile(i, s, j))),
                  pl.BlockSpec((d, tn), lambda i, s, j: (0, nj + up_tile(i, s, j))),
                  col_v(3), col_g(3), col_v(1), col_g(1),
                  pl.BlockSpec((tn, d), lambda i, s, j: (dn_tile(i, s, j), 0)),
                  pl.BlockSpec((1, d), lambda i, s, j: (0, 0))],
        out_specs=[x_spec, tail_spec, tail_spec],
        out_shape=[jax.ShapeDtypeStruct((m, d), F32),
                   jax.ShapeDtypeStruct((b, n_s, 2, f), F32),
                   jax.ShapeDtypeStruct((b, n_s, 2, f), F32)],
        scratch_shapes=[pltpu.VMEM((nseq, seq_len, d), BF16),
                        pltpu.VMEM((nseq, seq_len, d), F32),
                        pltpu.VMEM((d // LANES, nseq * seq_len, LANES), F32),
                        pltpu.VMEM((2, nseq * seq_len, tn), F32),
                        pltpu.VMEM((2, nseq * seq_len, tn), F32),
                        pltpu.VMEM((2 * nj, nseq, 2, tn), F32)],
        compiler_params=_params(("arbitrary", "arbitrary", "arbitrary")),
        name="conv_ffn",
    )(x, g.reshape(1, d), conv_state, conv_state, w_up, w_up, conv_w, conv_w,
      conv_b.reshape(1, -1), conv_b.reshape(1, -1), w_down, out_g.reshape(1, d))
    return y, jnp.concatenate([ns_v[:, -1], ns_g[:, -1]], axis=-1)


def _pad_cols(w, n):
    return jnp.pad(w, ((0, 0), (0, n - w.shape[1])))


def _pad_rows(w, n):
    return jnp.pad(w, ((0, n - w.shape[0]), (0, 0)))


def _rwkv_layer(b, x, shift_st, wkv_st, v_first, p):
    pre, new_shift = _rwkv_pre(b, x, shift_st, p["norm"], p["mix"], p["w1"], p["a1"], p["g1"], p["v1"])
    xr, xk, xv = pre[:3]
    r = _matmul(xr, p["wr"], name="rwkv_r")
    k = _matmul(xk, p["wk"], name="rwkv_k")
    v = _matmul(xv, p["wv"], name="rwkv_v")
    if p["v1"] is None:
        vres, v_first = None, v
    else:
        vres = (v_first, pre[6], p["v2"], p["v0"])
    y, s_bd = _wkv7(b, r, k, v, pre[3:6], (p["w2"], p["a2"], p["g2"]), (p["w0"], p["a0"]), vres,
                    p["k_k"], p["k_a"], p["r_k"], p["lnx_w"], p["lnx_b"],
                    _wkv_state_to_blockdiag(wkv_st))
    x_new = _matmul(y, p["wo"], extras=(x,), epilogue=lambda acc, res: res + acc, name="rwkv_o")
    return x_new, new_shift, _wkv_state_from_blockdiag(s_bd), v_first


def _gla_in_kernel(x_ref, g_ref, w_ref, wlr_ref, o_ref, lr_ref, h_ref):
    @pl.when(pl.program_id(1) == 0)
    def _():
        h = (_rms(x_ref[...], NORM_EPS) * g_ref[...]).astype(BF16)
        h_ref[...] = h
        lr_ref[...] = _dot(h, wlr_ref[...])

    o_ref[...] = _dot(h_ref[...], w_ref[...])


def _gla_in(x, g, w_in, n_main, w_lr, tm=1024, tn=1024):
    m, d = x.shape
    tm = min(tm, m)
    assert m % tm == 0 and n_main % tn == 0
    return pl.pallas_call(
        _gla_in_kernel,
        grid=(m // tm, n_main // tn),
        in_specs=[pl.BlockSpec((tm, d), lambda i, j: (i, 0)),
                  pl.BlockSpec((1, d), lambda i, j: (0, 0)),
                  pl.BlockSpec((d, tn), lambda i, j: (0, j)),
                  pl.BlockSpec(w_lr.shape, lambda i, j: (0, 0))],
        out_specs=[pl.BlockSpec((tm, tn), lambda i, j: (i, j)),
                   pl.BlockSpec((tm, w_lr.shape[1]), lambda i, j: (i, 0))],
        out_shape=[jax.ShapeDtypeStruct((m, n_main), F32),
                   jax.ShapeDtypeStruct((m, w_lr.shape[1]), F32)],
        scratch_shapes=[pltpu.VMEM((tm, d), BF16)],
        compiler_params=_params(("arbitrary", "arbitrary")),
        name="gla_in",
    )(x, g.reshape(1, d), w_in, w_lr)


def _gla_layer(b, x, gla_st, p):
    qkvg, lr = _gla_in(x, p["norm"], p["w_in"], p["n_main"], p["w_lr"])
    y, s_new = _gla(b, qkvg, lr, p["gk_w2"], p["gk_b"], p["head_norm"], gla_st)
    x_new = _matmul(y, p["wo"], extras=(x,), epilogue=lambda acc, res: res + acc, name="gla_o")
    return x_new, s_new


FFN_TILE_ROWS = 512


def _ffn_tile(b, t):
    if t >= FFN_TILE_ROWS:
        return 1, FFN_TILE_ROWS
    return min(b, FFN_TILE_ROWS // t), t


def _run_trunk(x, shift_st, wkv_st, gla_st, conv_st, layers, norm_final, ffn_nseq, ffn_len):
    new_shift, new_wkv, new_gla, new_conv = [], [], [], []
    v_first = None
    b, t, d = x.shape
    x = x.reshape(b * t, d)
    for i, (mixer, ffn) in enumerate(layers):
        j = i // 2
        if i % 2 == 0:
            x, s_shift, s_wkv, v_first = _rwkv_layer(b, x, shift_st[j], wkv_st[j], v_first, mixer)
            new_shift.append(s_shift)
            new_wkv.append(s_wkv)
        else:
            x, s_gla = _gla_layer(b, x, gla_st[j], mixer)
            new_gla.append(s_gla)
        x, s_conv = _conv_ffn(b, x, conv_st[i], ffn["norm"], ffn["w_up"], ffn["conv_w"], ffn["conv_b"],
                              ffn["w_down"], norm_final, i == len(layers) - 1, ffn_nseq, ffn_len)
        new_conv.append(s_conv)
    return x.reshape(b, t, d), jnp.stack(new_shift), jnp.stack(new_wkv), jnp.stack(new_gla), jnp.stack(new_conv)


def kernel(x_prompt, x_sample, state_rwkv_shift, state_rwkv_wkv, state_gla, state_ffn_conv, norm_mix, norm_ffn, norm_final, rwkv_mix, rwkv_w0, rwkv_w1, rwkv_w2, rwkv_a0, rwkv_a1, rwkv_a2, rwkv_v0, rwkv_v1, rwkv_v2, rwkv_g1, rwkv_g2, rwkv_k_k, rwkv_k_a, rwkv_r_k, rwkv_wr, rwkv_wk, rwkv_wv, rwkv_wo, rwkv_lnx_w, rwkv_lnx_b, gla_w_in, gla_gk_w2, gla_gk_b, gla_head_norm, gla_wo, ffn_w_up, ffn_conv_w, ffn_conv_b, ffn_w_down):
    depth = norm_mix.shape[0]
    d = x_prompt.shape[-1]
    bf = lambda w: w.astype(BF16)
    n_main = gla_w_in.shape[-1] - GLA_GATE_RANK
    layers = []
    for i in range(depth):
        j = i // 2
        if i % 2 == 0:
            has_vres = j > 0
            mixer = dict(
                norm=norm_mix[i], mix=rwkv_mix[j],
                wr=bf(rwkv_wr[j]), wk=bf(rwkv_wk[j]), wv=bf(rwkv_wv[j]), wo=bf(rwkv_wo[j]),
                w0=rwkv_w0[j], w1=bf(_pad_cols(rwkv_w1[j], LANES)), w2=bf(_pad_rows(rwkv_w2[j], LANES)),
                a0=rwkv_a0[j], a1=bf(_pad_cols(rwkv_a1[j], LANES)), a2=bf(_pad_rows(rwkv_a2[j], LANES)),
                g1=bf(rwkv_g1[j]), g2=bf(rwkv_g2[j]),
                v0=rwkv_v0[j - 1] if has_vres else None,
                v1=bf(_pad_cols(rwkv_v1[j - 1], LANES)) if has_vres else None,
                v2=bf(_pad_rows(rwkv_v2[j - 1], LANES)) if has_vres else None,
                k_k=rwkv_k_k[j], k_a=rwkv_k_a[j], r_k=rwkv_r_k[j].reshape(d),
                lnx_w=rwkv_lnx_w[j], lnx_b=rwkv_lnx_b[j])
        else:
            mixer = dict(
                norm=norm_mix[i],
                w_in=bf(gla_w_in[j]), n_main=n_main,
                w_lr=bf(_pad_cols(gla_w_in[j][:, n_main:], LANES)),
                gk_w2=bf(_pad_rows(gla_gk_w2[j], LANES)), gk_b=gla_gk_b[j],
                head_norm=gla_head_norm[j], wo=bf(gla_wo[j]))
        ffn = dict(norm=norm_ffn[i], w_up=bf(ffn_w_up[i]), conv_w=ffn_conv_w[i],
                   conv_b=ffn_conv_b[i],
                   w_down=bf(ffn_w_down[i]))
        layers.append((mixer, ffn))

    bp, tp, _ = x_prompt.shape
    bs, ts, _ = x_sample.shape
    zeros_like_state = lambda s: jnp.zeros((s.shape[0], bp) + s.shape[2:], s.dtype)
    out_p = _run_trunk(x_prompt, zeros_like_state(state_rwkv_shift), zeros_like_state(state_rwkv_wkv),
                       zeros_like_state(state_gla), zeros_like_state(state_ffn_conv), layers,
                       norm_final, *_ffn_tile(bp, tp))
    out_s = _run_trunk(x_sample, state_rwkv_shift, state_rwkv_wkv, state_gla, state_ffn_conv, layers,
                       norm_final, *_ffn_tile(bs, ts))
    return (out_p[0], out_s[0]) + tuple(out_p[1:]) + tuple(out_s[1:])
```
